```python
import jax, jax.numpy as jnp
from jax import lax
import numpy as np

D_MODEL = 1024
BATCH = 8
SEQ = 4096
DEPTH = 1

HEAD_DIM = 64
DIL_PAIRS = ((128, 1), (512, 4), (2048, 16))
DIL_HEADS_PER_GROUP = 4
DIL_HEADS = DIL_HEADS_PER_GROUP * len(DIL_PAIRS)
DIL_WIDTH = DIL_HEADS * HEAD_DIM
DIL_OUT_WIDTH = DIL_HEADS_PER_GROUP * HEAD_DIM
SWA_WINDOW = 128
SWA_Q_HEADS = 8
SWA_KV_HEADS = 2
SWA_Q_WIDTH = SWA_Q_HEADS * HEAD_DIM
SWA_KV_WIDTH = SWA_KV_HEADS * HEAD_DIM
ROPE_THETA = 500000.0
ROPE_DIM = HEAD_DIM // 4
D_FF = 4 * D_MODEL
BLOCK = 128
EPS = 1e-6
NEG = -1e30
IN_SIZES = (DIL_WIDTH, DIL_WIDTH, DIL_WIDTH, SWA_Q_WIDTH, SWA_KV_WIDTH, SWA_KV_WIDTH, D_MODEL, D_MODEL)
IN_WIDTH = sum(IN_SIZES)

kernel_name = "hybrid_dilated_swa_sink_gated_block"


def rmsnorm(x, g):
    xf = x.astype(jnp.float32)
    y = xf * lax.rsqrt(jnp.mean(xf * xf, axis=-1, keepdims=True) + EPS)
    return (y * g.astype(jnp.float32)).astype(x.dtype)


def rope_tables(positions):
    inv_freq = ROPE_THETA ** (-jnp.arange(0, ROPE_DIM, 2, dtype=jnp.float32) / ROPE_DIM)
    ang = positions.astype(jnp.float32)[..., None] * inv_freq
    return jnp.cos(ang)[:, :, None, :], jnp.sin(ang)[:, :, None, :]


def apply_partial_rope(x, cos, sin):
    half = ROPE_DIM // 2
    xr = x[..., :ROPE_DIM].astype(jnp.float32)
    x1, x2 = xr[..., :half], xr[..., half:]
    rot = jnp.concatenate([x1 * cos - x2 * sin, x2 * cos + x1 * sin], axis=-1).astype(x.dtype)
    return jnp.concatenate([rot, x[..., ROPE_DIM:]], axis=-1)


def banded_attention(q, k, v, max_dist, sinks=None):
    B, N, L, H, Dh = q.shape
    Hkv = k.shape[3]
    G = H // Hkv
    Lp = -(-L // BLOCK) * BLOCK
    pad = Lp - L
    if pad:
        cfg = ((0, 0), (0, 0), (0, pad), (0, 0), (0, 0))
        q, k, v = jnp.pad(q, cfg), jnp.pad(k, cfg), jnp.pad(v, cfg)
    nb = Lp // BLOCK
    qb = q.reshape(B, N, nb, BLOCK, Hkv, G, Dh).astype(jnp.float32)

    def band(t):
        tp = jnp.pad(t, ((0, 0), (0, 0), (BLOCK, 0), (0, 0), (0, 0)))
        tb = tp.reshape(B, N, nb + 1, BLOCK, Hkv, Dh)
        return jnp.concatenate([tb[:, :, :-1], tb[:, :, 1:]], axis=3)

    kb = band(k).astype(jnp.float32)
    vb = band(v).astype(jnp.float32)
    scale = 1.0 / np.sqrt(Dh).astype(np.float32)
    s = jnp.einsum('bnjqhgd,bnjkhd->bnjhgqk', qb, kb) * scale
    blk = jnp.arange(nb)[:, None, None]
    qpos = blk * BLOCK + jnp.arange(BLOCK)[None, :, None]
    kpos = (blk - 1) * BLOCK + jnp.arange(2 * BLOCK)[None, None, :]
    dist = qpos - kpos
    mask = (dist >= 0) & (dist <= max_dist) & (kpos >= 0)
    s = jnp.where(mask[:, None, None, :, :], s, jnp.float32(NEG))
    m = jnp.max(s, axis=-1, keepdims=True)
    if sinks is not None:
        sk = sinks.astype(jnp.float32).reshape(Hkv, G)[:, :, None, None]
        m = jnp.maximum(m, sk)
        p = jnp.exp(s - m)
        denom = jnp.sum(p, axis=-1, keepdims=True) + jnp.exp(sk - m)
    else:
        p = jnp.exp(s - m)
        denom = jnp.sum(p, axis=-1, keepdims=True)
    o = jnp.einsum('bnjhgqk,bnjkhd->bnjqhgd', p / denom, vb)
    o = o.reshape(B, N, Lp, H, Dh)[:, :, :L].astype(v.dtype)
    lse = (m + jnp.log(denom))[..., 0]
    lse = lse.transpose(0, 1, 2, 5, 3, 4).reshape(B, N, Lp, H)[:, :, :L]
    return o, lse


def dilated_attention(q, k, v):
    B, S, _, Dh = q.shape
    outs, lses = [], []
    for g, (w, d) in enumerate(DIL_PAIRS):
        lo, hi = g * DIL_HEADS_PER_GROUP, (g + 1) * DIL_HEADS_PER_GROUP

        def strided(t):
            return t[:, :, lo:hi].reshape(B, S // d, d, DIL_HEADS_PER_GROUP, Dh).transpose(0, 2, 1, 3, 4)

        o, lse = banded_attention(strided(q), strided(k), strided(v), w // d)
        outs.append(o.transpose(0, 2, 1, 3, 4).reshape(B, S, DIL_HEADS_PER_GROUP, Dh))
        lses.append(lse.transpose(0, 2, 1, 3).reshape(B, S, DIL_HEADS_PER_GROUP))
    alpha = jax.nn.softmax(jnp.stack(lses, axis=0), axis=0)
    o = jnp.sum(alpha[..., None] * jnp.stack(outs, axis=0).astype(jnp.float32), axis=0)
    return o.reshape(B, S, DIL_OUT_WIDTH).astype(q.dtype)


def setup_inputs(seed: int = 0) -> dict:
    key = jax.random.key(seed)
    ks = jax.random.split(key, 16)
    f32 = jnp.float32

    def w(k, shape, fan_in):
        return jax.random.normal(k, shape, f32) * (fan_in ** -0.5)

    def gain(k, shape):
        return 1.0 + 0.02 * jax.random.normal(k, shape, f32)

    x = jax.random.normal(ks[0], (BATCH, SEQ, D_MODEL), f32)
    offset = jax.random.randint(ks[1], (BATCH, 1), 0, 1024, dtype=jnp.int32)
    positions = (offset + jnp.arange(SEQ, dtype=jnp.int32)[None, :]).astype(jnp.int32)
    return {
        "x": x,
        "positions": positions,
        "ln1_g": gain(ks[2], (DEPTH, D_MODEL)),
        "w_in": w(ks[3], (DEPTH, D_MODEL, IN_WIDTH), D_MODEL),
        "q_norm_a": gain(ks[4], (DEPTH, HEAD_DIM)),
        "k_norm_a": gain(ks[5], (DEPTH, HEAD_DIM)),
        "q_norm_b": gain(ks[6], (DEPTH, HEAD_DIM)),
        "k_norm_b": gain(ks[7], (DEPTH, HEAD_DIM)),
        "sinks": 0.5 * jax.random.normal(ks[8], (DEPTH, SWA_Q_HEADS), f32),
        "w_branch_a": w(ks[9], (DEPTH, DIL_OUT_WIDTH, D_MODEL), DIL_OUT_WIDTH),
        "w_branch_b": w(ks[10], (DEPTH, SWA_Q_WIDTH, D_MODEL), SWA_Q_WIDTH),
        "w_out": w(ks[11], (DEPTH, D_MODEL, D_MODEL), D_MODEL),
        "ln2_g": gain(ks[12], (DEPTH, D_MODEL)),
        "w_up": w(ks[13], (DEPTH, D_MODEL, D_FF), D_MODEL),
        "w_down": w(ks[14], (DEPTH, D_FF, D_MODEL), D_FF),
    }


def reference(x, positions, ln1_g, w_in, q_norm_a, k_norm_a, q_norm_b, k_norm_b, sinks,
              w_branch_a, w_branch_b, w_out, ln2_g, w_up, w_down):
    B, S, _ = x.shape
    cos, sin = rope_tables(positions)
    offsets = np.cumsum(np.array(IN_SIZES))[:-1].tolist()
    for l in range(DEPTH):
        h = rmsnorm(x, ln1_g[l])
        proj = h @ w_in[l]
        qa, ka, va, qb, kb, vb, ga, gb = jnp.split(proj, offsets, axis=-1)
        qa = apply_partial_rope(rmsnorm(qa.reshape(B, S, DIL_HEADS, HEAD_DIM), q_norm_a[l]), cos, sin)
        ka = apply_partial_rope(rmsnorm(ka.reshape(B, S, DIL_HEADS, HEAD_DIM), k_norm_a[l]), cos, sin)
        va = va.reshape(B, S, DIL_HEADS, HEAD_DIM)
        oa = dilated_attention(qa, ka, va)
        qb = apply_partial_rope(rmsnorm(qb.reshape(B, S, SWA_Q_HEADS, HEAD_DIM), q_norm_b[l]), cos, sin)
        kb = apply_partial_rope(rmsnorm(kb.reshape(B, S, SWA_KV_HEADS, HEAD_DIM), k_norm_b[l]), cos, sin)
        vb = vb.reshape(B, S, SWA_KV_HEADS, HEAD_DIM)
        ob, _ = banded_attention(qb[:, None], kb[:, None], vb[:, None], SWA_WINDOW - 1, sinks[l])
        ob = ob.reshape(B, S, SWA_Q_WIDTH)
        mix = jax.nn.sigmoid(ga) * (oa @ w_branch_a[l]) + jax.nn.sigmoid(gb) * (ob @ w_branch_b[l])
        x = x + mix @ w_out[l]
        h2 = rmsnorm(x, ln2_g[l])
        x = x + jnp.square(jax.nn.relu(h2 @ w_up[l])) @ w_down[l]
    return x
```

```python
import functools

import numpy as np
import jax
import jax.numpy as jnp
from jax import lax
from jax.experimental import pallas as pl
from jax.experimental.pallas import tpu as pltpu

D_MODEL = 1024
HEAD_DIM = 64
DIL_PAIRS = ((128, 1), (512, 4), (2048, 16))
DIL_GROUP_WIDTH = 256
DIL_WIDTH = 768
SWA_WINDOW = 128
SWA_Q_WIDTH = 512
SWA_KV_WIDTH = 128
QKV_A_WIDTH = 3 * DIL_WIDTH
QKV_B_WIDTH = SWA_Q_WIDTH + 2 * SWA_KV_WIDTH
GATE_WIDTH = 2 * D_MODEL
D_FF = 4 * D_MODEL
ROPE_THETA = 500000.0
ROPE_DIM = HEAD_DIM // 4
ROPE_HALF = ROPE_DIM // 2
BLOCK = 128
EPS = 1e-6
NEG = -1e30

LANES = 128
CHUNK = 256
TOKEN_TILE = 512
ATTN_TILE = 512
VMEM_LIMIT_BYTES = 56 * 1024 * 1024

_BF16 = jnp.bfloat16
_F32 = jnp.float32


def _resident(shape):
    return pl.BlockSpec(shape, lambda *_: (0,) * len(shape), pipeline_mode=pl.Buffered(1))


def _in_proj_kernel(x_ref, ln_ref, w_ref, gain_ref, cos_ref, sin_ref, seg_ref, qkva_ref, qkvb_ref, gate_ref):
    x = x_ref[...]
    ms = jnp.mean(x * x, axis=-1, keepdims=True)
    h = (x * lax.rsqrt(ms + EPS) * ln_ref[...]).astype(_BF16)

    cos_t = cos_ref[...]
    sin_t = sin_ref[...]
    lane = lax.broadcasted_iota(jnp.int32, cos_t.shape, 1) % HEAD_DIM
    first_half = lane < ROPE_HALF
    sin_from_hi = jnp.where(first_half, -sin_t, 0.0)
    sin_from_lo = jnp.where(first_half, 0.0, sin_t)
    seg = seg_ref[...]

    def norm_rope(y, gain):
        ss = jnp.dot((y * y).astype(_BF16), seg, preferred_element_type=_F32)
        yn = y * lax.rsqrt(ss * (1.0 / HEAD_DIM) + EPS) * gain
        up = pltpu.roll(yn, LANES - ROPE_HALF, 1)
        dn = pltpu.roll(yn, ROPE_HALF, 1)
        return yn * cos_t + up * sin_from_hi + dn * sin_from_lo

    def emit(out_ref, w_col, n_cols, gain_of_col):
        for c in range(n_cols // CHUNK):
            y = jnp.dot(h, w_ref[:, w_col + c * CHUNK:w_col + (c + 1) * CHUNK], preferred_element_type=_F32)
            for half in range(CHUNK // LANES):
                col = c * CHUNK + half * LANES
                yh = y[:, half * LANES:(half + 1) * LANES]
                gain_idx = gain_of_col(col)
                if gain_idx is not None:
                    yh = norm_rope(yh, gain_ref[gain_idx])
                out_ref[:, col:col + LANES] = yh.astype(out_ref.dtype)

    emit(qkva_ref, 0, QKV_A_WIDTH, lambda col: 0 if col < DIL_WIDTH else (1 if col < 2 * DIL_WIDTH else None))
    emit(qkvb_ref, QKV_A_WIDTH, QKV_B_WIDTH,
         lambda col: 2 if col < SWA_Q_WIDTH else (3 if col < SWA_Q_WIDTH + SWA_KV_WIDTH else None))
    emit(gate_ref, QKV_A_WIDTH + QKV_B_WIDTH, GATE_WIDTH, lambda col: None)


def _in_proj(x2, ln1, w_in, gains, cos_t, sin_t, seg):
    n = x2.shape[0]
    tm = TOKEN_TILE
    row = lambda width: pl.BlockSpec((tm, width), lambda i: (i, 0))
    widths = (QKV_A_WIDTH, QKV_B_WIDTH, GATE_WIDTH)
    return pl.pallas_call(
        _in_proj_kernel,
        grid=(n // tm,),
        in_specs=[row(D_MODEL), _resident((1, D_MODEL)), _resident(w_in.shape), _resident(gains.shape),
                  row(LANES), row(LANES), _resident(seg.shape)],
        out_specs=[row(w) for w in widths],
        out_shape=[jax.ShapeDtypeStruct((n, w), _BF16) for w in widths],
        compiler_params=pltpu.CompilerParams(dimension_semantics=("arbitrary",), vmem_limit_bytes=VMEM_LIMIT_BYTES),
        name="in_proj",
    )(x2, ln1, w_in, gains, cos_t, sin_t, seg)


def _band_mask(max_dist, first_key_col):
    row = lax.broadcasted_iota(jnp.int32, (BLOCK, 2 * BLOCK), 0)
    col = lax.broadcasted_iota(jnp.int32, (BLOCK, 2 * BLOCK), 1)
    dist = row + BLOCK - col
    return (dist >= 0) & (dist <= max_dist) & (col >= first_key_col)


def _softmax_block(s, mask, sink=None):
    s = jnp.where(mask, s, NEG)
    m = jnp.max(s, axis=1, keepdims=True)
    if sink is not None:
        m = jnp.maximum(m, sink)
    p = jnp.exp(s - m)
    denom = jnp.sum(p, axis=1, keepdims=True)
    if sink is not None:
        denom = denom + jnp.exp(sink - m)
    return p, m, denom


_NT = (((1,), (1,)), ((), ()))


def _key_blocks(qb, j, kprev_ref, k_ref, vprev_ref, v_ref, max_dist):
    rows = slice(qb * BLOCK, (qb + 1) * BLOCK)
    if qb == 0:
        k_prev, v_prev = kprev_ref[...], vprev_ref[...]
        mask = _band_mask(max_dist, jnp.where(j == 0, BLOCK, 0))
    else:
        prev_rows = slice((qb - 1) * BLOCK, qb * BLOCK)
        k_prev, v_prev = k_ref[prev_rows, :], v_ref[prev_rows, :]
        mask = _band_mask(max_dist, 0)
    k = jnp.concatenate([k_prev, k_ref[rows, :]], axis=0)
    v = jnp.concatenate([v_prev, v_ref[rows, :]], axis=0)
    return rows, k, v, mask


def _dilated_kernel(q_ref, kprev_ref, k_ref, vprev_ref, v_ref, o_ref, lse_ref, *, max_dist, n_blocks):
    j = pl.program_id(2)
    head_of_lane = lax.broadcasted_iota(jnp.int32, (BLOCK, DIL_GROUP_WIDTH), 1) // HEAD_DIM
    for qb in range(n_blocks):
        rows, k, v, mask = _key_blocks(qb, j, kprev_ref, k_ref, vprev_ref, v_ref, max_dist)
        q = q_ref[rows, :]
        o_acc = jnp.zeros((BLOCK, DIL_GROUP_WIDTH), _F32)
        lse_acc = jnp.zeros((BLOCK, DIL_GROUP_WIDTH), _F32)
        for hd in range(DIL_GROUP_WIDTH // HEAD_DIM):
            mine = head_of_lane == hd
            qh = jnp.where(mine, q, jnp.zeros_like(q))
            s = lax.dot_general(qh, k, _NT, preferred_element_type=_F32)
            p, m, denom = _softmax_block(s, mask)
            pv = jnp.dot(p.astype(_BF16), v, preferred_element_type=_F32)
            o_acc = jnp.where(mine, pv / denom, o_acc)
            lse_acc = jnp.where(mine, m + jnp.log(denom), lse_acc)
        o_ref[rows, :] = o_acc.astype(o_ref.dtype)
        lse_ref[rows, :] = lse_acc


def _dilated_attention(qkva3, group, window, dilation):
    b, s, width = qkva3.shape
    d = dilation
    length = s // d
    tile = min(ATTN_TILE, length)
    n_blocks = tile // BLOCK
    view = qkva3.reshape(b, length, d * width)
    per_row = width // DIL_GROUP_WIDTH
    n_groups = DIL_WIDTH // DIL_GROUP_WIDTH
    q_col, k_col, v_col = group, n_groups + group, 2 * n_groups + group

    def cur(col):
        return pl.BlockSpec((None, tile, DIL_GROUP_WIDTH), lambda bi, r, j: (bi, j, r * per_row + col))

    def prev(col):
        return pl.BlockSpec((None, BLOCK, DIL_GROUP_WIDTH),
                            lambda bi, r, j: (bi, jnp.maximum(j * n_blocks - 1, 0), r * per_row + col))

    out_spec = pl.BlockSpec((None, tile, DIL_GROUP_WIDTH), lambda bi, r, j: (bi, j, r))
    o, lse = pl.pallas_call(
        functools.partial(_dilated_kernel, max_dist=window // d, n_blocks=n_blocks),
        grid=(b, d, length // tile),
        in_specs=[cur(q_col), prev(k_col), cur(k_col), prev(v_col), cur(v_col)],
        out_specs=[out_spec, out_spec],
        out_shape=[jax.ShapeDtypeStruct((b, length, d * DIL_GROUP_WIDTH), _BF16),
                   jax.ShapeDtypeStruct((b, length, d * DIL_GROUP_WIDTH), _F32)],
        compiler_params=pltpu.CompilerParams(dimension_semantics=("arbitrary",) * 3, vmem_limit_bytes=VMEM_LIMIT_BYTES),
        name=f"dilated_attn_d{d}",
    )(view, view, view, view, view)
    return o.reshape(b * s, DIL_GROUP_WIDTH), lse.reshape(b * s, DIL_GROUP_WIDTH)


def _swa_kernel(sink_ref, q_ref, kprev_ref, k_ref, vprev_ref, v_ref, o_ref, *, n_blocks):
    j = pl.program_id(1)
    low = lax.broadcasted_iota(jnp.int32, (BLOCK, LANES), 1) < HEAD_DIM
    kv_low = lax.broadcasted_iota(jnp.int32, (2 * BLOCK, LANES), 1) < HEAD_DIM
    n_q_heads = SWA_Q_WIDTH // HEAD_DIM
    n_kv_heads = SWA_KV_WIDTH // HEAD_DIM
    group = n_q_heads // n_kv_heads

    def both_halves(t, kv):
        t32 = t.astype(_F32)
        swapped = pltpu.roll(t32, HEAD_DIM, 1)
        keep = kv_low if kv == 0 else jnp.logical_not(kv_low)
        return jnp.where(keep, t32, swapped).astype(_BF16)

    for qb in range(n_blocks):
        rows, k, v, mask = _key_blocks(qb, j, kprev_ref, k_ref, vprev_ref, v_ref, SWA_WINDOW - 1)
        k_dup = [both_halves(k, kv) for kv in range(n_kv_heads)]
        v_dup = [both_halves(v, kv) for kv in range(n_kv_heads)]
        for pair in range(n_q_heads // 2):
            q_pair = q_ref[rows, pair * LANES:(pair + 1) * LANES]
            o_pair = jnp.zeros((BLOCK, LANES), _F32)
            for sub in range(2):
                hd = 2 * pair + sub
                kv = hd // group
                mine = low if sub == 0 else jnp.logical_not(low)
                qh = jnp.where(mine, q_pair, jnp.zeros_like(q_pair))
                s = lax.dot_general(qh, k_dup[kv], _NT, preferred_element_type=_F32)
                p, _, denom = _softmax_block(s, mask, sink=sink_ref[hd])
                pv = jnp.dot(p.astype(_BF16), v_dup[kv], preferred_element_type=_F32)
                o_pair = jnp.where(mine, pv / denom, o_pair)
            o_ref[rows, pair * LANES:(pair + 1) * LANES] = o_pair.astype(o_ref.dtype)


def _swa_attention(qkvb3, sinks):
    b, s, _ = qkvb3.shape
    tile = ATTN_TILE
    n_blocks = tile // BLOCK
    k_col = SWA_Q_WIDTH // LANES
    v_col = k_col + SWA_KV_WIDTH // LANES

    def cur(col):
        return pl.BlockSpec((None, tile, LANES), lambda bi, j: (bi, j, col))

    def prev(col):
        return pl.BlockSpec((None, BLOCK, LANES), lambda bi, j: (bi, jnp.maximum(j * n_blocks - 1, 0), col))

    q_spec = pl.BlockSpec((None, tile, SWA_Q_WIDTH), lambda bi, j: (bi, j, 0))
    o = pl.pallas_call(
        functools.partial(_swa_kernel, n_blocks=n_blocks),
        grid=(b, s // tile),
        in_specs=[pl.BlockSpec(memory_space=pltpu.SMEM), q_spec, prev(k_col), cur(k_col), prev(v_col), cur(v_col)],
        out_specs=q_spec,
        out_shape=jax.ShapeDtypeStruct((b, s, SWA_Q_WIDTH), _BF16),
        compiler_params=pltpu.CompilerParams(dimension_semantics=("arbitrary",) * 2, vmem_limit_bytes=VMEM_LIMIT_BYTES),
        name="swa_attn",
    )(sinks, qkvb3, qkvb3, qkvb3, qkvb3, qkvb3)
    return o.reshape(b * s, SWA_Q_WIDTH)


def _out_mlp_kernel(x_ref, o0_ref, o1_ref, o2_ref, l0_ref, l1_ref, l2_ref, ob_ref, gate_ref,
                    wa_ref, wb_ref, wo_ref, ln_ref, wup_ref, wdn_ref, out_ref):
    l0, l1, l2 = l0_ref[...], l1_ref[...], l2_ref[...]
    m = jnp.maximum(jnp.maximum(l0, l1), l2)
    e0, e1, e2 = jnp.exp(l0 - m), jnp.exp(l1 - m), jnp.exp(l2 - m)
    num = e0 * o0_ref[...].astype(_F32) + e1 * o1_ref[...].astype(_F32) + e2 * o2_ref[...].astype(_F32)
    oa = (num / (e0 + e1 + e2)).astype(_BF16)

    ya = jnp.dot(oa, wa_ref[...], preferred_element_type=_F32)
    yb = jnp.dot(ob_ref[...], wb_ref[...], preferred_element_type=_F32)
    ga = gate_ref[:, :D_MODEL].astype(_F32)
    gb = gate_ref[:, D_MODEL:].astype(_F32)
    mix = (jax.nn.sigmoid(ga) * ya + jax.nn.sigmoid(gb) * yb).astype(_BF16)
    x1 = x_ref[...] + jnp.dot(mix, wo_ref[...], preferred_element_type=_F32)

    ms = jnp.mean(x1 * x1, axis=-1, keepdims=True)
    h2 = (x1 * lax.rsqrt(ms + EPS) * ln_ref[...]).astype(_BF16)
    ff_chunk = 512
    acc = x1
    for c in range(D_FF // ff_chunk):
        u = jnp.dot(h2, wup_ref[:, c * ff_chunk:(c + 1) * ff_chunk], preferred_element_type=_F32)
        a = jnp.square(jnp.maximum(u, 0.0)).astype(_BF16)
        acc = acc + jnp.dot(a, wdn_ref[c * ff_chunk:(c + 1) * ff_chunk, :], preferred_element_type=_F32)
    out_ref[...] = acc


def _out_mlp(x2, o_groups, lse_groups, ob, gates, wa, wb, wo, ln2, wup, wdn):
    n = x2.shape[0]
    tm = TOKEN_TILE
    row = lambda width: pl.BlockSpec((tm, width), lambda i: (i, 0))
    return pl.pallas_call(
        _out_mlp_kernel,
        grid=(n // tm,),
        in_specs=[row(D_MODEL)] + [row(DIL_GROUP_WIDTH)] * 6 + [row(SWA_Q_WIDTH), row(GATE_WIDTH),
                  _resident(wa.shape), _resident(wb.shape), _resident(wo.shape), _resident((1, D_MODEL)),
                  _resident(wup.shape), _resident(wdn.shape)],
        out_specs=row(D_MODEL),
        out_shape=jax.ShapeDtypeStruct((n, D_MODEL), _F32),
        compiler_params=pltpu.CompilerParams(dimension_semantics=("arbitrary",), vmem_limit_bytes=VMEM_LIMIT_BYTES),
        name="out_mlp",
    )(x2, *o_groups, *lse_groups, ob, gates, wa, wb, wo, ln2, wup, wdn)


def _rope_lane_tables(positions):
    inv_freq = ROPE_THETA ** (-jnp.arange(0, ROPE_DIM, 2, dtype=_F32) / ROPE_DIM)
    lane = np.arange(LANES) % HEAD_DIM
    rotary = jnp.asarray(lane < ROPE_DIM)
    freq = jnp.where(rotary, inv_freq[lane % ROPE_HALF], 0.0)
    ang = positions.reshape(-1, 1).astype(_F32) * freq[None, :]
    return jnp.where(rotary, jnp.cos(ang), 1.0), jnp.where(rotary, jnp.sin(ang), 0.0)


def kernel(x, positions, ln1_g, w_in, q_norm_a, k_norm_a, q_norm_b, k_norm_b, sinks,
           w_branch_a, w_branch_b, w_out, ln2_g, w_up, w_down):
    b, s, d_model = x.shape
    assert d_model == D_MODEL and ln1_g.shape[0] == 1 and s % (16 * BLOCK) == 0 and (b * s) % TOKEN_TILE == 0
    x2 = x.reshape(b * s, d_model)
    cos_t, sin_t = _rope_lane_tables(positions)
    scale = 1.0 / np.sqrt(HEAD_DIM).astype(np.float32)
    gains = jnp.stack([jnp.tile(q_norm_a[0] * scale, 2), jnp.tile(k_norm_a[0], 2),
                       jnp.tile(q_norm_b[0] * scale, 2), jnp.tile(k_norm_b[0], 2)])[:, None, :]
    head = np.arange(LANES) // HEAD_DIM
    seg = jnp.asarray(head[:, None] == head[None, :], _BF16)

    qkva, qkvb, gates = _in_proj(x2, ln1_g, w_in[0].astype(_BF16), gains, cos_t, sin_t, seg)
    qkva3 = qkva.reshape(b, s, QKV_A_WIDTH)
    o_groups, lse_groups = [], []
    for g, (window, dilation) in enumerate(DIL_PAIRS):
        o, lse = _dilated_attention(qkva3, g, window, dilation)
        o_groups.append(o)
        lse_groups.append(lse)
    ob = _swa_attention(qkvb.reshape(b, s, QKV_B_WIDTH), sinks[0])
    out = _out_mlp(x2, o_groups, lse_groups, ob, gates,
                   w_branch_a[0].astype(_BF16), w_branch_b[0].astype(_BF16), w_out[0].astype(_BF16),
                   ln2_g, w_up[0].astype(_BF16), w_down[0].astype(_BF16))
    return out.reshape(b, s, d_model)
```

```python
import functools

import numpy as np
import jax
import jax.numpy as jnp
from jax import lax
from jax.experimental import pallas as pl
from jax.experimental.pallas import tpu as pltpu

D_MODEL = 1024
HEAD_DIM = 64
DIL_PAIRS = ((128, 1), (512, 4), (2048, 16))
GROUP_HEADS = 4
GROUP_WIDTH = GROUP_HEADS * HEAD_DIM
DIL_WIDTH = 768
N_DIL_GROUPS = DIL_WIDTH // GROUP_WIDTH
GROUP_QKV_WIDTH = 3 * GROUP_WIDTH
SWA_WINDOW = 128
SWA_Q_WIDTH = 512
SWA_KV_WIDTH = 128
QKV_A_WIDTH = 3 * DIL_WIDTH
QKV_B_WIDTH = SWA_Q_WIDTH + 2 * SWA_KV_WIDTH
GATE_WIDTH = 2 * D_MODEL
D_FF = 4 * D_MODEL
ROPE_THETA = 500000.0
ROPE_DIM = HEAD_DIM // 4
ROPE_HALF = ROPE_DIM // 2
BLOCK = 128
EPS = 1e-6
NEG = -1e30

LANES = 128
CHUNK = 256
TOKEN_TILE = 512
ATTN_UNITS = 8
VMEM_LIMIT_BYTES = 56 * 1024 * 1024

_BF16 = jnp.bfloat16
_F32 = jnp.float32


def _resident(shape):
    return pl.BlockSpec(shape, lambda *_: (0,) * len(shape), pipeline_mode=pl.Buffered(1))


def _in_proj_kernel(x_ref, ln_ref, w_ref, gain_ref, cos_ref, sin_ref, seg_ref,
                    qkv0_ref, qkv1_ref, qkv2_ref, qkvb_ref, gate_ref, slab1_ref, slab2_ref):
    tm = x_ref.shape[0]
    x = x_ref[...]
    ms = jnp.mean(x * x, axis=-1, keepdims=True)
    h = (x * lax.rsqrt(ms + EPS) * ln_ref[...]).astype(_BF16)

    cos_t = cos_ref[...]
    sin_t = sin_ref[...]
    lane = lax.broadcasted_iota(jnp.int32, cos_t.shape, 1) % HEAD_DIM
    first_half = lane < ROPE_HALF
    sin_from_hi = jnp.where(first_half, -sin_t, 0.0)
    sin_from_lo = jnp.where(first_half, 0.0, sin_t)
    seg = seg_ref[...]

    def norm_rope(y, gain):
        ss = jnp.dot((y * y).astype(_BF16), seg, preferred_element_type=_F32)
        yn = y * lax.rsqrt(ss * (1.0 / HEAD_DIM) + EPS) * gain
        up = pltpu.roll(yn, LANES - ROPE_HALF, 1)
        dn = pltpu.roll(yn, ROPE_HALF, 1)
        return yn * cos_t + up * sin_from_hi + dn * sin_from_lo

    def halves(w_col):
        y = jnp.dot(h, w_ref[:, w_col:w_col + CHUNK], preferred_element_type=_F32)
        return [y[:, i * LANES:(i + 1) * LANES] for i in range(CHUNK // LANES)]

    group_out = (qkv0_ref, slab1_ref, slab2_ref)
    for part in range(3):
        for g in range(N_DIL_GROUPS):
            for i, yh in enumerate(halves(part * DIL_WIDTH + g * GROUP_WIDTH)):
                if part < 2:
                    yh = norm_rope(yh, gain_ref[part])
                slab = part * (GROUP_WIDTH // LANES) + i
                if g == 0:
                    qkv0_ref[:, slab * LANES:(slab + 1) * LANES] = yh.astype(_BF16)
                else:
                    group_out[g][slab] = yh
    for g, out_ref, slab_ref in ((1, qkv1_ref, slab1_ref), (2, qkv2_ref, slab2_ref)):
        d = DIL_PAIRS[g][1]
        for r in range(d):
            for slab in range(GROUP_QKV_WIDTH // LANES):
                rows = slab_ref[slab, pl.ds(r, tm // d, stride=d), :]
                out_ref[r, :, slab * LANES:(slab + 1) * LANES] = rows.astype(_BF16)

    for c in range(QKV_B_WIDTH // CHUNK):
        for i, yh in enumerate(halves(QKV_A_WIDTH + c * CHUNK)):
            col = c * CHUNK + i * LANES
            if col < SWA_Q_WIDTH:
                yh = norm_rope(yh, gain_ref[2])
            elif col < SWA_Q_WIDTH + SWA_KV_WIDTH:
                yh = norm_rope(yh, gain_ref[3])
            qkvb_ref[:, col:col + LANES] = yh.astype(_BF16)
    for c in range(GATE_WIDTH // CHUNK):
        for i, yh in enumerate(halves(QKV_A_WIDTH + QKV_B_WIDTH + c * CHUNK)):
            col = c * CHUNK + i * LANES
            gate_ref[:, col:col + LANES] = yh.astype(_BF16)


def _in_proj(x2, ln1, w_in, gains, cos_t, sin_t, seg, batch, seq):
    n = x2.shape[0]
    tm = TOKEN_TILE
    tiles_per_seq = seq // tm
    row = lambda width: pl.BlockSpec((tm, width), lambda i: (i, 0))

    def deinterleaved(d):
        return pl.BlockSpec((None, d, tm // d, GROUP_QKV_WIDTH),
                            lambda i: (i // tiles_per_seq, 0, i % tiles_per_seq, 0))

    d1, d2 = DIL_PAIRS[1][1], DIL_PAIRS[2][1]
    n_slabs = GROUP_QKV_WIDTH // LANES
    return pl.pallas_call(
        _in_proj_kernel,
        grid=(n // tm,),
        in_specs=[row(D_MODEL), _resident((1, D_MODEL)), _resident(w_in.shape), _resident(gains.shape),
                  row(LANES), row(LANES), _resident(seg.shape)],
        out_specs=[row(GROUP_QKV_WIDTH), deinterleaved(d1), deinterleaved(d2), row(QKV_B_WIDTH), row(GATE_WIDTH)],
        out_shape=[jax.ShapeDtypeStruct((n, GROUP_QKV_WIDTH), _BF16),
                   jax.ShapeDtypeStruct((batch, d1, seq // d1, GROUP_QKV_WIDTH), _BF16),
                   jax.ShapeDtypeStruct((batch, d2, seq // d2, GROUP_QKV_WIDTH), _BF16),
                   jax.ShapeDtypeStruct((n, QKV_B_WIDTH), _BF16),
                   jax.ShapeDtypeStruct((n, GATE_WIDTH), _BF16)],
        scratch_shapes=[pltpu.VMEM((n_slabs, tm, LANES), _F32), pltpu.VMEM((n_slabs, tm, LANES), _F32)],
        compiler_params=pltpu.CompilerParams(dimension_semantics=("arbitrary",), vmem_limit_bytes=VMEM_LIMIT_BYTES),
        name="in_proj",
    )(x2, ln1, w_in, gains, cos_t, sin_t, seg)


_NT = (((1,), (1,)), ((), ()))
_STACK = GROUP_HEADS * BLOCK


def _stacked_scores(q, k_win):
    head_of_lane = lax.broadcasted_iota(jnp.int32, q.shape, 1) // HEAD_DIM
    zero = jnp.zeros_like(q)
    q_stack = jnp.concatenate([jnp.where(head_of_lane == hd, q, zero) for hd in range(GROUP_HEADS)], axis=0)
    return lax.dot_general(q_stack, k_win, _NT, preferred_element_type=_F32)


def _band_mask(max_dist, first_key_col):
    row = lax.broadcasted_iota(jnp.int32, (_STACK, 2 * BLOCK), 0) % BLOCK
    col = lax.broadcasted_iota(jnp.int32, (_STACK, 2 * BLOCK), 1)
    dist = row + BLOCK - col
    return (dist >= 0) & (dist <= max_dist) & (col >= first_key_col)


def _attend(s, mask, v_win, sink_col=None):
    s = jnp.where(mask, s, NEG)
    m = jnp.max(s, axis=1, keepdims=True)
    if sink_col is not None:
        m = jnp.maximum(m, sink_col)
    p = jnp.exp(s - m)
    denom = jnp.sum(p, axis=1, keepdims=True)
    if sink_col is not None:
        denom = denom + jnp.exp(sink_col - m)
    pv = jnp.dot(p.astype(_BF16), v_win, preferred_element_type=_F32)
    o_stack = pv / denom
    lse_stack = m + jnp.log(denom)
    head_of_lane = lax.broadcasted_iota(jnp.int32, (BLOCK, GROUP_WIDTH), 1) // HEAD_DIM
    o = o_stack[:BLOCK]
    lse = jnp.broadcast_to(lse_stack[:BLOCK], (BLOCK, GROUP_WIDTH))
    for hd in range(1, GROUP_HEADS):
        rows = slice(hd * BLOCK, (hd + 1) * BLOCK)
        mine = head_of_lane == hd
        o = jnp.where(mine, o_stack[rows], o)
        lse = jnp.where(mine, lse_stack[rows], lse)
    return o, lse


def _pipelined(units, scores_fn, finish_fn):
    s_next = scores_fn(units[0])
    for i, unit in enumerate(units):
        s = s_next
        if i + 1 < len(units):
            s_next = scores_fn(units[i + 1])
        finish_fn(unit, s)


def _window(qb, prev_ref, cur_ref, seq=None):
    idx = (lambda rows: (rows, slice(None))) if seq is None else (lambda rows: (seq, rows, slice(None)))
    rows = slice(qb * BLOCK, (qb + 1) * BLOCK)
    if qb == 0:
        prev = prev_ref[idx(slice(0, BLOCK))]
    else:
        prev = cur_ref[idx(slice((qb - 1) * BLOCK, qb * BLOCK))]
    return jnp.concatenate([prev, cur_ref[idx(rows)]], axis=0)


def _dilated_kernel(q_ref, kprev_ref, k_ref, vprev_ref, v_ref, o_ref, lse_ref, *, max_dist):
    n_seqs, tile, _ = q_ref.shape
    j = pl.program_id(1)
    units = [(seq, qb) for seq in range(n_seqs) for qb in range(tile // BLOCK)]

    def scores(unit):
        seq, qb = unit
        return _stacked_scores(q_ref[seq, qb * BLOCK:(qb + 1) * BLOCK, :], _window(qb, kprev_ref, k_ref, seq))

    def finish(unit, s):
        seq, qb = unit
        first_key_col = jnp.where(j == 0, BLOCK, 0) if qb == 0 else 0
        o, lse = _attend(s, _band_mask(max_dist, first_key_col), _window(qb, vprev_ref, v_ref, seq))
        o_ref[seq, qb * BLOCK:(qb + 1) * BLOCK, :] = o.astype(o_ref.dtype)
        lse_ref[seq, qb * BLOCK:(qb + 1) * BLOCK, :] = lse

    _pipelined(units, scores, finish)


def _dilated_attention(qkv, window, dilation):
    n_seqs, length, _ = qkv.shape
    tile = min(ATTN_UNITS * BLOCK, length)
    seqs = ATTN_UNITS * BLOCK // tile
    n_blocks = tile // BLOCK

    def cur(col):
        return pl.BlockSpec((seqs, tile, GROUP_WIDTH), lambda i, j: (i, j, col))

    def prev(col):
        return pl.BlockSpec((seqs, BLOCK, GROUP_WIDTH), lambda i, j: (i, jnp.maximum(j * n_blocks - 1, 0), col))

    out_spec = pl.BlockSpec((seqs, tile, GROUP_WIDTH), lambda i, j: (i, j, 0))
    return pl.pallas_call(
        functools.partial(_dilated_kernel, max_dist=window // dilation),
        grid=(n_seqs // seqs, length // tile),
        in_specs=[cur(0), prev(1), cur(1), prev(2), cur(2)],
        out_specs=[out_spec, out_spec],
        out_shape=[jax.ShapeDtypeStruct((n_seqs, length, GROUP_WIDTH), _BF16),
                   jax.ShapeDtypeStruct((n_seqs, length, GROUP_WIDTH), _F32)],
        compiler_params=pltpu.CompilerParams(dimension_semantics=("arbitrary",) * 2, vmem_limit_bytes=VMEM_LIMIT_BYTES),
        name=f"dilated_attn_d{dilation}",
    )(qkv, qkv, qkv, qkv, qkv)


def _swa_kernel(sink_ref, q_ref, kprev_ref, k_ref, vprev_ref, v_ref, o_ref):
    tile = q_ref.shape[0]
    j = pl.program_id(1)
    n_kv_heads = SWA_KV_WIDTH // HEAD_DIM
    kv_low = lax.broadcasted_iota(jnp.int32, (2 * BLOCK, LANES), 1) < HEAD_DIM
    units = [(qb, kv) for qb in range(tile // BLOCK) for kv in range(n_kv_heads)]

    def replicated(t, kv):
        t32 = t.astype(_F32)
        swapped = pltpu.roll(t32, HEAD_DIM, 1)
        keep = kv_low if kv == 0 else jnp.logical_not(kv_low)
        pair = jnp.where(keep, t32, swapped).astype(_BF16)
        return jnp.concatenate([pair, pair], axis=1)

    def scores(unit):
        qb, kv = unit
        q = q_ref[qb * BLOCK:(qb + 1) * BLOCK, kv * GROUP_WIDTH:(kv + 1) * GROUP_WIDTH]
        return _stacked_scores(q, replicated(_window(qb, kprev_ref, k_ref), kv))

    def finish(unit, s):
        qb, kv = unit
        first_key_col = jnp.where(j == 0, BLOCK, 0) if qb == 0 else 0
        sink_col = jnp.concatenate([jnp.full((BLOCK, 1), sink_ref[kv * GROUP_HEADS + hd], _F32)
                                    for hd in range(GROUP_HEADS)], axis=0)
        o, _ = _attend(s, _band_mask(SWA_WINDOW - 1, first_key_col),
                       replicated(_window(qb, vprev_ref, v_ref), kv), sink_col)
        o_ref[qb * BLOCK:(qb + 1) * BLOCK, kv * GROUP_WIDTH:(kv + 1) * GROUP_WIDTH] = o.astype(o_ref.dtype)

    _pipelined(units, scores, finish)


def _swa_attention(qkvb3, sinks):
    b, s, _ = qkvb3.shape
    tile = ATTN_UNITS * BLOCK // 2
    n_blocks = tile // BLOCK
    k_col = SWA_Q_WIDTH // LANES
    v_col = k_col + SWA_KV_WIDTH // LANES

    def cur(col):
        return pl.BlockSpec((None, tile, LANES), lambda bi, j: (bi, j, col))

    def prev(col):
        return pl.BlockSpec((None, BLOCK, LANES), lambda bi, j: (bi, jnp.maximum(j * n_blocks - 1, 0), col))

    q_spec = pl.BlockSpec((None, tile, SWA_Q_WIDTH), lambda bi, j: (bi, j, 0))
    o = pl.pallas_call(
        _swa_kernel,
        grid=(b, s // tile),
        in_specs=[pl.BlockSpec(memory_space=pltpu.SMEM), q_spec, prev(k_col), cur(k_col), prev(v_col), cur(v_col)],
        out_specs=q_spec,
        out_shape=jax.ShapeDtypeStruct((b, s, SWA_Q_WIDTH), _BF16),
        compiler_params=pltpu.CompilerParams(dimension_semantics=("arbitrary",) * 2, vmem_limit_bytes=VMEM_LIMIT_BYTES),
        name="swa_attn",
    )(sinks, qkvb3, qkvb3, qkvb3, qkvb3, qkvb3)
    return o.reshape(b * s, SWA_Q_WIDTH)


def _out_mlp_kernel(x_ref, o0_ref, l0_ref, o1_ref, l1_ref, o2_ref, l2_ref, ob_ref, gate_ref,
                    wa_ref, wb_ref, wo_ref, ln_ref, wup_ref, wdn_ref, out_ref, slab_ref):
    tm = x_ref.shape[0]
    n_slabs = GROUP_WIDTH // LANES

    def interleaved(src_ref, base):
        d = src_ref.shape[0]
        for r in range(d):
            for slab in range(n_slabs):
                slab_ref[base + slab, pl.ds(r, tm // d, stride=d), :] = (
                    src_ref[r, :, slab * LANES:(slab + 1) * LANES].astype(_F32))
        return jnp.concatenate([slab_ref[base + slab] for slab in range(n_slabs)], axis=1)

    o0, l0 = o0_ref[...].astype(_F32), l0_ref[...]
    o1, l1 = interleaved(o1_ref, 0), interleaved(l1_ref, n_slabs)
    o2, l2 = interleaved(o2_ref, 2 * n_slabs), interleaved(l2_ref, 3 * n_slabs)
    m = jnp.maximum(jnp.maximum(l0, l1), l2)
    e0, e1, e2 = jnp.exp(l0 - m), jnp.exp(l1 - m), jnp.exp(l2 - m)
    oa = ((e0 * o0 + e1 * o1 + e2 * o2) / (e0 + e1 + e2)).astype(_BF16)

    ya = jnp.dot(oa, wa_ref[...], preferred_element_type=_F32)
    yb = jnp.dot(ob_ref[...], wb_ref[...], preferred_element_type=_F32)
    ga = gate_ref[:, :D_MODEL].astype(_F32)
    gb = gate_ref[:, D_MODEL:].astype(_F32)
    mix = (jax.nn.sigmoid(ga) * ya + jax.nn.sigmoid(gb) * yb).astype(_BF16)
    x1 = x_ref[...] + jnp.dot(mix, wo_ref[...], preferred_element_type=_F32)

    ms = jnp.mean(x1 * x1, axis=-1, keepdims=True)
    h2 = (x1 * lax.rsqrt(ms + EPS) * ln_ref[...]).astype(_BF16)
    ff_chunk = 512
    acc = x1
    for c in range(D_FF // ff_chunk):
        u = jnp.dot(h2, wup_ref[:, c * ff_chunk:(c + 1) * ff_chunk], preferred_element_type=_F32)
        a = jnp.square(jnp.maximum(u, 0.0)).astype(_BF16)
        acc = acc + jnp.dot(a, wdn_ref[c * ff_chunk:(c + 1) * ff_chunk, :], preferred_element_type=_F32)
    out_ref[...] = acc


def _out_mlp(x2, attn_a, ob, gates, wa, wb, wo, ln2, wup, wdn, seq):
    n = x2.shape[0]
    tm = TOKEN_TILE
    tiles_per_seq = seq // tm
    row = lambda width: pl.BlockSpec((tm, width), lambda i: (i, 0))

    def deinterleaved(d):
        return pl.BlockSpec((None, d, tm // d, GROUP_WIDTH), lambda i: (i // tiles_per_seq, 0, i % tiles_per_seq, 0))

    group_specs = [row(GROUP_WIDTH)] * 2
    for _, d in DIL_PAIRS[1:]:
        group_specs += [deinterleaved(d)] * 2
    return pl.pallas_call(
        _out_mlp_kernel,
        grid=(n // tm,),
        in_specs=[row(D_MODEL)] + group_specs + [row(SWA_Q_WIDTH), row(GATE_WIDTH),
                  _resident(wa.shape), _resident(wb.shape), _resident(wo.shape), _resident((1, D_MODEL)),
                  _resident(wup.shape), _resident(wdn.shape)],
        out_specs=row(D_MODEL),
        out_shape=jax.ShapeDtypeStruct((n, D_MODEL), _F32),
        scratch_shapes=[pltpu.VMEM((4 * GROUP_WIDTH // LANES, tm, LANES), _F32)],
        compiler_params=pltpu.CompilerParams(dimension_semantics=("arbitrary",), vmem_limit_bytes=VMEM_LIMIT_BYTES),
        name="out_mlp",
    )(x2, *[a for pair in attn_a for a in pair], ob, gates, wa, wb, wo, ln2, wup, wdn)


def _rope_lane_tables(positions):
    inv_freq = ROPE_THETA ** (-jnp.arange(0, ROPE_DIM, 2, dtype=_F32) / ROPE_DIM)
    lane = np.arange(LANES) % HEAD_DIM
    rotary = jnp.asarray(lane < ROPE_DIM)
    freq = jnp.where(rotary, inv_freq[lane % ROPE_HALF], 0.0)
    ang = positions.reshape(-1, 1).astype(_F32) * freq[None, :]
    return jnp.where(rotary, jnp.cos(ang), 1.0), jnp.where(rotary, jnp.sin(ang), 0.0)


def kernel(x, positions, ln1_g, w_in, q_norm_a, k_norm_a, q_norm_b, k_norm_b, sinks,
           w_branch_a, w_branch_b, w_out, ln2_g, w_up, w_down):
    b, s, d_model = x.shape
    assert d_model == D_MODEL and ln1_g.shape[0] == 1 and s % (16 * BLOCK) == 0 and s % TOKEN_TILE == 0
    x2 = x.reshape(b * s, d_model)
    cos_t, sin_t = _rope_lane_tables(positions)
    scale = 1.0 / np.sqrt(HEAD_DIM).astype(np.float32)
    gains = jnp.stack([jnp.tile(q_norm_a[0] * scale, 2), jnp.tile(k_norm_a[0], 2),
                       jnp.tile(q_norm_b[0] * scale, 2), jnp.tile(k_norm_b[0], 2)])[:, None, :]
    head = np.arange(LANES) // HEAD_DIM
    seg = jnp.asarray(head[:, None] == head[None, :], _BF16)

    qkv0, qkv1, qkv2, qkvb, gates = _in_proj(x2, ln1_g, w_in[0].astype(_BF16), gains, cos_t, sin_t, seg, b, s)
    attn_a = []
    for (window, d), qkv in zip(DIL_PAIRS, (qkv0.reshape(b, s, GROUP_QKV_WIDTH), qkv1, qkv2)):
        o, lse = _dilated_attention(qkv.reshape(b * d, s // d, GROUP_QKV_WIDTH), window, d)
        shape = (b * s, GROUP_WIDTH) if d == 1 else (b, d, s // d, GROUP_WIDTH)
        attn_a.append((o.reshape(shape), lse.reshape(shape)))
    ob = _swa_attention(qkvb.reshape(b, s, QKV_B_WIDTH), sinks[0])
    out = _out_mlp(x2, attn_a, ob, gates,
                   w_branch_a[0].astype(_BF16), w_branch_b[0].astype(_BF16), w_out[0].astype(_BF16),
                   ln2_g, w_up[0].astype(_BF16), w_down[0].astype(_BF16), s)
    return out.reshape(b, s, d_model)
```

```python
import functools

import numpy as np
import jax
import jax.numpy as jnp
from jax import lax
from jax.experimental import pallas as pl
from jax.experimental.pallas import tpu as pltpu

D_MODEL = 1024
HEAD_DIM = 64
DIL_PAIRS = ((128, 1), (512, 4), (2048, 16))
GROUP_HEADS = 4
GROUP_WIDTH = GROUP_HEADS * HEAD_DIM
DIL_WIDTH = 768
N_DIL_GROUPS = DIL_WIDTH // GROUP_WIDTH
GROUP_QKV_WIDTH = 3 * GROUP_WIDTH
SWA_WINDOW = 128
SWA_Q_WIDTH = 512
SWA_KV_WIDTH = 128
QKV_A_WIDTH = 3 * DIL_WIDTH
QKV_B_WIDTH = SWA_Q_WIDTH + 2 * SWA_KV_WIDTH
GATE_WIDTH = 2 * D_MODEL
D_FF = 4 * D_MODEL
ROPE_THETA = 500000.0
ROPE_DIM = HEAD_DIM // 4
ROPE_HALF = ROPE_DIM // 2
BLOCK = 128
EPS = 1e-6
NEG = -1e30

LANES = 128
CHUNK = 256
TOKEN_TILE = 512
PROJ_LOOKAHEAD = 2
ATTN_UNITS = 8
VMEM_LIMIT_BYTES = 56 * 1024 * 1024

_BF16 = jnp.bfloat16
_F32 = jnp.float32


def _resident(shape):
    return pl.BlockSpec(shape, lambda *_: (0,) * len(shape), pipeline_mode=pl.Buffered(1))


def _in_proj_kernel(x_ref, ln_ref, w_ref, gain_ref, cos_ref, sin_ref, seg_ref,
                    qkv0_ref, qkv1_ref, qkv2_ref, qkvb_ref, gate_ref, slab1_ref, slab2_ref):
    tm = x_ref.shape[0]
    x = x_ref[...]
    ms = jnp.mean(x * x, axis=-1, keepdims=True)
    h = (x * lax.rsqrt(ms + EPS) * ln_ref[...]).astype(_BF16)

    cos_t = cos_ref[...]
    sin_t = sin_ref[...]
    lane = lax.broadcasted_iota(jnp.int32, cos_t.shape, 1) % HEAD_DIM
    first_half = lane < ROPE_HALF
    sin_from_hi = jnp.where(first_half, -sin_t, 0.0)
    sin_from_lo = jnp.where(first_half, 0.0, sin_t)
    seg = seg_ref[...]

    def norm_rope(y, gain):
        ss = jnp.dot((y * y).astype(_BF16), seg, preferred_element_type=_F32)
        yn = y * lax.rsqrt(ss * (1.0 / HEAD_DIM) + EPS) * gain
        up = pltpu.roll(yn, LANES - ROPE_HALF, 1)
        dn = pltpu.roll(yn, ROPE_HALF, 1)
        return yn * cos_t + up * sin_from_hi + dn * sin_from_lo

    def project(w_col):
        return jnp.dot(h, w_ref[:, w_col:w_col + CHUNK], preferred_element_type=_F32)

    tasks = []
    group_out = (qkv0_ref, slab1_ref, slab2_ref)

    def dilated_store(g, slab):
        def store(yh):
            if g == 0:
                qkv0_ref[:, slab * LANES:(slab + 1) * LANES] = yh.astype(_BF16)
            else:
                group_out[g][slab] = yh
        return store

    def row_store(out_ref, col):
        def store(yh):
            out_ref[:, col:col + LANES] = yh.astype(_BF16)
        return store

    n_halves = CHUNK // LANES
    for part in range(3):
        for g in range(N_DIL_GROUPS):
            tasks.append((part * DIL_WIDTH + g * GROUP_WIDTH,
                          [(part if part < 2 else None, dilated_store(g, part * n_halves + i))
                           for i in range(n_halves)]))
    for c in range(QKV_B_WIDTH // CHUNK):
        cols = [c * CHUNK + i * LANES for i in range(n_halves)]
        tasks.append((QKV_A_WIDTH + c * CHUNK,
                      [(2 if col < SWA_Q_WIDTH else (3 if col < SWA_Q_WIDTH + SWA_KV_WIDTH else None),
                        row_store(qkvb_ref, col)) for col in cols]))
    for c in range(GATE_WIDTH // CHUNK):
        tasks.append((QKV_A_WIDTH + QKV_B_WIDTH + c * CHUNK,
                      [(None, row_store(gate_ref, c * CHUNK + i * LANES)) for i in range(n_halves)]))

    def epilogue(y, stores):
        for i, (gain_idx, store) in enumerate(stores):
            yh = y[:, i * LANES:(i + 1) * LANES]
            store(yh if gain_idx is None else norm_rope(yh, gain_ref[gain_idx]))

    in_flight = []
    for w_col, stores in tasks:
        in_flight.append((project(w_col), stores))
        if len(in_flight) > PROJ_LOOKAHEAD:
            epilogue(*in_flight.pop(0))
    for y, stores in in_flight:
        epilogue(y, stores)

    for g, out_ref, slab_ref in ((1, qkv1_ref, slab1_ref), (2, qkv2_ref, slab2_ref)):
        d = DIL_PAIRS[g][1]
        for r in range(d):
            for slab in range(GROUP_QKV_WIDTH // LANES):
                rows = slab_ref[slab, pl.ds(r, tm // d, stride=d), :]
                out_ref[r, :, slab * LANES:(slab + 1) * LANES] = rows.astype(_BF16)


def _in_proj(x2, ln1, w_in, gains, cos_t, sin_t, seg, batch, seq):
    n = x2.shape[0]
    tm = TOKEN_TILE
    tiles_per_seq = seq // tm
    row = lambda width: pl.BlockSpec((tm, width), lambda i: (i, 0))

    def deinterleaved(d):
        return pl.BlockSpec((None, d, tm // d, GROUP_QKV_WIDTH),
                            lambda i: (i // tiles_per_seq, 0, i % tiles_per_seq, 0))

    d1, d2 = DIL_PAIRS[1][1], DIL_PAIRS[2][1]
    n_slabs = GROUP_QKV_WIDTH // LANES
    return pl.pallas_call(
        _in_proj_kernel,
        grid=(n // tm,),
        in_specs=[row(D_MODEL), _resident((1, D_MODEL)), _resident(w_in.shape), _resident(gains.shape),
                  row(LANES), row(LANES), _resident(seg.shape)],
        out_specs=[row(GROUP_QKV_WIDTH), deinterleaved(d1), deinterleaved(d2), row(QKV_B_WIDTH), row(GATE_WIDTH)],
        out_shape=[jax.ShapeDtypeStruct((n, GROUP_QKV_WIDTH), _BF16),
                   jax.ShapeDtypeStruct((batch, d1, seq // d1, GROUP_QKV_WIDTH), _BF16),
                   jax.ShapeDtypeStruct((batch, d2, seq // d2, GROUP_QKV_WIDTH), _BF16),
                   jax.ShapeDtypeStruct((n, QKV_B_WIDTH), _BF16),
                   jax.ShapeDtypeStruct((n, GATE_WIDTH), _BF16)],
        scratch_shapes=[pltpu.VMEM((n_slabs, tm, LANES), _F32), pltpu.VMEM((n_slabs, tm, LANES), _F32)],
        compiler_params=pltpu.CompilerParams(dimension_semantics=("arbitrary",), vmem_limit_bytes=VMEM_LIMIT_BYTES),
        name="in_proj",
    )(x2, ln1, w_in, gains, cos_t, sin_t, seg)


_NT = (((1,), (1,)), ((), ()))
_STACK = GROUP_HEADS * BLOCK


def _stacked_scores(q, k_win):
    head_of_lane = lax.broadcasted_iota(jnp.int32, q.shape, 1) // HEAD_DIM
    zero = jnp.zeros_like(q)
    q_stack = jnp.concatenate([jnp.where(head_of_lane == hd, q, zero) for hd in range(GROUP_HEADS)], axis=0)
    return lax.dot_general(q_stack, k_win, _NT, preferred_element_type=_F32)


def _band_bias(max_dist, first_key_col):
    row = lax.broadcasted_iota(jnp.int32, (_STACK, 2 * BLOCK), 0) % BLOCK
    col = lax.broadcasted_iota(jnp.int32, (_STACK, 2 * BLOCK), 1)
    dist = row + BLOCK - col
    return jnp.where((dist >= 0) & (dist <= max_dist) & (col >= first_key_col), 0.0, NEG).astype(_F32)


def _head_rows(hd):
    return slice(hd * BLOCK, (hd + 1) * BLOCK)


def _softmax2(s, bias):
    s = s + bias
    m = jnp.max(s, axis=1, keepdims=True)
    p = jnp.exp2(s - m)
    return p, m, jnp.sum(p, axis=1, keepdims=True)


def _merge_heads(stack_col):
    low = lax.broadcasted_iota(jnp.int32, (BLOCK, LANES), 1) < HEAD_DIM
    return jnp.concatenate([jnp.where(low, stack_col(2 * pair, pair), stack_col(2 * pair + 1, pair))
                            for pair in range(GROUP_HEADS // 2)], axis=1)


def _attend(s, bias, v_win):
    p, m, denom = _softmax2(s, bias)
    pv = jnp.dot(p.astype(_BF16), v_win, preferred_element_type=_F32)
    inv = 1.0 / denom
    lse = m + jnp.log2(denom)
    o = _merge_heads(lambda hd, pair: pv[_head_rows(hd), pair * LANES:(pair + 1) * LANES] * inv[_head_rows(hd)])
    lse = _merge_heads(lambda hd, pair: jnp.broadcast_to(lse[_head_rows(hd)], (BLOCK, LANES)))
    return o, lse


def _pipelined(units, scores_fn, finish_fn):
    s_next = scores_fn(units[0])
    for i, unit in enumerate(units):
        s = s_next
        if i + 1 < len(units):
            s_next = scores_fn(units[i + 1])
        finish_fn(unit, s)


def _window(qb, prev_ref, cur_ref, seq=None):
    idx = (lambda rows: (rows, slice(None))) if seq is None else (lambda rows: (seq, rows, slice(None)))
    rows = slice(qb * BLOCK, (qb + 1) * BLOCK)
    if qb == 0:
        prev = prev_ref[idx(slice(0, BLOCK))]
    else:
        prev = cur_ref[idx(slice((qb - 1) * BLOCK, qb * BLOCK))]
    return jnp.concatenate([prev, cur_ref[idx(rows)]], axis=0)


def _dilated_kernel(q_ref, kprev_ref, k_ref, vprev_ref, v_ref, o_ref, lse_ref, *, max_dist):
    n_seqs, tile, _ = q_ref.shape
    j = pl.program_id(1)
    units = [(seq, qb) for seq in range(n_seqs) for qb in range(tile // BLOCK)]
    bias_first = _band_bias(max_dist, jnp.where(j == 0, BLOCK, 0))
    bias_inner = _band_bias(max_dist, 0)

    def scores(unit):
        seq, qb = unit
        return _stacked_scores(q_ref[seq, qb * BLOCK:(qb + 1) * BLOCK, :], _window(qb, kprev_ref, k_ref, seq))

    def finish(unit, s):
        seq, qb = unit
        o, lse = _attend(s, bias_first if qb == 0 else bias_inner, _window(qb, vprev_ref, v_ref, seq))
        o_ref[seq, qb * BLOCK:(qb + 1) * BLOCK, :] = o.astype(o_ref.dtype)
        lse_ref[seq, qb * BLOCK:(qb + 1) * BLOCK, :] = lse

    _pipelined(units, scores, finish)


def _dilated_attention(qkv, window, dilation):
    n_seqs, length, _ = qkv.shape
    tile = min(ATTN_UNITS * BLOCK, length)
    seqs = ATTN_UNITS * BLOCK // tile
    n_blocks = tile // BLOCK

    def cur(col):
        return pl.BlockSpec((seqs, tile, GROUP_WIDTH), lambda i, j: (i, j, col))

    def prev(col):
        return pl.BlockSpec((seqs, BLOCK, GROUP_WIDTH), lambda i, j: (i, jnp.maximum(j * n_blocks - 1, 0), col))

    out_spec = pl.BlockSpec((seqs, tile, GROUP_WIDTH), lambda i, j: (i, j, 0))
    return pl.pallas_call(
        functools.partial(_dilated_kernel, max_dist=window // dilation),
        grid=(n_seqs // seqs, length // tile),
        in_specs=[cur(0), prev(1), cur(1), prev(2), cur(2)],
        out_specs=[out_spec, out_spec],
        out_shape=[jax.ShapeDtypeStruct((n_seqs, length, GROUP_WIDTH), _BF16),
                   jax.ShapeDtypeStruct((n_seqs, length, GROUP_WIDTH), _F32)],
        compiler_params=pltpu.CompilerParams(dimension_semantics=("arbitrary",) * 2, vmem_limit_bytes=VMEM_LIMIT_BYTES),
        name=f"dilated_attn_d{dilation}",
    )(qkv, qkv, qkv, qkv, qkv)


def _swa_kernel(sink_ref, q_ref, kprev_ref, k_ref, vprev_ref, v_ref, o_ref, kdup_ref, vdup_ref):
    tile = q_ref.shape[0]
    j = pl.program_id(1)
    n_kv_heads = SWA_KV_WIDTH // HEAD_DIM
    units = [(qb, kv) for qb in range(tile // BLOCK) for kv in range(n_kv_heads)]

    def duplicate(dst_ref, row0, t):
        low = lax.broadcasted_iota(jnp.int32, t.shape, 1) < HEAD_DIM
        t32 = t.astype(_F32)
        swapped = pltpu.roll(t32, HEAD_DIM, 1)
        dst_ref[0, row0:row0 + t.shape[0], :] = jnp.where(low, t32, swapped).astype(_BF16)
        dst_ref[1, row0:row0 + t.shape[0], :] = jnp.where(low, swapped, t32).astype(_BF16)

    for dst_ref, prev_ref, cur_ref in ((kdup_ref, kprev_ref, k_ref), (vdup_ref, vprev_ref, v_ref)):
        duplicate(dst_ref, 0, prev_ref[...])
        duplicate(dst_ref, BLOCK, cur_ref[...])

    low = lax.broadcasted_iota(jnp.int32, (BLOCK, LANES), 1) < HEAD_DIM

    def scores(unit):
        qb, kv = unit
        rows = slice(qb * BLOCK, (qb + 1) * BLOCK)
        parts = []
        for pair in range(GROUP_HEADS // 2):
            lane0 = kv * GROUP_WIDTH + pair * LANES
            q_pair = q_ref[rows, lane0:lane0 + LANES]
            zero = jnp.zeros_like(q_pair)
            parts += [jnp.where(low, q_pair, zero), jnp.where(low, zero, q_pair)]
        q_stack = jnp.concatenate(parts, axis=0)
        k_win = kdup_ref[kv, qb * BLOCK:(qb + 2) * BLOCK, :]
        return lax.dot_general(q_stack, k_win, _NT, preferred_element_type=_F32)

    sink_col = lax.broadcasted_iota(jnp.int32, (_STACK, 2 * BLOCK), 1) == 0
    bias_first = jnp.where(sink_col, 0.0, _band_bias(SWA_WINDOW - 1, jnp.where(j == 0, BLOCK, 0)))
    bias_inner = jnp.where(sink_col, 0.0, _band_bias(SWA_WINDOW - 1, 0))
    sink_lane = lax.broadcasted_iota(jnp.int32, (BLOCK, LANES), 1) == 0
    sink_lane_stack = lax.broadcasted_iota(jnp.int32, (_STACK, LANES), 1) == 0

    def finish(unit, s):
        qb, kv = unit
        rows = slice(qb * BLOCK, (qb + 1) * BLOCK)
        s_left = jnp.concatenate([jnp.where(sink_lane, sink_ref[kv * GROUP_HEADS + hd], s[_head_rows(hd), :LANES])
                                  for hd in range(GROUP_HEADS)], axis=0)
        s = jnp.concatenate([s_left, s[:, LANES:]], axis=1)
        p, _, denom = _softmax2(s, bias_first if qb == 0 else bias_inner)
        p = jnp.concatenate([jnp.where(sink_lane_stack, 0.0, p[:, :LANES]), p[:, LANES:]], axis=1).astype(_BF16)
        pv = jnp.dot(p, vdup_ref[kv, qb * BLOCK:(qb + 2) * BLOCK, :], preferred_element_type=_F32)
        inv = 1.0 / denom
        for pair in range(GROUP_HEADS // 2):
            lane0 = kv * GROUP_WIDTH + pair * LANES
            o_pair = jnp.where(low, pv[_head_rows(2 * pair)] * inv[_head_rows(2 * pair)],
                               pv[_head_rows(2 * pair + 1)] * inv[_head_rows(2 * pair + 1)])
            o_ref[rows, lane0:lane0 + LANES] = o_pair.astype(o_ref.dtype)

    _pipelined(units, scores, finish)


def _swa_attention(qkvb3, sinks):
    b, s, _ = qkvb3.shape
    tile = ATTN_UNITS * BLOCK // 2
    n_blocks = tile // BLOCK
    k_col = SWA_Q_WIDTH // LANES
    v_col = k_col + SWA_KV_WIDTH // LANES

    def cur(col):
        return pl.BlockSpec((None, tile, LANES), lambda bi, j: (bi, j, col))

    def prev(col):
        return pl.BlockSpec((None, BLOCK, LANES), lambda bi, j: (bi, jnp.maximum(j * n_blocks - 1, 0), col))

    q_spec = pl.BlockSpec((None, tile, SWA_Q_WIDTH), lambda bi, j: (bi, j, 0))
    o = pl.pallas_call(
        _swa_kernel,
        grid=(b, s // tile),
        in_specs=[pl.BlockSpec(memory_space=pltpu.SMEM), q_spec, prev(k_col), cur(k_col), prev(v_col), cur(v_col)],
        out_specs=q_spec,
        out_shape=jax.ShapeDtypeStruct((b, s, SWA_Q_WIDTH), _BF16),
        scratch_shapes=[pltpu.VMEM((SWA_KV_WIDTH // HEAD_DIM, BLOCK + tile, LANES), _BF16)] * 2,
        compiler_params=pltpu.CompilerParams(dimension_semantics=("arbitrary",) * 2, vmem_limit_bytes=VMEM_LIMIT_BYTES),
        name="swa_attn",
    )(sinks, qkvb3, qkvb3, qkvb3, qkvb3, qkvb3)
    return o.reshape(b * s, SWA_Q_WIDTH)


def _out_mlp_kernel(x_ref, o0_ref, l0_ref, o1_ref, l1_ref, o2_ref, l2_ref, ob_ref, gate_ref,
                    wa_ref, wb_ref, wo_ref, ln_ref, wup_ref, wdn_ref, out_ref, slab_ref):
    tm = x_ref.shape[0]
    n_slabs = GROUP_WIDTH // LANES

    def interleaved(src_ref, base):
        d = src_ref.shape[0]
        for r in range(d):
            for slab in range(n_slabs):
                slab_ref[base + slab, pl.ds(r, tm // d, stride=d), :] = (
                    src_ref[r, :, slab * LANES:(slab + 1) * LANES].astype(_F32))
        return jnp.concatenate([slab_ref[base + slab] for slab in range(n_slabs)], axis=1)

    o0, l0 = o0_ref[...].astype(_F32), l0_ref[...]
    o1, l1 = interleaved(o1_ref, 0), interleaved(l1_ref, n_slabs)
    o2, l2 = interleaved(o2_ref, 2 * n_slabs), interleaved(l2_ref, 3 * n_slabs)
    m = jnp.maximum(jnp.maximum(l0, l1), l2)
    e0, e1, e2 = jnp.exp2(l0 - m), jnp.exp2(l1 - m), jnp.exp2(l2 - m)
    oa = ((e0 * o0 + e1 * o1 + e2 * o2) / (e0 + e1 + e2)).astype(_BF16)

    ya = jnp.dot(oa, wa_ref[...], preferred_element_type=_F32)
    yb = jnp.dot(ob_ref[...], wb_ref[...], preferred_element_type=_F32)
    ga = gate_ref[:, :D_MODEL].astype(_F32)
    gb = gate_ref[:, D_MODEL:].astype(_F32)
    mix = (jax.nn.sigmoid(ga) * ya + jax.nn.sigmoid(gb) * yb).astype(_BF16)
    x1 = x_ref[...] + jnp.dot(mix, wo_ref[...], preferred_element_type=_F32)

    ms = jnp.mean(x1 * x1, axis=-1, keepdims=True)
    h2 = (x1 * lax.rsqrt(ms + EPS) * ln_ref[...]).astype(_BF16)
    ff_chunk = 512
    acc = x1
    for c in range(D_FF // ff_chunk):
        u = jnp.dot(h2, wup_ref[:, c * ff_chunk:(c + 1) * ff_chunk], preferred_element_type=_F32)
        a = jnp.square(jnp.maximum(u, 0.0)).astype(_BF16)
        acc = acc + jnp.dot(a, wdn_ref[c * ff_chunk:(c + 1) * ff_chunk, :], preferred_element_type=_F32)
    out_ref[...] = acc


def _out_mlp(x2, attn_a, ob, gates, wa, wb, wo, ln2, wup, wdn, seq):
    n = x2.shape[0]
    tm = TOKEN_TILE
    tiles_per_seq = seq // tm
    row = lambda width: pl.BlockSpec((tm, width), lambda i: (i, 0))

    def deinterleaved(d):
        return pl.BlockSpec((None, d, tm // d, GROUP_WIDTH), lambda i: (i // tiles_per_seq, 0, i % tiles_per_seq, 0))

    group_specs = [row(GROUP_WIDTH)] * 2
    for _, d in DIL_PAIRS[1:]:
        group_specs += [deinterleaved(d)] * 2
    return pl.pallas_call(
        _out_mlp_kernel,
        grid=(n // tm,),
        in_specs=[row(D_MODEL)] + group_specs + [row(SWA_Q_WIDTH), row(GATE_WIDTH),
                  _resident(wa.shape), _resident(wb.shape), _resident(wo.shape), _resident((1, D_MODEL)),
                  _resident(wup.shape), _resident(wdn.shape)],
        out_specs=row(D_MODEL),
        out_shape=jax.ShapeDtypeStruct((n, D_MODEL), _F32),
        scratch_shapes=[pltpu.VMEM((4 * GROUP_WIDTH // LANES, tm, LANES), _F32)],
        compiler_params=pltpu.CompilerParams(dimension_semantics=("arbitrary",), vmem_limit_bytes=VMEM_LIMIT_BYTES),
        name="out_mlp",
    )(x2, *[a for pair in attn_a for a in pair], ob, gates, wa, wb, wo, ln2, wup, wdn)


def _rope_lane_tables(positions):
    inv_freq = ROPE_THETA ** (-jnp.arange(0, ROPE_DIM, 2, dtype=_F32) / ROPE_DIM)
    lane = np.arange(LANES) % HEAD_DIM
    rotary = jnp.asarray(lane < ROPE_DIM)
    freq = jnp.where(rotary, inv_freq[lane % ROPE_HALF], 0.0)
    ang = positions.reshape(-1, 1).astype(_F32) * freq[None, :]
    return jnp.where(rotary, jnp.cos(ang), 1.0), jnp.where(rotary, jnp.sin(ang), 0.0)


def kernel(x, positions, ln1_g, w_in, q_norm_a, k_norm_a, q_norm_b, k_norm_b, sinks,
           w_branch_a, w_branch_b, w_out, ln2_g, w_up, w_down):
    b, s, d_model = x.shape
    assert d_model == D_MODEL and ln1_g.shape[0] == 1 and s % (16 * BLOCK) == 0 and s % TOKEN_TILE == 0
    x2 = x.reshape(b * s, d_model)
    cos_t, sin_t = _rope_lane_tables(positions)
    scale = np.float32(np.log2(np.e) / np.sqrt(HEAD_DIM))
    gains = jnp.stack([jnp.tile(q_norm_a[0] * scale, 2), jnp.tile(k_norm_a[0], 2),
                       jnp.tile(q_norm_b[0] * scale, 2), jnp.tile(k_norm_b[0], 2)])[:, None, :]
    head = np.arange(LANES) // HEAD_DIM
    seg = jnp.asarray(head[:, None] == head[None, :], _BF16)

    qkv0, qkv1, qkv2, qkvb, gates = _in_proj(x2, ln1_g, w_in[0].astype(_BF16), gains, cos_t, sin_t, seg, b, s)
    attn_a = []
    for (window, d), qkv in zip(DIL_PAIRS, (qkv0.reshape(b, s, GROUP_QKV_WIDTH), qkv1, qkv2)):
        o, lse = _dilated_attention(qkv.reshape(b * d, s // d, GROUP_QKV_WIDTH), window, d)
        shape = (b * s, GROUP_WIDTH) if d == 1 else (b, d, s // d, GROUP_WIDTH)
        attn_a.append((o.reshape(shape), lse.reshape(shape)))
    ob = _swa_attention(qkvb.reshape(b, s, QKV_B_WIDTH), sinks[0] * np.float32(np.log2(np.e)))
    out = _out_mlp(x2, attn_a, ob, gates,
                   w_branch_a[0].astype(_BF16), w_branch_b[0].astype(_BF16), w_out[0].astype(_BF16),
                   ln2_g, w_up[0].astype(_BF16), w_down[0].astype(_BF16), s)
    return out.reshape(b, s, d_model)
```

```python
import functools

import numpy as np
import jax
import jax.numpy as jnp
from jax import lax
from jax.experimental import pallas as pl
from jax.experimental.pallas import tpu as pltpu

D_MODEL = 1024
HEAD_DIM = 64
DIL_PAIRS = ((128, 1), (512, 4), (2048, 16))
GROUP_HEADS = 4
GROUP_WIDTH = GROUP_HEADS * HEAD_DIM
DIL_WIDTH = 768
N_DIL_GROUPS = DIL_WIDTH // GROUP_WIDTH
GROUP_QKV_WIDTH = 3 * GROUP_WIDTH
SWA_WINDOW = 128
SWA_Q_WIDTH = 512
SWA_KV_WIDTH = 128
QKV_A_WIDTH = 3 * DIL_WIDTH
QKV_B_WIDTH = SWA_Q_WIDTH + 2 * SWA_KV_WIDTH
GATE_WIDTH = 2 * D_MODEL
D_FF = 4 * D_MODEL
ROPE_THETA = 500000.0
ROPE_DIM = HEAD_DIM // 4
ROPE_HALF = ROPE_DIM // 2
BLOCK = 128
EPS = 1e-6
NEG = -1e30

LANES = 128
CHUNK = 256
TOKEN_TILE = 512
PROJ_LOOKAHEAD = 2
ATTN_UNITS = 8
VMEM_LIMIT_BYTES = 56 * 1024 * 1024

_BF16 = jnp.bfloat16
_F32 = jnp.float32


def _resident(shape):
    return pl.BlockSpec(shape, lambda *_: (0,) * len(shape), pipeline_mode=pl.Buffered(1))


def _in_proj_kernel(x_ref, rope_ref, ln_ref, w_ref, gain_ref, seg_ref,
                    qkv0_ref, qkv1_ref, qkv2_ref, qkvb_ref, gate_ref, slab1_ref, slab2_ref):
    tm = x_ref.shape[0]
    x = x_ref[...]
    ms = jnp.mean(x * x, axis=-1, keepdims=True)
    h = (x * lax.rsqrt(ms + EPS) * ln_ref[...]).astype(_BF16)

    lane = lax.broadcasted_iota(jnp.int32, (tm, LANES), 1) % HEAD_DIM
    rotary = lane < ROPE_DIM
    packed = rope_ref[...]
    cos_t = jnp.take_along_axis(packed, jnp.where(rotary, lane % ROPE_HALF, 2 * ROPE_HALF), axis=1,
                                mode="promise_in_bounds")
    sin_t = jnp.take_along_axis(packed, jnp.where(rotary, ROPE_HALF + lane % ROPE_HALF, 2 * ROPE_HALF + 1), axis=1,
                                mode="promise_in_bounds")
    first_half = lane < ROPE_HALF
    sin_from_hi = jnp.where(first_half, -sin_t, 0.0)
    sin_from_lo = jnp.where(first_half, 0.0, sin_t)
    seg = seg_ref[...]

    def norm_rope(y, gain):
        ss = jnp.dot((y * y).astype(_BF16), seg, preferred_element_type=_F32)
        yn = y * lax.rsqrt(ss * (1.0 / HEAD_DIM) + EPS) * gain
        up = pltpu.roll(yn, LANES - ROPE_HALF, 1)
        dn = pltpu.roll(yn, ROPE_HALF, 1)
        return yn * cos_t + up * sin_from_hi + dn * sin_from_lo

    def project(w_col):
        return jnp.dot(h, w_ref[:, w_col:w_col + CHUNK], preferred_element_type=_F32)

    tasks = []
    group_out = (qkv0_ref, slab1_ref, slab2_ref)

    def dilated_store(g, slab):
        def store(yh):
            if g == 0:
                qkv0_ref[:, slab * LANES:(slab + 1) * LANES] = yh.astype(_BF16)
            else:
                group_out[g][slab] = yh
        return store

    def row_store(out_ref, col):
        def store(yh):
            out_ref[:, col:col + LANES] = yh.astype(_BF16)
        return store

    n_halves = CHUNK // LANES
    for part in range(3):
        for g in range(N_DIL_GROUPS):
            tasks.append((part * DIL_WIDTH + g * GROUP_WIDTH,
                          [(part if part < 2 else None, dilated_store(g, part * n_halves + i))
                           for i in range(n_halves)]))
    for c in range(QKV_B_WIDTH // CHUNK):
        cols = [c * CHUNK + i * LANES for i in range(n_halves)]
        tasks.append((QKV_A_WIDTH + c * CHUNK,
                      [(2 if col < SWA_Q_WIDTH else (3 if col < SWA_Q_WIDTH + SWA_KV_WIDTH else None),
                        row_store(qkvb_ref, col)) for col in cols]))
    def gate_store(col):
        def store(yh):
            gate_ref[:, col:col + LANES] = (0.5 * jnp.tanh(0.5 * yh) + 0.5).astype(_BF16)
        return store

    for c in range(GATE_WIDTH // CHUNK):
        tasks.append((QKV_A_WIDTH + QKV_B_WIDTH + c * CHUNK,
                      [(None, gate_store(c * CHUNK + i * LANES)) for i in range(n_halves)]))

    def epilogue(y, stores):
        for i, (gain_idx, store) in enumerate(stores):
            yh = y[:, i * LANES:(i + 1) * LANES]
            store(yh if gain_idx is None else norm_rope(yh, gain_ref[gain_idx]))

    in_flight = []
    for w_col, stores in tasks:
        in_flight.append((project(w_col), stores))
        if len(in_flight) > PROJ_LOOKAHEAD:
            epilogue(*in_flight.pop(0))
    for y, stores in in_flight:
        epilogue(y, stores)

    for g, out_ref, slab_ref in ((1, qkv1_ref, slab1_ref), (2, qkv2_ref, slab2_ref)):
        d = DIL_PAIRS[g][1]
        for r in range(d):
            for slab in range(GROUP_QKV_WIDTH // LANES):
                rows = slab_ref[slab, pl.ds(r, tm // d, stride=d), :]
                out_ref[r, :, slab * LANES:(slab + 1) * LANES] = rows.astype(_BF16)


def _in_proj(x2, rope, ln1, w_in, gains, seg, batch, seq):
    n = x2.shape[0]
    tm = TOKEN_TILE
    tiles_per_seq = seq // tm
    row = lambda width: pl.BlockSpec((tm, width), lambda i: (i, 0))

    def deinterleaved(d):
        return pl.BlockSpec((None, d, tm // d, GROUP_QKV_WIDTH),
                            lambda i: (i // tiles_per_seq, 0, i % tiles_per_seq, 0))

    d1, d2 = DIL_PAIRS[1][1], DIL_PAIRS[2][1]
    n_slabs = GROUP_QKV_WIDTH // LANES
    return pl.pallas_call(
        _in_proj_kernel,
        grid=(n // tm,),
        in_specs=[row(D_MODEL), row(LANES), _resident((1, D_MODEL)), _resident(w_in.shape), _resident(gains.shape),
                  _resident(seg.shape)],
        out_specs=[row(GROUP_QKV_WIDTH), deinterleaved(d1), deinterleaved(d2), row(QKV_B_WIDTH), row(GATE_WIDTH)],
        out_shape=[jax.ShapeDtypeStruct((n, GROUP_QKV_WIDTH), _BF16),
                   jax.ShapeDtypeStruct((batch, d1, seq // d1, GROUP_QKV_WIDTH), _BF16),
                   jax.ShapeDtypeStruct((batch, d2, seq // d2, GROUP_QKV_WIDTH), _BF16),
                   jax.ShapeDtypeStruct((n, QKV_B_WIDTH), _BF16),
                   jax.ShapeDtypeStruct((n, GATE_WIDTH), _BF16)],
        scratch_shapes=[pltpu.VMEM((n_slabs, tm, LANES), _F32), pltpu.VMEM((n_slabs, tm, LANES), _F32)],
        compiler_params=pltpu.CompilerParams(dimension_semantics=("arbitrary",), vmem_limit_bytes=VMEM_LIMIT_BYTES),
        name="in_proj",
    )(x2, rope, ln1, w_in, gains, seg)


_NT = (((1,), (1,)), ((), ()))
_STACK = GROUP_HEADS * BLOCK


def _stacked_scores(q, k_win):
    head_of_lane = lax.broadcasted_iota(jnp.int32, q.shape, 1) // HEAD_DIM
    zero = jnp.zeros_like(q)
    q_stack = jnp.concatenate([jnp.where(head_of_lane == hd, q, zero) for hd in range(GROUP_HEADS)], axis=0)
    return lax.dot_general(q_stack, k_win, _NT, preferred_element_type=_F32)


def _band_bias(max_dist, first_key_col):
    row = lax.broadcasted_iota(jnp.int32, (_STACK, 2 * BLOCK), 0) % BLOCK
    col = lax.broadcasted_iota(jnp.int32, (_STACK, 2 * BLOCK), 1)
    dist = row + BLOCK - col
    return jnp.where((dist >= 0) & (dist <= max_dist) & (col >= first_key_col), 0.0, NEG).astype(_F32)


def _head_rows(hd):
    return slice(hd * BLOCK, (hd + 1) * BLOCK)


def _softmax2(s, bias):
    s = s + bias
    m = jnp.max(s, axis=1, keepdims=True)
    p = jnp.exp2(s - m)
    return p, m, jnp.sum(p, axis=1, keepdims=True)


def _merge_heads(stack_col):
    low = lax.broadcasted_iota(jnp.int32, (BLOCK, LANES), 1) < HEAD_DIM
    return jnp.concatenate([jnp.where(low, stack_col(2 * pair, pair), stack_col(2 * pair + 1, pair))
                            for pair in range(GROUP_HEADS // 2)], axis=1)


def _attend(s, bias, v_win):
    p, m, denom = _softmax2(s, bias)
    pv = jnp.dot(p.astype(_BF16), v_win, preferred_element_type=_F32)
    inv = 1.0 / denom
    lse = m + jnp.log2(denom)
    o = _merge_heads(lambda hd, pair: pv[_head_rows(hd), pair * LANES:(pair + 1) * LANES] * inv[_head_rows(hd)])
    lse = _merge_heads(lambda hd, pair: jnp.broadcast_to(lse[_head_rows(hd)], (BLOCK, LANES)))
    return o, lse


def _pipelined(units, scores_fn, finish_fn):
    s_next = scores_fn(units[0])
    for i, unit in enumerate(units):
        s = s_next
        if i + 1 < len(units):
            s_next = scores_fn(units[i + 1])
        finish_fn(unit, s)


def _window(qb, prev_ref, cur_ref, seq=None):
    idx = (lambda rows: (rows, slice(None))) if seq is None else (lambda rows: (seq, rows, slice(None)))
    rows = slice(qb * BLOCK, (qb + 1) * BLOCK)
    if qb == 0:
        prev = prev_ref[idx(slice(0, BLOCK))]
    else:
        prev = cur_ref[idx(slice((qb - 1) * BLOCK, qb * BLOCK))]
    return jnp.concatenate([prev, cur_ref[idx(rows)]], axis=0)


def _dilated_kernel(q_ref, kprev_ref, k_ref, vprev_ref, v_ref, o_ref, lse_ref, *, max_dist):
    n_seqs, tile, _ = q_ref.shape
    j = pl.program_id(1)
    units = [(seq, qb) for seq in range(n_seqs) for qb in range(tile // BLOCK)]
    bias_first = _band_bias(max_dist, jnp.where(j == 0, BLOCK, 0))
    bias_inner = _band_bias(max_dist, 0)

    def scores(unit):
        seq, qb = unit
        return _stacked_scores(q_ref[seq, qb * BLOCK:(qb + 1) * BLOCK, :], _window(qb, kprev_ref, k_ref, seq))

    def finish(unit, s):
        seq, qb = unit
        o, lse = _attend(s, bias_first if qb == 0 else bias_inner, _window(qb, vprev_ref, v_ref, seq))
        o_ref[seq, qb * BLOCK:(qb + 1) * BLOCK, :] = o.astype(o_ref.dtype)
        lse_ref[seq, qb * BLOCK:(qb + 1) * BLOCK, :] = lse

    _pipelined(units, scores, finish)


def _dilated_attention(qkv, window, dilation):
    n_seqs, length, _ = qkv.shape
    tile = min(ATTN_UNITS * BLOCK, length)
    seqs = ATTN_UNITS * BLOCK // tile
    n_blocks = tile // BLOCK

    def cur(col):
        return pl.BlockSpec((seqs, tile, GROUP_WIDTH), lambda i, j: (i, j, col))

    def prev(col):
        return pl.BlockSpec((seqs, BLOCK, GROUP_WIDTH), lambda i, j: (i, jnp.maximum(j * n_blocks - 1, 0), col))

    out_spec = pl.BlockSpec((seqs, tile, GROUP_WIDTH), lambda i, j: (i, j, 0))
    return pl.pallas_call(
        functools.partial(_dilated_kernel, max_dist=window // dilation),
        grid=(n_seqs // seqs, length // tile),
        in_specs=[cur(0), prev(1), cur(1), prev(2), cur(2)],
        out_specs=[out_spec, out_spec],
        out_shape=[jax.ShapeDtypeStruct((n_seqs, length, GROUP_WIDTH), _BF16),
                   jax.ShapeDtypeStruct((n_seqs, length, GROUP_WIDTH), _F32)],
        compiler_params=pltpu.CompilerParams(dimension_semantics=("arbitrary",) * 2, vmem_limit_bytes=VMEM_LIMIT_BYTES),
        name=f"dilated_attn_d{dilation}",
    )(qkv, qkv, qkv, qkv, qkv)


def _swa_kernel(sink_ref, q_ref, kprev_ref, k_ref, vprev_ref, v_ref, o_ref, kdup_ref, vdup_ref):
    tile = q_ref.shape[0]
    j = pl.program_id(1)
    n_kv_heads = SWA_KV_WIDTH // HEAD_DIM
    units = [(qb, kv) for qb in range(tile // BLOCK) for kv in range(n_kv_heads)]

    def duplicate(dst_ref, row0, t):
        low = lax.broadcasted_iota(jnp.int32, t.shape, 1) < HEAD_DIM
        t32 = t.astype(_F32)
        swapped = pltpu.roll(t32, HEAD_DIM, 1)
        dst_ref[0, row0:row0 + t.shape[0], :] = jnp.where(low, t32, swapped).astype(_BF16)
        dst_ref[1, row0:row0 + t.shape[0], :] = jnp.where(low, swapped, t32).astype(_BF16)

    for dst_ref, prev_ref, cur_ref in ((kdup_ref, kprev_ref, k_ref), (vdup_ref, vprev_ref, v_ref)):
        duplicate(dst_ref, 0, prev_ref[...])
        duplicate(dst_ref, BLOCK, cur_ref[...])

    low = lax.broadcasted_iota(jnp.int32, (BLOCK, LANES), 1) < HEAD_DIM

    def scores(unit):
        qb, kv = unit
        rows = slice(qb * BLOCK, (qb + 1) * BLOCK)
        parts = []
        for pair in range(GROUP_HEADS // 2):
            lane0 = kv * GROUP_WIDTH + pair * LANES
            q_pair = q_ref[rows, lane0:lane0 + LANES]
            zero = jnp.zeros_like(q_pair)
            parts += [jnp.where(low, q_pair, zero), jnp.where(low, zero, q_pair)]
        q_stack = jnp.concatenate(parts, axis=0)
        k_win = kdup_ref[kv, qb * BLOCK:(qb + 2) * BLOCK, :]
        return lax.dot_general(q_stack, k_win, _NT, preferred_element_type=_F32)

    sink_col = lax.broadcasted_iota(jnp.int32, (_STACK, 2 * BLOCK), 1) == 0
    bias_first = jnp.where(sink_col, 0.0, _band_bias(SWA_WINDOW - 1, jnp.where(j == 0, BLOCK, 0)))
    bias_inner = jnp.where(sink_col, 0.0, _band_bias(SWA_WINDOW - 1, 0))
    sink_lane = lax.broadcasted_iota(jnp.int32, (BLOCK, LANES), 1) == 0
    sink_lane_stack = lax.broadcasted_iota(jnp.int32, (_STACK, LANES), 1) == 0

    def finish(unit, s):
        qb, kv = unit
        rows = slice(qb * BLOCK, (qb + 1) * BLOCK)
        s_left = jnp.concatenate([jnp.where(sink_lane, sink_ref[kv * GROUP_HEADS + hd], s[_head_rows(hd), :LANES])
                                  for hd in range(GROUP_HEADS)], axis=0)
        s = jnp.concatenate([s_left, s[:, LANES:]], axis=1)
        p, _, denom = _softmax2(s, bias_first if qb == 0 else bias_inner)
        p = jnp.concatenate([jnp.where(sink_lane_stack, 0.0, p[:, :LANES]), p[:, LANES:]], axis=1).astype(_BF16)
        pv = jnp.dot(p, vdup_ref[kv, qb * BLOCK:(qb + 2) * BLOCK, :], preferred_element_type=_F32)
        inv = 1.0 / denom
        for pair in range(GROUP_HEADS // 2):
            lane0 = kv * GROUP_WIDTH + pair * LANES
            o_pair = jnp.where(low, pv[_head_rows(2 * pair)] * inv[_head_rows(2 * pair)],
                               pv[_head_rows(2 * pair + 1)] * inv[_head_rows(2 * pair + 1)])
            o_ref[rows, lane0:lane0 + LANES] = o_pair.astype(o_ref.dtype)

    _pipelined(units, scores, finish)


def _swa_attention(qkvb3, sinks):
    b, s, _ = qkvb3.shape
    tile = ATTN_UNITS * BLOCK // 2
    n_blocks = tile // BLOCK
    k_col = SWA_Q_WIDTH // LANES
    v_col = k_col + SWA_KV_WIDTH // LANES

    def cur(col):
        return pl.BlockSpec((None, tile, LANES), lambda bi, j: (bi, j, col))

    def prev(col):
        return pl.BlockSpec((None, BLOCK, LANES), lambda bi, j: (bi, jnp.maximum(j * n_blocks - 1, 0), col))

    q_spec = pl.BlockSpec((None, tile, SWA_Q_WIDTH), lambda bi, j: (bi, j, 0))
    o = pl.pallas_call(
        _swa_kernel,
        grid=(b, s // tile),
        in_specs=[pl.BlockSpec(memory_space=pltpu.SMEM), q_spec, prev(k_col), cur(k_col), prev(v_col), cur(v_col)],
        out_specs=q_spec,
        out_shape=jax.ShapeDtypeStruct((b, s, SWA_Q_WIDTH), _BF16),
        scratch_shapes=[pltpu.VMEM((SWA_KV_WIDTH // HEAD_DIM, BLOCK + tile, LANES), _BF16)] * 2,
        compiler_params=pltpu.CompilerParams(dimension_semantics=("arbitrary",) * 2, vmem_limit_bytes=VMEM_LIMIT_BYTES),
        name="swa_attn",
    )(sinks, qkvb3, qkvb3, qkvb3, qkvb3, qkvb3)
    return o.reshape(b * s, SWA_Q_WIDTH)


def _out_mlp_kernel(x_ref, o0_ref, l0_ref, o1_ref, l1_ref, o2_ref, l2_ref, ob_ref, gate_ref,
                    wa_ref, wb_ref, wo_ref, ln_ref, wup_ref, wdn_ref, out_ref, slab_ref):
    tm = x_ref.shape[0]
    n_slabs = GROUP_WIDTH // LANES

    def interleaved(src_ref, base):
        d = src_ref.shape[0]
        for r in range(d):
            for slab in range(n_slabs):
                slab_ref[base + slab, pl.ds(r, tm // d, stride=d), :] = (
                    src_ref[r, :, slab * LANES:(slab + 1) * LANES].astype(_F32))
        return jnp.concatenate([slab_ref[base + slab] for slab in range(n_slabs)], axis=1)

    o0, l0 = o0_ref[...].astype(_F32), l0_ref[...]
    o1, l1 = interleaved(o1_ref, 0), interleaved(l1_ref, n_slabs)
    o2, l2 = interleaved(o2_ref, 2 * n_slabs), interleaved(l2_ref, 3 * n_slabs)
    m = jnp.maximum(jnp.maximum(l0, l1), l2)
    e0, e1, e2 = jnp.exp2(l0 - m), jnp.exp2(l1 - m), jnp.exp2(l2 - m)
    oa = ((e0 * o0 + e1 * o1 + e2 * o2) / (e0 + e1 + e2)).astype(_BF16)

    ya = jnp.dot(oa, wa_ref[...], preferred_element_type=_F32)
    yb = jnp.dot(ob_ref[...], wb_ref[...], preferred_element_type=_F32)
    gate_a = gate_ref[:, :D_MODEL].astype(_F32)
    gate_b = gate_ref[:, D_MODEL:].astype(_F32)
    mix = (gate_a * ya + gate_b * yb).astype(_BF16)
    x1 = x_ref[...] + jnp.dot(mix, wo_ref[...], preferred_element_type=_F32)

    ms = jnp.mean(x1 * x1, axis=-1, keepdims=True)
    h2 = (x1 * lax.rsqrt(ms + EPS) * ln_ref[...]).astype(_BF16)
    ff_chunk = 512
    acc = x1
    for c in range(D_FF // ff_chunk):
        u = jnp.dot(h2, wup_ref[:, c * ff_chunk:(c + 1) * ff_chunk], preferred_element_type=_F32)
        a = jnp.square(jnp.maximum(u, 0.0)).astype(_BF16)
        acc = acc + jnp.dot(a, wdn_ref[c * ff_chunk:(c + 1) * ff_chunk, :], preferred_element_type=_F32)
    out_ref[...] = acc


def _out_mlp(x2, attn_a, ob, gates, wa, wb, wo, ln2, wup, wdn, seq):
    n = x2.shape[0]
    tm = TOKEN_TILE
    tiles_per_seq = seq // tm
    row = lambda width: pl.BlockSpec((tm, width), lambda i: (i, 0))

    def deinterleaved(d):
        return pl.BlockSpec((None, d, tm // d, GROUP_WIDTH), lambda i: (i // tiles_per_seq, 0, i % tiles_per_seq, 0))

    group_specs = [row(GROUP_WIDTH)] * 2
    for _, d in DIL_PAIRS[1:]:
        group_specs += [deinterleaved(d)] * 2
    return pl.pallas_call(
        _out_mlp_kernel,
        grid=(n // tm,),
        in_specs=[row(D_MODEL)] + group_specs + [row(SWA_Q_WIDTH), row(GATE_WIDTH),
                  _resident(wa.shape), _resident(wb.shape), _resident(wo.shape), _resident((1, D_MODEL)),
                  _resident(wup.shape), _resident(wdn.shape)],
        out_specs=row(D_MODEL),
        out_shape=jax.ShapeDtypeStruct((n, D_MODEL), _F32),
        scratch_shapes=[pltpu.VMEM((4 * GROUP_WIDTH // LANES, tm, LANES), _F32)],
        compiler_params=pltpu.CompilerParams(dimension_semantics=("arbitrary",), vmem_limit_bytes=VMEM_LIMIT_BYTES),
        name="out_mlp",
    )(x2, *[a for pair in attn_a for a in pair], ob, gates, wa, wb, wo, ln2, wup, wdn)


def _rope_packed_table(positions):
    inv_freq = ROPE_THETA ** (-jnp.arange(0, ROPE_DIM, 2, dtype=_F32) / ROPE_DIM)
    ang = positions.reshape(-1, 1).astype(_F32) * inv_freq[None, :]
    n = ang.shape[0]
    return jnp.concatenate([jnp.cos(ang), jnp.sin(ang), jnp.ones((n, 1), _F32),
                            jnp.zeros((n, LANES - 2 * ROPE_HALF - 1), _F32)], axis=1)


def kernel(x, positions, ln1_g, w_in, q_norm_a, k_norm_a, q_norm_b, k_norm_b, sinks,
           w_branch_a, w_branch_b, w_out, ln2_g, w_up, w_down):
    b, s, d_model = x.shape
    assert d_model == D_MODEL and ln1_g.shape[0] == 1 and s % (16 * BLOCK) == 0 and s % TOKEN_TILE == 0
    x2 = x.reshape(b * s, d_model)
    scale = np.float32(np.log2(np.e) / np.sqrt(HEAD_DIM))
    gains = jnp.stack([jnp.tile(q_norm_a[0] * scale, 2), jnp.tile(k_norm_a[0], 2),
                       jnp.tile(q_norm_b[0] * scale, 2), jnp.tile(k_norm_b[0], 2)])[:, None, :]
    head = np.arange(LANES) // HEAD_DIM
    seg = jnp.asarray(head[:, None] == head[None, :], _BF16)

    qkv0, qkv1, qkv2, qkvb, gates = _in_proj(x2, _rope_packed_table(positions), ln1_g, w_in[0].astype(_BF16), gains,
                                             seg, b, s)
    attn_a = []
    for (window, d), qkv in zip(DIL_PAIRS, (qkv0.reshape(b, s, GROUP_QKV_WIDTH), qkv1, qkv2)):
        o, lse = _dilated_attention(qkv.reshape(b * d, s // d, GROUP_QKV_WIDTH), window, d)
        shape = (b * s, GROUP_WIDTH) if d == 1 else (b, d, s // d, GROUP_WIDTH)
        attn_a.append((o.reshape(shape), lse.reshape(shape)))
    ob = _swa_attention(qkvb.reshape(b, s, QKV_B_WIDTH), sinks[0] * np.float32(np.log2(np.e)))
    out = _out_mlp(x2, attn_a, ob, gates,
                   w_branch_a[0].astype(_BF16), w_branch_b[0].astype(_BF16), w_out[0].astype(_BF16),
                   ln2_g, w_up[0].astype(_BF16), w_down[0].astype(_BF16), s)
    return out.reshape(b, s, d_model)
```

```python
import functools

import numpy as np
import jax
import jax.numpy as jnp
from jax import lax
from jax.experimental import pallas as pl
from jax.experimental.pallas import tpu as pltpu

D_MODEL = 1024
HEAD_DIM = 64
DIL_PAIRS = ((128, 1), (512, 4), (2048, 16))
GROUP_HEADS = 4
GROUP_WIDTH = GROUP_HEADS * HEAD_DIM
DIL_WIDTH = 768
N_DIL_GROUPS = DIL_WIDTH // GROUP_WIDTH
GROUP_QKV_WIDTH = 3 * GROUP_WIDTH
SWA_WINDOW = 128
SWA_Q_WIDTH = 512
SWA_KV_WIDTH = 128
QKV_A_WIDTH = 3 * DIL_WIDTH
QKV_B_WIDTH = SWA_Q_WIDTH + 2 * SWA_KV_WIDTH
GATE_WIDTH = 2 * D_MODEL
D_FF = 4 * D_MODEL
ROPE_THETA = 500000.0
ROPE_DIM = HEAD_DIM // 4
ROPE_HALF = ROPE_DIM // 2
BLOCK = 128
EPS = 1e-6
NEG = -1e30

LANES = 128
CHUNK = 256
TOKEN_TILE = 512
PROJ_LOOKAHEAD = 2
ATTN_UNITS = 8
VMEM_LIMIT_BYTES = 56 * 1024 * 1024

_BF16 = jnp.bfloat16
_F32 = jnp.float32


def _resident(shape):
    return pl.BlockSpec(shape, lambda *_: (0,) * len(shape), pipeline_mode=pl.Buffered(1))


def _in_proj_kernel(x_ref, pos_ref, ln_ref, w_ref, gain_ref, freq_ref, seg_ref,
                    qkv0_ref, qkv1_ref, qkv2_ref, qkvb_ref, gate_ref, slab1_ref, slab2_ref):
    tm = x_ref.shape[0]
    x = x_ref[...]
    ms = jnp.mean(x * x, axis=-1, keepdims=True)
    h = (x * lax.rsqrt(ms + EPS) * ln_ref[...]).astype(_BF16)

    ang = freq_ref[...] * pos_ref[...].astype(_F32)
    packed = jnp.concatenate([jnp.cos(ang), jnp.sin(ang), jnp.ones_like(ang),
                              jnp.zeros((LANES - 3 * ROPE_HALF, tm), _F32)], axis=0).T
    lane = lax.broadcasted_iota(jnp.int32, (tm, LANES), 1) % HEAD_DIM
    rotary = lane < ROPE_DIM
    cos_t = jnp.take_along_axis(packed, jnp.where(rotary, lane % ROPE_HALF, 2 * ROPE_HALF), axis=1,
                                mode="promise_in_bounds")
    sin_t = jnp.take_along_axis(packed, jnp.where(rotary, ROPE_HALF + lane % ROPE_HALF, 3 * ROPE_HALF), axis=1,
                                mode="promise_in_bounds")
    first_half = lane < ROPE_HALF
    sin_from_hi = jnp.where(first_half, -sin_t, 0.0)
    sin_from_lo = jnp.where(first_half, 0.0, sin_t)
    seg = seg_ref[...]

    def norm_rope(y, gain):
        ss = jnp.dot((y * y).astype(_BF16), seg, preferred_element_type=_F32)
        yn = y * lax.rsqrt(ss * (1.0 / HEAD_DIM) + EPS) * gain
        up = pltpu.roll(yn, LANES - ROPE_HALF, 1)
        dn = pltpu.roll(yn, ROPE_HALF, 1)
        return yn * cos_t + up * sin_from_hi + dn * sin_from_lo

    def project(w_col):
        return jnp.dot(h, w_ref[:, w_col:w_col + CHUNK], preferred_element_type=_F32)

    tasks = []
    group_out = (qkv0_ref, slab1_ref, slab2_ref)

    def dilated_store(g, slab):
        def store(yh):
            if g == 0:
                qkv0_ref[:, slab * LANES:(slab + 1) * LANES] = yh.astype(_BF16)
            else:
                group_out[g][slab] = yh
        return store

    def row_store(out_ref, col):
        def store(yh):
            out_ref[:, col:col + LANES] = yh.astype(_BF16)
        return store

    n_halves = CHUNK // LANES
    for part in range(3):
        for g in range(N_DIL_GROUPS):
            tasks.append((part * DIL_WIDTH + g * GROUP_WIDTH,
                          [(part if part < 2 else None, dilated_store(g, part * n_halves + i))
                           for i in range(n_halves)]))
    for c in range(QKV_B_WIDTH // CHUNK):
        cols = [c * CHUNK + i * LANES for i in range(n_halves)]
        tasks.append((QKV_A_WIDTH + c * CHUNK,
                      [(2 if col < SWA_Q_WIDTH else (3 if col < SWA_Q_WIDTH + SWA_KV_WIDTH else None),
                        row_store(qkvb_ref, col)) for col in cols]))
    def gate_store(col):
        def store(yh):
            gate_ref[:, col:col + LANES] = (0.5 * jnp.tanh(0.5 * yh) + 0.5).astype(_BF16)
        return store

    for c in range(GATE_WIDTH // CHUNK):
        tasks.append((QKV_A_WIDTH + QKV_B_WIDTH + c * CHUNK,
                      [(None, gate_store(c * CHUNK + i * LANES)) for i in range(n_halves)]))

    def epilogue(y, stores):
        for i, (gain_idx, store) in enumerate(stores):
            yh = y[:, i * LANES:(i + 1) * LANES]
            store(yh if gain_idx is None else norm_rope(yh, gain_ref[gain_idx]))

    in_flight = []
    for w_col, stores in tasks:
        in_flight.append((project(w_col), stores))
        if len(in_flight) > PROJ_LOOKAHEAD:
            epilogue(*in_flight.pop(0))
    for y, stores in in_flight:
        epilogue(y, stores)

    for g, out_ref, slab_ref in ((1, qkv1_ref, slab1_ref), (2, qkv2_ref, slab2_ref)):
        d = DIL_PAIRS[g][1]
        for r in range(d):
            for slab in range(GROUP_QKV_WIDTH // LANES):
                rows = slab_ref[slab, pl.ds(r, tm // d, stride=d), :]
                out_ref[r, :, slab * LANES:(slab + 1) * LANES] = rows.astype(_BF16)


def _in_proj(x2, pos_rows, ln1, w_in, gains, freq, seg, batch, seq):
    n = x2.shape[0]
    tm = TOKEN_TILE
    tiles_per_seq = seq // tm
    row = lambda width: pl.BlockSpec((tm, width), lambda i: (i, 0))

    def deinterleaved(d):
        return pl.BlockSpec((None, d, tm // d, GROUP_QKV_WIDTH),
                            lambda i: (i // tiles_per_seq, 0, i % tiles_per_seq, 0))

    d1, d2 = DIL_PAIRS[1][1], DIL_PAIRS[2][1]
    n_slabs = GROUP_QKV_WIDTH // LANES
    return pl.pallas_call(
        _in_proj_kernel,
        grid=(n // tm,),
        in_specs=[row(D_MODEL), pl.BlockSpec((None, 1, tm), lambda i: (i, 0, 0)), _resident((1, D_MODEL)),
                  _resident(w_in.shape), _resident(gains.shape), _resident(freq.shape), _resident(seg.shape)],
        out_specs=[row(GROUP_QKV_WIDTH), deinterleaved(d1), deinterleaved(d2), row(QKV_B_WIDTH), row(GATE_WIDTH)],
        out_shape=[jax.ShapeDtypeStruct((n, GROUP_QKV_WIDTH), _BF16),
                   jax.ShapeDtypeStruct((batch, d1, seq // d1, GROUP_QKV_WIDTH), _BF16),
                   jax.ShapeDtypeStruct((batch, d2, seq // d2, GROUP_QKV_WIDTH), _BF16),
                   jax.ShapeDtypeStruct((n, QKV_B_WIDTH), _BF16),
                   jax.ShapeDtypeStruct((n, GATE_WIDTH), _BF16)],
        scratch_shapes=[pltpu.VMEM((n_slabs, tm, LANES), _F32), pltpu.VMEM((n_slabs, tm, LANES), _F32)],
        compiler_params=pltpu.CompilerParams(dimension_semantics=("arbitrary",), vmem_limit_bytes=VMEM_LIMIT_BYTES),
        name="in_proj",
    )(x2, pos_rows, ln1, w_in, gains, freq, seg)


_NT = (((1,), (1,)), ((), ()))
_STACK = GROUP_HEADS * BLOCK


def _stacked_scores(q, k_win):
    head_of_lane = lax.broadcasted_iota(jnp.int32, q.shape, 1) // HEAD_DIM
    zero = jnp.zeros_like(q)
    q_stack = jnp.concatenate([jnp.where(head_of_lane == hd, q, zero) for hd in range(GROUP_HEADS)], axis=0)
    return lax.dot_general(q_stack, k_win, _NT, preferred_element_type=_F32)


def _band_bias(max_dist, first_key_col):
    row = lax.broadcasted_iota(jnp.int32, (_STACK, 2 * BLOCK), 0) % BLOCK
    col = lax.broadcasted_iota(jnp.int32, (_STACK, 2 * BLOCK), 1)
    dist = row + BLOCK - col
    return jnp.where((dist >= 0) & (dist <= max_dist) & (col >= first_key_col), 0.0, NEG).astype(_F32)


def _head_rows(hd):
    return slice(hd * BLOCK, (hd + 1) * BLOCK)


def _softmax2(s, bias):
    s = s + bias
    m = jnp.max(s, axis=1, keepdims=True)
    p = jnp.exp2(s - m)
    return p, m, jnp.sum(p, axis=1, keepdims=True)


def _merge_heads(stack_col):
    low = lax.broadcasted_iota(jnp.int32, (BLOCK, LANES), 1) < HEAD_DIM
    return jnp.concatenate([jnp.where(low, stack_col(2 * pair, pair), stack_col(2 * pair + 1, pair))
                            for pair in range(GROUP_HEADS // 2)], axis=1)


def _attend(s, bias, v_win):
    p, m, denom = _softmax2(s, bias)
    pv = jnp.dot(p.astype(_BF16), v_win, preferred_element_type=_F32)
    inv = 1.0 / denom
    lse = m + jnp.log2(denom)
    o = _merge_heads(lambda hd, pair: pv[_head_rows(hd), pair * LANES:(pair + 1) * LANES] * inv[_head_rows(hd)])
    lse = _merge_heads(lambda hd, pair: jnp.broadcast_to(lse[_head_rows(hd)], (BLOCK, LANES)))
    return o, lse


def _pipelined(units, scores_fn, finish_fn):
    s_next = scores_fn(units[0])
    for i, unit in enumerate(units):
        s = s_next
        if i + 1 < len(units):
            s_next = scores_fn(units[i + 1])
        finish_fn(unit, s)


def _window(qb, prev_ref, cur_ref, seq=None):
    idx = (lambda rows: (rows, slice(None))) if seq is None else (lambda rows: (seq, rows, slice(None)))
    rows = slice(qb * BLOCK, (qb + 1) * BLOCK)
    if qb == 0:
        prev = prev_ref[idx(slice(0, BLOCK))]
    else:
        prev = cur_ref[idx(slice((qb - 1) * BLOCK, qb * BLOCK))]
    return jnp.concatenate([prev, cur_ref[idx(rows)]], axis=0)


def _dilated_kernel(q_ref, kprev_ref, k_ref, vprev_ref, v_ref, o_ref, lse_ref, *, max_dist):
    n_seqs, tile, _ = q_ref.shape
    j = pl.program_id(1)
    units = [(seq, qb) for seq in range(n_seqs) for qb in range(tile // BLOCK)]
    bias_first = _band_bias(max_dist, jnp.where(j == 0, BLOCK, 0))
    bias_inner = _band_bias(max_dist, 0)

    def scores(unit):
        seq, qb = unit
        return _stacked_scores(q_ref[seq, qb * BLOCK:(qb + 1) * BLOCK, :], _window(qb, kprev_ref, k_ref, seq))

    def finish(unit, s):
        seq, qb = unit
        o, lse = _attend(s, bias_first if qb == 0 else bias_inner, _window(qb, vprev_ref, v_ref, seq))
        o_ref[seq, qb * BLOCK:(qb + 1) * BLOCK, :] = o.astype(o_ref.dtype)
        lse_ref[seq, qb * BLOCK:(qb + 1) * BLOCK, :] = lse

    _pipelined(units, scores, finish)


def _dilated_attention(qkv, window, dilation):
    n_seqs, length, _ = qkv.shape
    tile = min(ATTN_UNITS * BLOCK, length)
    seqs = ATTN_UNITS * BLOCK // tile
    n_blocks = tile // BLOCK

    def cur(col):
        return pl.BlockSpec((seqs, tile, GROUP_WIDTH), lambda i, j: (i, j, col))

    def prev(col):
        return pl.BlockSpec((seqs, BLOCK, GROUP_WIDTH), lambda i, j: (i, jnp.maximum(j * n_blocks - 1, 0), col))

    out_spec = pl.BlockSpec((seqs, tile, GROUP_WIDTH), lambda i, j: (i, j, 0))
    return pl.pallas_call(
        functools.partial(_dilated_kernel, max_dist=window // dilation),
        grid=(n_seqs // seqs, length // tile),
        in_specs=[cur(0), prev(1), cur(1), prev(2), cur(2)],
        out_specs=[out_spec, out_spec],
        out_shape=[jax.ShapeDtypeStruct((n_seqs, length, GROUP_WIDTH), _BF16),
                   jax.ShapeDtypeStruct((n_seqs, length, GROUP_WIDTH), _F32)],
        compiler_params=pltpu.CompilerParams(dimension_semantics=("arbitrary",) * 2, vmem_limit_bytes=VMEM_LIMIT_BYTES),
        name=f"dilated_attn_d{dilation}",
    )(qkv, qkv, qkv, qkv, qkv)


def _swa_kernel(sink_ref, q_ref, kprev_ref, k_ref, vprev_ref, v_ref, o_ref, kdup_ref, vdup_ref):
    tile = q_ref.shape[0]
    j = pl.program_id(1)
    n_kv_heads = SWA_KV_WIDTH // HEAD_DIM
    units = [(qb, kv) for qb in range(tile // BLOCK) for kv in range(n_kv_heads)]

    def duplicate(dst_ref, row0, t):
        low = lax.broadcasted_iota(jnp.int32, t.shape, 1) < HEAD_DIM
        t32 = t.astype(_F32)
        swapped = pltpu.roll(t32, HEAD_DIM, 1)
        dst_ref[0, row0:row0 + t.shape[0], :] = jnp.where(low, t32, swapped).astype(_BF16)
        dst_ref[1, row0:row0 + t.shape[0], :] = jnp.where(low, swapped, t32).astype(_BF16)

    for dst_ref, prev_ref, cur_ref in ((kdup_ref, kprev_ref, k_ref), (vdup_ref, vprev_ref, v_ref)):
        duplicate(dst_ref, 0, prev_ref[...])
        duplicate(dst_ref, BLOCK, cur_ref[...])

    low = lax.broadcasted_iota(jnp.int32, (BLOCK, LANES), 1) < HEAD_DIM

    def scores(unit):
        qb, kv = unit
        rows = slice(qb * BLOCK, (qb + 1) * BLOCK)
        parts = []
        for pair in range(GROUP_HEADS // 2):
            lane0 = kv * GROUP_WIDTH + pair * LANES
            q_pair = q_ref[rows, lane0:lane0 + LANES]
            zero = jnp.zeros_like(q_pair)
            parts += [jnp.where(low, q_pair, zero), jnp.where(low, zero, q_pair)]
        q_stack = jnp.concatenate(parts, axis=0)
        k_win = kdup_ref[kv, qb * BLOCK:(qb + 2) * BLOCK, :]
        return lax.dot_general(q_stack, k_win, _NT, preferred_element_type=_F32)

    sink_col = lax.broadcasted_iota(jnp.int32, (_STACK, 2 * BLOCK), 1) == 0
    bias_first = jnp.where(sink_col, 0.0, _band_bias(SWA_WINDOW - 1, jnp.where(j == 0, BLOCK, 0)))
    bias_inner = jnp.where(sink_col, 0.0, _band_bias(SWA_WINDOW - 1, 0))
    sink_lane = lax.broadcasted_iota(jnp.int32, (BLOCK, LANES), 1) == 0
    sink_lane_stack = lax.broadcasted_iota(jnp.int32, (_STACK, LANES), 1) == 0

    def finish(unit, s):
        qb, kv = unit
        rows = slice(qb * BLOCK, (qb + 1) * BLOCK)
        s_left = jnp.concatenate([jnp.where(sink_lane, sink_ref[kv * GROUP_HEADS + hd], s[_head_rows(hd), :LANES])
                                  for hd in range(GROUP_HEADS)], axis=0)
        s = jnp.concatenate([s_left, s[:, LANES:]], axis=1)
        p, _, denom = _softmax2(s, bias_first if qb == 0 else bias_inner)
        p = jnp.concatenate([jnp.where(sink_lane_stack, 0.0, p[:, :LANES]), p[:, LANES:]], axis=1).astype(_BF16)
        pv = jnp.dot(p, vdup_ref[kv, qb * BLOCK:(qb + 2) * BLOCK, :], preferred_element_type=_F32)
        inv = 1.0 / denom
        for pair in range(GROUP_HEADS // 2):
            lane0 = kv * GROUP_WIDTH + pair * LANES
            o_pair = jnp.where(low, pv[_head_rows(2 * pair)] * inv[_head_rows(2 * pair)],
                               pv[_head_rows(2 * pair + 1)] * inv[_head_rows(2 * pair + 1)])
            o_ref[rows, lane0:lane0 + LANES] = o_pair.astype(o_ref.dtype)

    _pipelined(units, scores, finish)


def _swa_attention(qkvb3, sinks):
    b, s, _ = qkvb3.shape
    tile = ATTN_UNITS * BLOCK // 2
    n_blocks = tile // BLOCK
    k_col = SWA_Q_WIDTH // LANES
    v_col = k_col + SWA_KV_WIDTH // LANES

    def cur(col):
        return pl.BlockSpec((None, tile, LANES), lambda bi, j: (bi, j, col))

    def prev(col):
        return pl.BlockSpec((None, BLOCK, LANES), lambda bi, j: (bi, jnp.maximum(j * n_blocks - 1, 0), col))

    q_spec = pl.BlockSpec((None, tile, SWA_Q_WIDTH), lambda bi, j: (bi, j, 0))
    o = pl.pallas_call(
        _swa_kernel,
        grid=(b, s // tile),
        in_specs=[pl.BlockSpec(memory_space=pltpu.SMEM), q_spec, prev(k_col), cur(k_col), prev(v_col), cur(v_col)],
        out_specs=q_spec,
        out_shape=jax.ShapeDtypeStruct((b, s, SWA_Q_WIDTH), _BF16),
        scratch_shapes=[pltpu.VMEM((SWA_KV_WIDTH // HEAD_DIM, BLOCK + tile, LANES), _BF16)] * 2,
        compiler_params=pltpu.CompilerParams(dimension_semantics=("arbitrary",) * 2, vmem_limit_bytes=VMEM_LIMIT_BYTES),
        name="swa_attn",
    )(sinks, qkvb3, qkvb3, qkvb3, qkvb3, qkvb3)
    return o.reshape(b * s, SWA_Q_WIDTH)


def _out_mlp_kernel(x_ref, o0_ref, l0_ref, o1_ref, l1_ref, o2_ref, l2_ref, ob_ref, gate_ref,
                    wa_ref, wb_ref, wo_ref, ln_ref, wup_ref, wdn_ref, out_ref, slab_ref):
    tm = x_ref.shape[0]
    n_slabs = GROUP_WIDTH // LANES

    def interleaved(src_ref, base):
        d = src_ref.shape[0]
        for r in range(d):
            for slab in range(n_slabs):
                slab_ref[base + slab, pl.ds(r, tm // d, stride=d), :] = (
                    src_ref[r, :, slab * LANES:(slab + 1) * LANES].astype(_F32))
        return jnp.concatenate([slab_ref[base + slab] for slab in range(n_slabs)], axis=1)

    o0, l0 = o0_ref[...].astype(_F32), l0_ref[...]
    o1, l1 = interleaved(o1_ref, 0), interleaved(l1_ref, n_slabs)
    o2, l2 = interleaved(o2_ref, 2 * n_slabs), interleaved(l2_ref, 3 * n_slabs)
    m = jnp.maximum(jnp.maximum(l0, l1), l2)
    e0, e1, e2 = jnp.exp2(l0 - m), jnp.exp2(l1 - m), jnp.exp2(l2 - m)
    oa = ((e0 * o0 + e1 * o1 + e2 * o2) / (e0 + e1 + e2)).astype(_BF16)

    ya = jnp.dot(oa, wa_ref[...], preferred_element_type=_F32)
    yb = jnp.dot(ob_ref[...], wb_ref[...], preferred_element_type=_F32)
    gate_a = gate_ref[:, :D_MODEL].astype(_F32)
    gate_b = gate_ref[:, D_MODEL:].astype(_F32)
    mix = (gate_a * ya + gate_b * yb).astype(_BF16)
    x1 = x_ref[...] + jnp.dot(mix, wo_ref[...], preferred_element_type=_F32)

    ms = jnp.mean(x1 * x1, axis=-1, keepdims=True)
    h2 = (x1 * lax.rsqrt(ms + EPS) * ln_ref[...]).astype(_BF16)
    ff_chunk = 512
    acc = x1
    for c in range(D_FF // ff_chunk):
        u = jnp.dot(h2, wup_ref[:, c * ff_chunk:(c + 1) * ff_chunk], preferred_element_type=_F32)
        a = jnp.square(jnp.maximum(u, 0.0)).astype(_BF16)
        acc = acc + jnp.dot(a, wdn_ref[c * ff_chunk:(c + 1) * ff_chunk, :], preferred_element_type=_F32)
    out_ref[...] = acc


def _out_mlp(x2, attn_a, ob, gates, wa, wb, wo, ln2, wup, wdn, seq):
    n = x2.shape[0]
    tm = TOKEN_TILE
    tiles_per_seq = seq // tm
    row = lambda width: pl.BlockSpec((tm, width), lambda i: (i, 0))

    def deinterleaved(d):
        return pl.BlockSpec((None, d, tm // d, GROUP_WIDTH), lambda i: (i // tiles_per_seq, 0, i % tiles_per_seq, 0))

    group_specs = [row(GROUP_WIDTH)] * 2
    for _, d in DIL_PAIRS[1:]:
        group_specs += [deinterleaved(d)] * 2
    return pl.pallas_call(
        _out_mlp_kernel,
        grid=(n // tm,),
        in_specs=[row(D_MODEL)] + group_specs + [row(SWA_Q_WIDTH), row(GATE_WIDTH),
                  _resident(wa.shape), _resident(wb.shape), _resident(wo.shape), _resident((1, D_MODEL)),
                  _resident(wup.shape), _resident(wdn.shape)],
        out_specs=row(D_MODEL),
        out_shape=jax.ShapeDtypeStruct((n, D_MODEL), _F32),
        scratch_shapes=[pltpu.VMEM((4 * GROUP_WIDTH // LANES, tm, LANES), _F32)],
        compiler_params=pltpu.CompilerParams(dimension_semantics=("arbitrary",), vmem_limit_bytes=VMEM_LIMIT_BYTES),
        name="out_mlp",
    )(x2, *[a for pair in attn_a for a in pair], ob, gates, wa, wb, wo, ln2, wup, wdn)


def _rope_inv_freq():
    return (ROPE_THETA ** (-jnp.arange(0, ROPE_DIM, 2, dtype=_F32) / ROPE_DIM))[:, None]


def kernel(x, positions, ln1_g, w_in, q_norm_a, k_norm_a, q_norm_b, k_norm_b, sinks,
           w_branch_a, w_branch_b, w_out, ln2_g, w_up, w_down):
    b, s, d_model = x.shape
    assert d_model == D_MODEL and ln1_g.shape[0] == 1 and s % (16 * BLOCK) == 0 and s % TOKEN_TILE == 0
    x2 = x.reshape(b * s, d_model)
    scale = np.float32(np.log2(np.e) / np.sqrt(HEAD_DIM))
    gains = jnp.stack([jnp.tile(q_norm_a[0] * scale, 2), jnp.tile(k_norm_a[0], 2),
                       jnp.tile(q_norm_b[0] * scale, 2), jnp.tile(k_norm_b[0], 2)])[:, None, :]
    head = np.arange(LANES) // HEAD_DIM
    seg = jnp.asarray(head[:, None] == head[None, :], _BF16)

    pos_rows = positions.reshape(b * s // TOKEN_TILE, 1, TOKEN_TILE)
    qkv0, qkv1, qkv2, qkvb, gates = _in_proj(x2, pos_rows, ln1_g, w_in[0].astype(_BF16), gains, _rope_inv_freq(),
                                             seg, b, s)
    attn_a = []
    for (window, d), qkv in zip(DIL_PAIRS, (qkv0.reshape(b, s, GROUP_QKV_WIDTH), qkv1, qkv2)):
        o, lse = _dilated_attention(qkv.reshape(b * d, s // d, GROUP_QKV_WIDTH), window, d)
        shape = (b * s, GROUP_WIDTH) if d == 1 else (b, d, s // d, GROUP_WIDTH)
        attn_a.append((o.reshape(shape), lse.reshape(shape)))
    ob = _swa_attention(qkvb.reshape(b, s, QKV_B_WIDTH), sinks[0] * np.float32(np.log2(np.e)))
    out = _out_mlp(x2, attn_a, ob, gates,
                   w_branch_a[0].astype(_BF16), w_branch_b[0].astype(_BF16), w_out[0].astype(_BF16),
                   ln2_g, w_up[0].astype(_BF16), w_down[0].astype(_BF16), s)
    return out.reshape(b, s, d_model)
```

```python
import functools
import itertools

import numpy as np
import jax
import jax.numpy as jnp
from jax import lax
from jax.experimental import pallas as pl
from jax.experimental.pallas import tpu as pltpu

D_MODEL = 1024
HEAD_DIM = 64
DIL_PAIRS = ((128, 1), (512, 4), (2048, 16))
GROUP_HEADS = 4
GROUP_WIDTH = GROUP_HEADS * HEAD_DIM
DIL_WIDTH = 768
N_DIL_GROUPS = DIL_WIDTH // GROUP_WIDTH
GROUP_QKV_WIDTH = 3 * GROUP_WIDTH
SWA_WINDOW = 128
SWA_Q_WIDTH = 512
SWA_KV_WIDTH = 128
QKV_A_WIDTH = 3 * DIL_WIDTH
QKV_B_WIDTH = SWA_Q_WIDTH + 2 * SWA_KV_WIDTH
GATE_WIDTH = 2 * D_MODEL
D_FF = 4 * D_MODEL
ROPE_THETA = 500000.0
ROPE_DIM = HEAD_DIM // 4
ROPE_HALF = ROPE_DIM // 2
BLOCK = 128
EPS = 1e-6
NEG = -1e30

LANES = 128
CHUNK = 256
TOKEN_TILE = 512
PROJ_LOOKAHEAD = 2
ATTN_LOOKAHEAD = 1
STREAM_INTERLEAVE = 2
VMEM_LIMIT_BYTES = 56 * 1024 * 1024

_BF16 = jnp.bfloat16
_F32 = jnp.float32


def _resident(shape):
    return pl.BlockSpec(shape, lambda *_: (0,) * len(shape), pipeline_mode=pl.Buffered(1))


class _Component:
    def __init__(self, inputs, in_specs, out_specs, out_shapes, scratch_shapes, stream):
        self.inputs, self.in_specs, self.out_specs = list(inputs), list(in_specs), list(out_specs)
        self.out_shapes, self.scratch_shapes, self.stream = list(out_shapes), list(scratch_shapes), stream


def _interleave(primary, secondary, ratio):
    live = {"p": primary, "s": secondary}

    def advance(key):
        if live[key] is not None:
            try:
                next(live[key])
            except StopIteration:
                live[key] = None

    while live["p"] is not None or live["s"] is not None:
        advance("p")
        for _ in range(ratio):
            advance("s")


def _run_streams(components, n_steps, name):
    n_in = [len(c.inputs) for c in components]
    n_out = [len(c.out_specs) for c in components]
    n_scr = [len(c.scratch_shapes) for c in components]

    def body(*refs):
        refs = list(refs)
        ins, outs, scrs = [], [], []
        pos = 0
        for group, counts in ((ins, n_in), (outs, n_out), (scrs, n_scr)):
            for k in counts:
                group.append(refs[pos:pos + k])
                pos += k
        step = pl.program_id(0)
        streams = [c.stream(i, o, s, step) for c, i, o, s in zip(components, ins, outs, scrs)]
        _interleave(streams[0], itertools.chain(*streams[1:]), STREAM_INTERLEAVE)

    outs = pl.pallas_call(
        body,
        grid=(n_steps,),
        in_specs=[s for c in components for s in c.in_specs],
        out_specs=[s for c in components for s in c.out_specs],
        out_shape=[s for c in components for s in c.out_shapes],
        scratch_shapes=[s for c in components for s in c.scratch_shapes],
        compiler_params=pltpu.CompilerParams(dimension_semantics=("arbitrary",), vmem_limit_bytes=VMEM_LIMIT_BYTES),
        name=name,
    )(*[a for c in components for a in c.inputs])
    split, pos = [], 0
    for k in n_out:
        split.append(outs[pos:pos + k])
        pos += k
    return split


def _in_proj_stream(in_refs, out_refs, scratch_refs, step):
    del step
    x_ref, pos_ref, ln_ref, w_ref, gain_ref, freq_ref, seg_ref = in_refs
    qkv0_ref, qkv1_ref, qkv2_ref, qkvb_ref, gate_ref = out_refs
    slab1_ref, slab2_ref = scratch_refs
    tm = x_ref.shape[0]
    x = x_ref[...]
    ms = jnp.mean(x * x, axis=-1, keepdims=True)
    h = (x * lax.rsqrt(ms + EPS) * ln_ref[...]).astype(_BF16)

    ang = freq_ref[...] * pos_ref[...].astype(_F32)
    packed = jnp.concatenate([jnp.cos(ang), jnp.sin(ang), jnp.ones_like(ang),
                              jnp.zeros((LANES - 3 * ROPE_HALF, tm), _F32)], axis=0).T
    lane = lax.broadcasted_iota(jnp.int32, (tm, LANES), 1) % HEAD_DIM
    rotary = lane < ROPE_DIM
    cos_t = jnp.take_along_axis(packed, jnp.where(rotary, lane % ROPE_HALF, 2 * ROPE_HALF), axis=1,
                                mode="promise_in_bounds")
    sin_t = jnp.take_along_axis(packed, jnp.where(rotary, ROPE_HALF + lane % ROPE_HALF, 3 * ROPE_HALF), axis=1,
                                mode="promise_in_bounds")
    first_half = lane < ROPE_HALF
    sin_from_hi = jnp.where(first_half, -sin_t, 0.0)
    sin_from_lo = jnp.where(first_half, 0.0, sin_t)
    seg = seg_ref[...]

    def norm_rope(y, gain):
        ss = jnp.dot((y * y).astype(_BF16), seg, preferred_element_type=_F32)
        yn = y * lax.rsqrt(ss * (1.0 / HEAD_DIM) + EPS) * gain
        up = pltpu.roll(yn, LANES - ROPE_HALF, 1)
        dn = pltpu.roll(yn, ROPE_HALF, 1)
        return yn * cos_t + up * sin_from_hi + dn * sin_from_lo

    def project(w_col):
        return jnp.dot(h, w_ref[:, w_col:w_col + CHUNK], preferred_element_type=_F32)

    tasks = []
    group_out = (qkv0_ref, slab1_ref, slab2_ref)

    def dilated_store(g, slab):
        def store(yh):
            if g == 0:
                qkv0_ref[:, slab * LANES:(slab + 1) * LANES] = yh.astype(_BF16)
            else:
                group_out[g][slab] = yh
        return store

    def row_store(out_ref, col):
        def store(yh):
            out_ref[:, col:col + LANES] = yh.astype(_BF16)
        return store

    def gate_store(col):
        def store(yh):
            gate_ref[:, col:col + LANES] = (0.5 * jnp.tanh(0.5 * yh) + 0.5).astype(_BF16)
        return store

    n_halves = CHUNK // LANES
    for part in range(3):
        for g in range(N_DIL_GROUPS):
            tasks.append((part * DIL_WIDTH + g * GROUP_WIDTH,
                          [(part if part < 2 else None, dilated_store(g, part * n_halves + i))
                           for i in range(n_halves)]))
    for c in range(QKV_B_WIDTH // CHUNK):
        cols = [c * CHUNK + i * LANES for i in range(n_halves)]
        tasks.append((QKV_A_WIDTH + c * CHUNK,
                      [(2 if col < SWA_Q_WIDTH else (3 if col < SWA_Q_WIDTH + SWA_KV_WIDTH else None),
                        row_store(qkvb_ref, col)) for col in cols]))
    for c in range(GATE_WIDTH // CHUNK):
        tasks.append((QKV_A_WIDTH + QKV_B_WIDTH + c * CHUNK,
                      [(None, gate_store(c * CHUNK + i * LANES)) for i in range(n_halves)]))

    def epilogue(y, stores):
        for i, (gain_idx, store) in enumerate(stores):
            yh = y[:, i * LANES:(i + 1) * LANES]
            store(yh if gain_idx is None else norm_rope(yh, gain_ref[gain_idx]))

    in_flight = []
    for w_col, stores in tasks:
        in_flight.append((project(w_col), stores))
        if len(in_flight) > PROJ_LOOKAHEAD:
            epilogue(*in_flight.pop(0))
        yield
    for y, stores in in_flight:
        epilogue(y, stores)

    for g, out_ref, slab_ref in ((1, qkv1_ref, slab1_ref), (2, qkv2_ref, slab2_ref)):
        d = DIL_PAIRS[g][1]
        for r in range(d):
            for slab in range(GROUP_QKV_WIDTH // LANES):
                rows = slab_ref[slab, pl.ds(r, tm // d, stride=d), :]
                out_ref[r, :, slab * LANES:(slab + 1) * LANES] = rows.astype(_BF16)
    yield


def _in_proj_component(x2, pos_rows, ln1, w_in, gains, freq, seg, batch, seq):
    tm = TOKEN_TILE
    n = batch * seq
    tiles_per_seq = seq // tm
    out_row = lambda width: pl.BlockSpec((tm, width), lambda i: (i, 0))

    def deinterleaved(d):
        return pl.BlockSpec((None, d, tm // d, GROUP_QKV_WIDTH),
                            lambda i: (i // tiles_per_seq, 0, i % tiles_per_seq, 0))

    d1, d2 = DIL_PAIRS[1][1], DIL_PAIRS[2][1]
    n_slabs = GROUP_QKV_WIDTH // LANES
    return _Component(
        inputs=[x2, pos_rows, ln1, w_in, gains, freq, seg],
        in_specs=[out_row(D_MODEL), pl.BlockSpec((None, 1, tm), lambda i: (i, 0, 0)),
                  _resident((1, D_MODEL)), _resident(w_in.shape), _resident(gains.shape), _resident(freq.shape),
                  _resident(seg.shape)],
        out_specs=[out_row(GROUP_QKV_WIDTH), deinterleaved(d1), deinterleaved(d2), out_row(QKV_B_WIDTH),
                   out_row(GATE_WIDTH)],
        out_shapes=[jax.ShapeDtypeStruct((n, GROUP_QKV_WIDTH), _BF16),
                    jax.ShapeDtypeStruct((batch, d1, seq // d1, GROUP_QKV_WIDTH), _BF16),
                    jax.ShapeDtypeStruct((batch, d2, seq // d2, GROUP_QKV_WIDTH), _BF16),
                    jax.ShapeDtypeStruct((n, QKV_B_WIDTH), _BF16),
                    jax.ShapeDtypeStruct((n, GATE_WIDTH), _BF16)],
        scratch_shapes=[pltpu.VMEM((n_slabs, tm, LANES), _F32), pltpu.VMEM((n_slabs, tm, LANES), _F32)],
        stream=_in_proj_stream)


_NT = (((1,), (1,)), ((), ()))
_STACK = GROUP_HEADS * BLOCK


def _stacked_scores(q, k_win):
    head_of_lane = lax.broadcasted_iota(jnp.int32, q.shape, 1) // HEAD_DIM
    zero = jnp.zeros_like(q)
    q_stack = jnp.concatenate([jnp.where(head_of_lane == hd, q, zero) for hd in range(GROUP_HEADS)], axis=0)
    return lax.dot_general(q_stack, k_win, _NT, preferred_element_type=_F32)


def _band_bias(max_dist, first_key_col):
    row = lax.broadcasted_iota(jnp.int32, (_STACK, 2 * BLOCK), 0) % BLOCK
    col = lax.broadcasted_iota(jnp.int32, (_STACK, 2 * BLOCK), 1)
    dist = row + BLOCK - col
    return jnp.where((dist >= 0) & (dist <= max_dist) & (col >= first_key_col), 0.0, NEG).astype(_F32)


def _head_rows(hd):
    return slice(hd * BLOCK, (hd + 1) * BLOCK)


def _softmax2(s, bias):
    s = s + bias
    m = jnp.max(s, axis=1, keepdims=True)
    p = jnp.exp2(s - m)
    return p, m, jnp.sum(p, axis=1, keepdims=True)


def _merge_heads(stack_col):
    low = lax.broadcasted_iota(jnp.int32, (BLOCK, LANES), 1) < HEAD_DIM
    return jnp.concatenate([jnp.where(low, stack_col(2 * pair, pair), stack_col(2 * pair + 1, pair))
                            for pair in range(GROUP_HEADS // 2)], axis=1)


def _attend(s, bias, v_win):
    p, m, denom = _softmax2(s, bias)
    pv = jnp.dot(p.astype(_BF16), v_win, preferred_element_type=_F32)
    inv = 1.0 / denom
    lse = m + jnp.log2(denom)
    o = _merge_heads(lambda hd, pair: pv[_head_rows(hd), pair * LANES:(pair + 1) * LANES] * inv[_head_rows(hd)])
    lse = _merge_heads(lambda hd, pair: jnp.broadcast_to(lse[_head_rows(hd)], (BLOCK, LANES)))
    return o, lse


def _unit_stream(units, scores_fn, finish_fn):
    in_flight = []
    for unit in units:
        in_flight.append((unit, scores_fn(unit)))
        yield
        if len(in_flight) > ATTN_LOOKAHEAD:
            finish_fn(*in_flight.pop(0))
            yield
    for unit, s in in_flight:
        finish_fn(unit, s)
        yield


def _window(qb, prev_ref, cur_ref, seq=None):
    idx = (lambda rows: (rows, slice(None))) if seq is None else (lambda rows: (seq, rows, slice(None)))
    rows = slice(qb * BLOCK, (qb + 1) * BLOCK)
    if qb == 0:
        prev = prev_ref[idx(slice(0, BLOCK))]
    else:
        prev = cur_ref[idx(slice((qb - 1) * BLOCK, qb * BLOCK))]
    return jnp.concatenate([prev, cur_ref[idx(rows)]], axis=0)


def _dilated_stream(in_refs, out_refs, scratch_refs, step, *, max_dist, tiles_per_seq):
    del scratch_refs
    q_ref, kprev_ref, k_ref, vprev_ref, v_ref = in_refs
    o_ref, lse_ref = out_refs
    n_seqs, tile, _ = q_ref.shape
    first_tile = (step % tiles_per_seq) == 0
    units = [(seq, qb) for seq in range(n_seqs) for qb in range(tile // BLOCK)]
    bias_first = _band_bias(max_dist, jnp.where(first_tile, BLOCK, 0))
    bias_inner = _band_bias(max_dist, 0)

    def scores(unit):
        seq, qb = unit
        return _stacked_scores(q_ref[seq, qb * BLOCK:(qb + 1) * BLOCK, :], _window(qb, kprev_ref, k_ref, seq))

    def finish(unit, s):
        seq, qb = unit
        o, lse = _attend(s, bias_first if qb == 0 else bias_inner, _window(qb, vprev_ref, v_ref, seq))
        o_ref[seq, qb * BLOCK:(qb + 1) * BLOCK, :] = o.astype(o_ref.dtype)
        lse_ref[seq, qb * BLOCK:(qb + 1) * BLOCK, :] = lse

    return _unit_stream(units, scores, finish)


def _dilated_component(qkv, window, dilation):
    n_seqs, length, _ = qkv.shape
    tile = min(TOKEN_TILE, length)
    seqs = TOKEN_TILE // tile
    n_blocks = tile // BLOCK
    tiles_per_seq = length // tile

    def cur(col):
        return pl.BlockSpec((seqs, tile, GROUP_WIDTH), lambda i: (i // tiles_per_seq, i % tiles_per_seq, col))

    def prev(col):
        return pl.BlockSpec((seqs, BLOCK, GROUP_WIDTH),
                            lambda i: (i // tiles_per_seq, jnp.maximum((i % tiles_per_seq) * n_blocks - 1, 0), col))

    out_spec = pl.BlockSpec((seqs, tile, GROUP_WIDTH), lambda i: (i // tiles_per_seq, i % tiles_per_seq, 0))
    return _Component(
        inputs=[qkv] * 5,
        in_specs=[cur(0), prev(1), cur(1), prev(2), cur(2)],
        out_specs=[out_spec, out_spec],
        out_shapes=[jax.ShapeDtypeStruct((n_seqs, length, GROUP_WIDTH), _BF16),
                    jax.ShapeDtypeStruct((n_seqs, length, GROUP_WIDTH), _F32)],
        scratch_shapes=[],
        stream=functools.partial(_dilated_stream, max_dist=window // dilation, tiles_per_seq=tiles_per_seq))


def _swa_stream(in_refs, out_refs, scratch_refs, step, *, tiles_per_seq):
    sink_ref, q_ref, kprev_ref, k_ref, vprev_ref, v_ref = in_refs
    (o_ref,) = out_refs
    kdup_ref, vdup_ref = scratch_refs
    tile = q_ref.shape[0]
    first_tile = (step % tiles_per_seq) == 0
    n_kv_heads = SWA_KV_WIDTH // HEAD_DIM
    units = [(qb, kv) for qb in range(tile // BLOCK) for kv in range(n_kv_heads)]

    def duplicate(dst_ref, row0, t):
        low = lax.broadcasted_iota(jnp.int32, t.shape, 1) < HEAD_DIM
        t32 = t.astype(_F32)
        swapped = pltpu.roll(t32, HEAD_DIM, 1)
        dst_ref[0, row0:row0 + t.shape[0], :] = jnp.where(low, t32, swapped).astype(_BF16)
        dst_ref[1, row0:row0 + t.shape[0], :] = jnp.where(low, swapped, t32).astype(_BF16)

    for dst_ref, prev_ref, cur_ref in ((kdup_ref, kprev_ref, k_ref), (vdup_ref, vprev_ref, v_ref)):
        duplicate(dst_ref, 0, prev_ref[...])
        duplicate(dst_ref, BLOCK, cur_ref[...])

    low = lax.broadcasted_iota(jnp.int32, (BLOCK, LANES), 1) < HEAD_DIM

    def scores(unit):
        qb, kv = unit
        rows = slice(qb * BLOCK, (qb + 1) * BLOCK)
        parts = []
        for pair in range(GROUP_HEADS // 2):
            lane0 = kv * GROUP_WIDTH + pair * LANES
            q_pair = q_ref[rows, lane0:lane0 + LANES]
            zero = jnp.zeros_like(q_pair)
            parts += [jnp.where(low, q_pair, zero), jnp.where(low, zero, q_pair)]
        q_stack = jnp.concatenate(parts, axis=0)
        k_win = kdup_ref[kv, qb * BLOCK:(qb + 2) * BLOCK, :]
        return lax.dot_general(q_stack, k_win, _NT, preferred_element_type=_F32)

    sink_col = lax.broadcasted_iota(jnp.int32, (_STACK, 2 * BLOCK), 1) == 0
    bias_first = jnp.where(sink_col, 0.0, _band_bias(SWA_WINDOW - 1, jnp.where(first_tile, BLOCK, 0)))
    bias_inner = jnp.where(sink_col, 0.0, _band_bias(SWA_WINDOW - 1, 0))
    sink_lane = lax.broadcasted_iota(jnp.int32, (BLOCK, LANES), 1) == 0
    sink_lane_stack = lax.broadcasted_iota(jnp.int32, (_STACK, LANES), 1) == 0

    def finish(unit, s):
        qb, kv = unit
        rows = slice(qb * BLOCK, (qb + 1) * BLOCK)
        s_left = jnp.concatenate([jnp.where(sink_lane, sink_ref[kv * GROUP_HEADS + hd], s[_head_rows(hd), :LANES])
                                  for hd in range(GROUP_HEADS)], axis=0)
        s = jnp.concatenate([s_left, s[:, LANES:]], axis=1)
        p, _, denom = _softmax2(s, bias_first if qb == 0 else bias_inner)
        p = jnp.concatenate([jnp.where(sink_lane_stack, 0.0, p[:, :LANES]), p[:, LANES:]], axis=1).astype(_BF16)
        pv = jnp.dot(p, vdup_ref[kv, qb * BLOCK:(qb + 2) * BLOCK, :], preferred_element_type=_F32)
        inv = 1.0 / denom
        for pair in range(GROUP_HEADS // 2):
            lane0 = kv * GROUP_WIDTH + pair * LANES
            o_pair = jnp.where(low, pv[_head_rows(2 * pair)] * inv[_head_rows(2 * pair)],
                               pv[_head_rows(2 * pair + 1)] * inv[_head_rows(2 * pair + 1)])
            o_ref[rows, lane0:lane0 + LANES] = o_pair.astype(o_ref.dtype)

    return _unit_stream(units, scores, finish)


def _swa_component(qkvb3, sinks):
    b, s, _ = qkvb3.shape
    tile = TOKEN_TILE
    n_blocks = tile // BLOCK
    tiles_per_seq = s // tile
    k_col = SWA_Q_WIDTH // LANES
    v_col = k_col + SWA_KV_WIDTH // LANES

    def cur(col):
        return pl.BlockSpec((None, tile, LANES), lambda i: (i // tiles_per_seq, i % tiles_per_seq, col))

    def prev(col):
        return pl.BlockSpec((None, BLOCK, LANES),
                            lambda i: (i // tiles_per_seq, jnp.maximum((i % tiles_per_seq) * n_blocks - 1, 0), col))

    q_spec = pl.BlockSpec((None, tile, SWA_Q_WIDTH), lambda i: (i // tiles_per_seq, i % tiles_per_seq, 0))
    return _Component(
        inputs=[sinks] + [qkvb3] * 5,
        in_specs=[pl.BlockSpec(memory_space=pltpu.SMEM), q_spec, prev(k_col), cur(k_col), prev(v_col), cur(v_col)],
        out_specs=[q_spec],
        out_shapes=[jax.ShapeDtypeStruct((b, s, SWA_Q_WIDTH), _BF16)],
        scratch_shapes=[pltpu.VMEM((SWA_KV_WIDTH // HEAD_DIM, BLOCK + tile, LANES), _BF16)] * 2,
        stream=functools.partial(_swa_stream, tiles_per_seq=tiles_per_seq))


def _attention_components(proj, batch, seq, sinks2):
    qkv0, qkv1, qkv2, qkvb, _ = proj
    comps = []
    for (window, d), qkv in zip(DIL_PAIRS, (qkv0, qkv1, qkv2)):
        comps.append(_dilated_component(qkv.reshape(batch * d, seq // d, GROUP_QKV_WIDTH), window, d))
    comps.append(_swa_component(qkvb.reshape(batch, seq, QKV_B_WIDTH), sinks2))
    return comps


def _attention_outputs(outs, batch, seq):
    attn_a = []
    for (_, d), (o, lse) in zip(DIL_PAIRS, outs[:3]):
        shape = (batch * seq, GROUP_WIDTH) if d == 1 else (batch, d, seq // d, GROUP_WIDTH)
        attn_a.append((o.reshape(shape), lse.reshape(shape)))
    return attn_a, outs[3][0].reshape(batch * seq, SWA_Q_WIDTH)


def _out_mlp_stream(in_refs, out_refs, scratch_refs, step):
    del step
    (x_ref, o0_ref, l0_ref, o1_ref, l1_ref, o2_ref, l2_ref, ob_ref, gate_ref,
     wa_ref, wb_ref, wo_ref, ln_ref, wup_ref, wdn_ref) = in_refs
    (out_ref,) = out_refs
    (slab_ref,) = scratch_refs
    tm = x_ref.shape[0]
    n_slabs = GROUP_WIDTH // LANES

    def interleaved(src_ref, base):
        d = src_ref.shape[0]
        for r in range(d):
            for slab in range(n_slabs):
                slab_ref[base + slab, pl.ds(r, tm // d, stride=d), :] = (
                    src_ref[r, :, slab * LANES:(slab + 1) * LANES].astype(_F32))
        return jnp.concatenate([slab_ref[base + slab] for slab in range(n_slabs)], axis=1)

    o0, l0 = o0_ref[...].astype(_F32), l0_ref[...]
    o1, l1 = interleaved(o1_ref, 0), interleaved(l1_ref, n_slabs)
    o2, l2 = interleaved(o2_ref, 2 * n_slabs), interleaved(l2_ref, 3 * n_slabs)
    m = jnp.maximum(jnp.maximum(l0, l1), l2)
    e0, e1, e2 = jnp.exp2(l0 - m), jnp.exp2(l1 - m), jnp.exp2(l2 - m)
    oa = ((e0 * o0 + e1 * o1 + e2 * o2) / (e0 + e1 + e2)).astype(_BF16)

    ya = jnp.dot(oa, wa_ref[...], preferred_element_type=_F32)
    yb = jnp.dot(ob_ref[...], wb_ref[...], preferred_element_type=_F32)
    yield
    gate_a = gate_ref[:, :D_MODEL].astype(_F32)
    gate_b = gate_ref[:, D_MODEL:].astype(_F32)
    mix = (gate_a * ya + gate_b * yb).astype(_BF16)
    x1 = x_ref[...] + jnp.dot(mix, wo_ref[...], preferred_element_type=_F32)
    yield

    ms = jnp.mean(x1 * x1, axis=-1, keepdims=True)
    h2 = (x1 * lax.rsqrt(ms + EPS) * ln_ref[...]).astype(_BF16)
    ff_chunk = 512
    acc = x1
    for c in range(D_FF // ff_chunk):
        u = jnp.dot(h2, wup_ref[:, c * ff_chunk:(c + 1) * ff_chunk], preferred_element_type=_F32)
        yield
        a = jnp.square(jnp.maximum(u, 0.0)).astype(_BF16)
        acc = acc + jnp.dot(a, wdn_ref[c * ff_chunk:(c + 1) * ff_chunk, :], preferred_element_type=_F32)
        yield
    out_ref[...] = acc


def _out_mlp_component(x2, attn_a, ob, gates, wa, wb, wo, ln2, wup, wdn, seq):
    tm = TOKEN_TILE
    tiles_per_seq = seq // tm
    row = lambda width: pl.BlockSpec((tm, width), lambda i: (i, 0))
    full_row = row(D_MODEL)

    def deinterleaved(d):
        return pl.BlockSpec((None, d, tm // d, GROUP_WIDTH), lambda i: (i // tiles_per_seq, 0, i % tiles_per_seq, 0))

    group_specs = [row(GROUP_WIDTH)] * 2
    for _, d in DIL_PAIRS[1:]:
        group_specs += [deinterleaved(d)] * 2
    return _Component(
        inputs=[x2] + [a for pair in attn_a for a in pair] + [ob, gates, wa, wb, wo, ln2, wup, wdn],
        in_specs=[full_row] + group_specs + [row(SWA_Q_WIDTH), row(GATE_WIDTH),
                  _resident(wa.shape), _resident(wb.shape), _resident(wo.shape), _resident((1, D_MODEL)),
                  _resident(wup.shape), _resident(wdn.shape)],
        out_specs=[full_row],
        out_shapes=[jax.ShapeDtypeStruct(x2.shape, _F32)],
        scratch_shapes=[pltpu.VMEM((4 * GROUP_WIDTH // LANES, tm, LANES), _F32)],
        stream=_out_mlp_stream)


def _rope_inv_freq():
    return (ROPE_THETA ** (-jnp.arange(0, ROPE_DIM, 2, dtype=_F32) / ROPE_DIM))[:, None]


def kernel(x, positions, ln1_g, w_in, q_norm_a, k_norm_a, q_norm_b, k_norm_b, sinks,
           w_branch_a, w_branch_b, w_out, ln2_g, w_up, w_down):
    b, s, d_model = x.shape
    assert d_model == D_MODEL and ln1_g.shape[0] == 1 and s % (16 * BLOCK) == 0 and s % TOKEN_TILE == 0
    n_tiles = b * s // TOKEN_TILE
    x2 = x.reshape(b * s, d_model)
    log2e = np.float32(np.log2(np.e))
    scale = np.float32(log2e / np.sqrt(HEAD_DIM))
    gains = jnp.stack([jnp.tile(q_norm_a[0] * scale, 2), jnp.tile(k_norm_a[0], 2),
                       jnp.tile(q_norm_b[0] * scale, 2), jnp.tile(k_norm_b[0], 2)])[:, None, :]
    head = np.arange(LANES) // HEAD_DIM
    seg = jnp.asarray(head[:, None] == head[None, :], _BF16)
    pos_rows = positions.reshape(b * s // TOKEN_TILE, 1, TOKEN_TILE)
    sinks2 = sinks[0] * log2e

    (proj,) = _run_streams(
        [_in_proj_component(x2, pos_rows, ln1_g, w_in[0].astype(_BF16), gains, _rope_inv_freq(), seg, b, s)],
        n_tiles, "in_proj")
    attn_a, ob = _attention_outputs(_run_streams(_attention_components(proj, b, s, sinks2), n_tiles, "attention"),
                                    b, s)
    ((out,),) = _run_streams(
        [_out_mlp_component(x2, attn_a, ob, proj[4],
                            w_branch_a[0].astype(_BF16), w_branch_b[0].astype(_BF16), w_out[0].astype(_BF16),
                            ln2_g, w_up[0].astype(_BF16), w_down[0].astype(_BF16), s)],
        n_tiles, "out_mlp")
    return out.reshape(b, s, d_model)
```

```python
import functools
import itertools

import numpy as np
import jax
import jax.numpy as jnp
from jax import lax
from jax.experimental import pallas as pl
from jax.experimental.pallas import tpu as pltpu

D_MODEL = 1024
HEAD_DIM = 64
DIL_PAIRS = ((128, 1), (512, 4), (2048, 16))
GROUP_HEADS = 4
GROUP_WIDTH = GROUP_HEADS * HEAD_DIM
DIL_WIDTH = 768
N_DIL_GROUPS = DIL_WIDTH // GROUP_WIDTH
GROUP_QKV_WIDTH = 3 * GROUP_WIDTH
SWA_WINDOW = 128
SWA_Q_WIDTH = 512
SWA_KV_WIDTH = 128
QKV_A_WIDTH = 3 * DIL_WIDTH
QKV_B_WIDTH = SWA_Q_WIDTH + 2 * SWA_KV_WIDTH
GATE_WIDTH = 2 * D_MODEL
D_FF = 4 * D_MODEL
ROPE_THETA = 500000.0
ROPE_DIM = HEAD_DIM // 4
ROPE_HALF = ROPE_DIM // 2
BLOCK = 128
EPS = 1e-6
NEG = -1e30

LANES = 128
CHUNK = 256
TOKEN_TILE = 512
IN_PROJ_TILE = 1024
PROJ_LOOKAHEAD = 2
ATTN_LOOKAHEAD = 1
STREAM_INTERLEAVE = 2
VMEM_LIMIT_BYTES = 56 * 1024 * 1024

_BF16 = jnp.bfloat16
_F32 = jnp.float32


def _resident(shape):
    return pl.BlockSpec(shape, lambda *_: (0,) * len(shape), pipeline_mode=pl.Buffered(1))


class _Component:
    def __init__(self, inputs, in_specs, out_specs, out_shapes, scratch_shapes, stream):
        self.inputs, self.in_specs, self.out_specs = list(inputs), list(in_specs), list(out_specs)
        self.out_shapes, self.scratch_shapes, self.stream = list(out_shapes), list(scratch_shapes), stream


def _interleave(primary, secondary, ratio):
    live = {"p": primary, "s": secondary}

    def advance(key):
        if live[key] is not None:
            try:
                next(live[key])
            except StopIteration:
                live[key] = None

    while live["p"] is not None or live["s"] is not None:
        advance("p")
        for _ in range(ratio):
            advance("s")


def _run_streams(components, n_steps, name):
    n_in = [len(c.inputs) for c in components]
    n_out = [len(c.out_specs) for c in components]
    n_scr = [len(c.scratch_shapes) for c in components]

    def body(*refs):
        refs = list(refs)
        ins, outs, scrs = [], [], []
        pos = 0
        for group, counts in ((ins, n_in), (outs, n_out), (scrs, n_scr)):
            for k in counts:
                group.append(refs[pos:pos + k])
                pos += k
        step = pl.program_id(0)
        streams = [c.stream(i, o, s, step) for c, i, o, s in zip(components, ins, outs, scrs)]
        _interleave(streams[0], itertools.chain(*streams[1:]), STREAM_INTERLEAVE)

    outs = pl.pallas_call(
        body,
        grid=(n_steps,),
        in_specs=[s for c in components for s in c.in_specs],
        out_specs=[s for c in components for s in c.out_specs],
        out_shape=[s for c in components for s in c.out_shapes],
        scratch_shapes=[s for c in components for s in c.scratch_shapes],
        compiler_params=pltpu.CompilerParams(dimension_semantics=("arbitrary",), vmem_limit_bytes=VMEM_LIMIT_BYTES),
        name=name,
    )(*[a for c in components for a in c.inputs])
    split, pos = [], 0
    for k in n_out:
        split.append(outs[pos:pos + k])
        pos += k
    return split


def _in_proj_stream(in_refs, out_refs, scratch_refs, step):
    del step
    x_ref, pos_ref, ln_ref, w_ref, gain_ref, freq_ref, seg_ref = in_refs
    qkv0_ref, qkv1_ref, qkv2_ref, qkvb_ref, gate_ref = out_refs
    slab1_ref, slab2_ref = scratch_refs
    tm = x_ref.shape[0]
    x = x_ref[...]
    ms = jnp.mean(x * x, axis=-1, keepdims=True)
    h = (x * lax.rsqrt(ms + EPS) * ln_ref[...]).astype(_BF16)

    ang = freq_ref[...] * pos_ref[...].astype(_F32)
    packed = jnp.concatenate([jnp.cos(ang), jnp.sin(ang), jnp.ones_like(ang),
                              jnp.zeros((LANES - 3 * ROPE_HALF, tm), _F32)], axis=0).T
    lane = lax.broadcasted_iota(jnp.int32, (tm, LANES), 1) % HEAD_DIM
    rotary = lane < ROPE_DIM
    cos_t = jnp.take_along_axis(packed, jnp.where(rotary, lane % ROPE_HALF, 2 * ROPE_HALF), axis=1,
                                mode="promise_in_bounds")
    sin_t = jnp.take_along_axis(packed, jnp.where(rotary, ROPE_HALF + lane % ROPE_HALF, 3 * ROPE_HALF), axis=1,
                                mode="promise_in_bounds")
    first_half = lane < ROPE_HALF
    sin_from_hi = jnp.where(first_half, -sin_t, 0.0)
    sin_from_lo = jnp.where(first_half, 0.0, sin_t)
    seg = seg_ref[...]

    def norm_rope(y, gain):
        ss = jnp.dot((y * y).astype(_BF16), seg, preferred_element_type=_F32)
        yn = y * lax.rsqrt(ss * (1.0 / HEAD_DIM) + EPS) * gain
        up = pltpu.roll(yn, LANES - ROPE_HALF, 1)
        dn = pltpu.roll(yn, ROPE_HALF, 1)
        return yn * cos_t + up * sin_from_hi + dn * sin_from_lo

    def project(w_col):
        return jnp.dot(h, w_ref[:, w_col:w_col + CHUNK], preferred_element_type=_F32)

    tasks = []
    group_out = (qkv0_ref, slab1_ref, slab2_ref)

    def dilated_store(g, slab):
        def store(yh):
            if g == 0:
                qkv0_ref[:, slab * LANES:(slab + 1) * LANES] = yh.astype(_BF16)
            else:
                group_out[g][slab] = yh
        return store

    def row_store(out_ref, col):
        def store(yh):
            out_ref[:, col:col + LANES] = yh.astype(_BF16)
        return store

    def gate_store(col):
        def store(yh):
            gate_ref[:, col:col + LANES] = (0.5 * jnp.tanh(0.5 * yh) + 0.5).astype(_BF16)
        return store

    n_halves = CHUNK // LANES
    for part in range(3):
        for g in range(N_DIL_GROUPS):
            tasks.append((part * DIL_WIDTH + g * GROUP_WIDTH,
                          [(part if part < 2 else None, dilated_store(g, part * n_halves + i))
                           for i in range(n_halves)]))
    for c in range(QKV_B_WIDTH // CHUNK):
        cols = [c * CHUNK + i * LANES for i in range(n_halves)]
        tasks.append((QKV_A_WIDTH + c * CHUNK,
                      [(2 if col < SWA_Q_WIDTH else (3 if col < SWA_Q_WIDTH + SWA_KV_WIDTH else None),
                        row_store(qkvb_ref, col)) for col in cols]))
    for c in range(GATE_WIDTH // CHUNK):
        tasks.append((QKV_A_WIDTH + QKV_B_WIDTH + c * CHUNK,
                      [(None, gate_store(c * CHUNK + i * LANES)) for i in range(n_halves)]))

    def epilogue(y, stores):
        for i, (gain_idx, store) in enumerate(stores):
            yh = y[:, i * LANES:(i + 1) * LANES]
            store(yh if gain_idx is None else norm_rope(yh, gain_ref[gain_idx]))

    in_flight = []
    for w_col, stores in tasks:
        in_flight.append((project(w_col), stores))
        if len(in_flight) > PROJ_LOOKAHEAD:
            epilogue(*in_flight.pop(0))
        yield
    for y, stores in in_flight:
        epilogue(y, stores)

    for g, out_ref, slab_ref in ((1, qkv1_ref, slab1_ref), (2, qkv2_ref, slab2_ref)):
        d = DIL_PAIRS[g][1]
        for r in range(d):
            for slab in range(GROUP_QKV_WIDTH // LANES):
                rows = slab_ref[slab, pl.ds(r, tm // d, stride=d), :]
                out_ref[r, :, slab * LANES:(slab + 1) * LANES] = rows.astype(_BF16)
    yield


def _in_proj_component(x2, pos_rows, ln1, w_in, gains, freq, seg, batch, seq):
    tm = IN_PROJ_TILE
    n = batch * seq
    tiles_per_seq = seq // tm
    out_row = lambda width: pl.BlockSpec((tm, width), lambda i: (i, 0))

    def deinterleaved(d):
        return pl.BlockSpec((None, d, tm // d, GROUP_QKV_WIDTH),
                            lambda i: (i // tiles_per_seq, 0, i % tiles_per_seq, 0))

    d1, d2 = DIL_PAIRS[1][1], DIL_PAIRS[2][1]
    n_slabs = GROUP_QKV_WIDTH // LANES
    return _Component(
        inputs=[x2, pos_rows, ln1, w_in, gains, freq, seg],
        in_specs=[out_row(D_MODEL), pl.BlockSpec((None, 1, tm), lambda i: (i, 0, 0)),
                  _resident((1, D_MODEL)), _resident(w_in.shape), _resident(gains.shape), _resident(freq.shape),
                  _resident(seg.shape)],
        out_specs=[out_row(GROUP_QKV_WIDTH), deinterleaved(d1), deinterleaved(d2), out_row(QKV_B_WIDTH),
                   out_row(GATE_WIDTH)],
        out_shapes=[jax.ShapeDtypeStruct((n, GROUP_QKV_WIDTH), _BF16),
                    jax.ShapeDtypeStruct((batch, d1, seq // d1, GROUP_QKV_WIDTH), _BF16),
                    jax.ShapeDtypeStruct((batch, d2, seq // d2, GROUP_QKV_WIDTH), _BF16),
                    jax.ShapeDtypeStruct((n, QKV_B_WIDTH), _BF16),
                    jax.ShapeDtypeStruct((n, GATE_WIDTH), _BF16)],
        scratch_shapes=[pltpu.VMEM((n_slabs, tm, LANES), _F32), pltpu.VMEM((n_slabs, tm, LANES), _F32)],
        stream=_in_proj_stream)


_NT = (((1,), (1,)), ((), ()))
_STACK = GROUP_HEADS * BLOCK


def _stacked_scores(q, k_win):
    head_of_lane = lax.broadcasted_iota(jnp.int32, q.shape, 1) // HEAD_DIM
    zero = jnp.zeros_like(q)
    q_stack = jnp.concatenate([jnp.where(head_of_lane == hd, q, zero) for hd in range(GROUP_HEADS)], axis=0)
    return lax.dot_general(q_stack, k_win, _NT, preferred_element_type=_F32)


def _band_bias(max_dist, first_key_col):
    row = lax.broadcasted_iota(jnp.int32, (_STACK, 2 * BLOCK), 0) % BLOCK
    col = lax.broadcasted_iota(jnp.int32, (_STACK, 2 * BLOCK), 1)
    dist = row + BLOCK - col
    return jnp.where((dist >= 0) & (dist <= max_dist) & (col >= first_key_col), 0.0, NEG).astype(_F32)


def _head_rows(hd):
    return slice(hd * BLOCK, (hd + 1) * BLOCK)


def _softmax2(s, bias):
    s = s + bias
    m = jnp.max(s, axis=1, keepdims=True)
    p = jnp.exp2(s - m)
    return p, m, jnp.sum(p, axis=1, keepdims=True)


def _merge_heads(stack_col):
    low = lax.broadcasted_iota(jnp.int32, (BLOCK, LANES), 1) < HEAD_DIM
    return jnp.concatenate([jnp.where(low, stack_col(2 * pair, pair), stack_col(2 * pair + 1, pair))
                            for pair in range(GROUP_HEADS // 2)], axis=1)


def _attend(s, bias, v_win):
    p, m, denom = _softmax2(s, bias)
    p = p.astype(_BF16)
    pair_rows = 2 * BLOCK
    pv = [jnp.dot(p[pair * pair_rows:(pair + 1) * pair_rows], v_win[:, pair * LANES:(pair + 1) * LANES],
                  preferred_element_type=_F32) for pair in range(GROUP_HEADS // 2)]
    spread = lambda col: _merge_heads(lambda hd, pair: jnp.broadcast_to(col[_head_rows(hd)], (BLOCK, LANES)))
    denom = spread(denom)
    o = _merge_heads(lambda hd, pair: pv[pair][(hd % 2) * BLOCK:(hd % 2 + 1) * BLOCK]) / denom
    return o, spread(m) + jnp.log2(denom)


def _unit_stream(units, scores_fn, finish_fn):
    in_flight = []
    for unit in units:
        in_flight.append((unit, scores_fn(unit)))
        yield
        if len(in_flight) > ATTN_LOOKAHEAD:
            finish_fn(*in_flight.pop(0))
            yield
    for unit, s in in_flight:
        finish_fn(unit, s)
        yield


def _window(qb, prev_ref, cur_ref, seq=None):
    idx = (lambda rows: (rows, slice(None))) if seq is None else (lambda rows: (seq, rows, slice(None)))
    rows = slice(qb * BLOCK, (qb + 1) * BLOCK)
    if qb == 0:
        prev = prev_ref[idx(slice(0, BLOCK))]
    else:
        prev = cur_ref[idx(slice((qb - 1) * BLOCK, qb * BLOCK))]
    return jnp.concatenate([prev, cur_ref[idx(rows)]], axis=0)


def _dilated_stream(in_refs, out_refs, scratch_refs, step, *, max_dist, tiles_per_seq):
    del scratch_refs
    q_ref, kprev_ref, k_ref, vprev_ref, v_ref = in_refs
    o_ref, lse_ref = out_refs
    n_seqs, tile, _ = q_ref.shape
    first_tile = (step % tiles_per_seq) == 0
    units = [(seq, qb) for seq in range(n_seqs) for qb in range(tile // BLOCK)]
    bias_first = _band_bias(max_dist, jnp.where(first_tile, BLOCK, 0))
    bias_inner = _band_bias(max_dist, 0)

    def scores(unit):
        seq, qb = unit
        return _stacked_scores(q_ref[seq, qb * BLOCK:(qb + 1) * BLOCK, :], _window(qb, kprev_ref, k_ref, seq))

    def finish(unit, s):
        seq, qb = unit
        o, lse = _attend(s, bias_first if qb == 0 else bias_inner, _window(qb, vprev_ref, v_ref, seq))
        o_ref[seq, qb * BLOCK:(qb + 1) * BLOCK, :] = o.astype(o_ref.dtype)
        lse_ref[seq, qb * BLOCK:(qb + 1) * BLOCK, :] = lse

    return _unit_stream(units, scores, finish)


def _dilated_component(qkv, window, dilation):
    n_seqs, length, _ = qkv.shape
    tile = min(TOKEN_TILE, length)
    seqs = TOKEN_TILE // tile
    n_blocks = tile // BLOCK
    tiles_per_seq = length // tile

    def cur(col):
        return pl.BlockSpec((seqs, tile, GROUP_WIDTH), lambda i: (i // tiles_per_seq, i % tiles_per_seq, col))

    def prev(col):
        return pl.BlockSpec((seqs, BLOCK, GROUP_WIDTH),
                            lambda i: (i // tiles_per_seq, jnp.maximum((i % tiles_per_seq) * n_blocks - 1, 0), col))

    out_spec = pl.BlockSpec((seqs, tile, GROUP_WIDTH), lambda i: (i // tiles_per_seq, i % tiles_per_seq, 0))
    return _Component(
        inputs=[qkv] * 5,
        in_specs=[cur(0), prev(1), cur(1), prev(2), cur(2)],
        out_specs=[out_spec, out_spec],
        out_shapes=[jax.ShapeDtypeStruct((n_seqs, length, GROUP_WIDTH), _BF16),
                    jax.ShapeDtypeStruct((n_seqs, length, GROUP_WIDTH), _F32)],
        scratch_shapes=[],
        stream=functools.partial(_dilated_stream, max_dist=window // dilation, tiles_per_seq=tiles_per_seq))


def _swa_stream(in_refs, out_refs, scratch_refs, step, *, tiles_per_seq):
    sink_ref, q_ref, kprev_ref, k_ref, vprev_ref, v_ref = in_refs
    (o_ref,) = out_refs
    kdup_ref, vdup_ref = scratch_refs
    tile = q_ref.shape[0]
    first_tile = (step % tiles_per_seq) == 0
    n_kv_heads = SWA_KV_WIDTH // HEAD_DIM
    units = [(qb, kv) for qb in range(tile // BLOCK) for kv in range(n_kv_heads)]

    def duplicate(dst_ref, row0, t):
        low = lax.broadcasted_iota(jnp.int32, t.shape, 1) < HEAD_DIM
        t32 = t.astype(_F32)
        swapped = pltpu.roll(t32, HEAD_DIM, 1)
        dst_ref[0, row0:row0 + t.shape[0], :] = jnp.where(low, t32, swapped).astype(_BF16)
        dst_ref[1, row0:row0 + t.shape[0], :] = jnp.where(low, swapped, t32).astype(_BF16)

    for dst_ref, prev_ref, cur_ref in ((kdup_ref, kprev_ref, k_ref), (vdup_ref, vprev_ref, v_ref)):
        duplicate(dst_ref, 0, prev_ref[...])
        duplicate(dst_ref, BLOCK, cur_ref[...])

    low = lax.broadcasted_iota(jnp.int32, (BLOCK, LANES), 1) < HEAD_DIM

    def scores(unit):
        qb, kv = unit
        rows = slice(qb * BLOCK, (qb + 1) * BLOCK)
        parts = []
        for pair in range(GROUP_HEADS // 2):
            lane0 = kv * GROUP_WIDTH + pair * LANES
            q_pair = q_ref[rows, lane0:lane0 + LANES]
            zero = jnp.zeros_like(q_pair)
            parts += [jnp.where(low, q_pair, zero), jnp.where(low, zero, q_pair)]
        q_stack = jnp.concatenate(parts, axis=0)
        k_win = kdup_ref[kv, qb * BLOCK:(qb + 2) * BLOCK, :]
        return lax.dot_general(q_stack, k_win, _NT, preferred_element_type=_F32)

    sink_col = lax.broadcasted_iota(jnp.int32, (_STACK, 2 * BLOCK), 1) == 0
    bias_first = jnp.where(sink_col, 0.0, _band_bias(SWA_WINDOW - 1, jnp.where(first_tile, BLOCK, 0)))
    bias_inner = jnp.where(sink_col, 0.0, _band_bias(SWA_WINDOW - 1, 0))
    sink_lane = lax.broadcasted_iota(jnp.int32, (BLOCK, LANES), 1) == 0
    sink_lane_stack = lax.broadcasted_iota(jnp.int32, (_STACK, LANES), 1) == 0

    def finish(unit, s):
        qb, kv = unit
        rows = slice(qb * BLOCK, (qb + 1) * BLOCK)
        s_left = jnp.concatenate([jnp.where(sink_lane, sink_ref[kv * GROUP_HEADS + hd], s[_head_rows(hd), :LANES])
                                  for hd in range(GROUP_HEADS)], axis=0)
        s = jnp.concatenate([s_left, s[:, LANES:]], axis=1)
        p, _, denom = _softmax2(s, bias_first if qb == 0 else bias_inner)
        p = jnp.concatenate([jnp.where(sink_lane_stack, 0.0, p[:, :LANES]), p[:, LANES:]], axis=1).astype(_BF16)
        pv = jnp.dot(p, vdup_ref[kv, qb * BLOCK:(qb + 2) * BLOCK, :], preferred_element_type=_F32)
        for pair in range(GROUP_HEADS // 2):
            lane0 = kv * GROUP_WIDTH + pair * LANES
            lo, hi = _head_rows(2 * pair), _head_rows(2 * pair + 1)
            denom_pair = jnp.where(low, jnp.broadcast_to(denom[lo], (BLOCK, LANES)),
                                   jnp.broadcast_to(denom[hi], (BLOCK, LANES)))
            o_pair = jnp.where(low, pv[lo], pv[hi]) / denom_pair
            o_ref[rows, lane0:lane0 + LANES] = o_pair.astype(o_ref.dtype)

    return _unit_stream(units, scores, finish)


def _swa_component(qkvb3, sinks):
    b, s, _ = qkvb3.shape
    tile = TOKEN_TILE
    n_blocks = tile // BLOCK
    tiles_per_seq = s // tile
    k_col = SWA_Q_WIDTH // LANES
    v_col = k_col + SWA_KV_WIDTH // LANES

    def cur(col):
        return pl.BlockSpec((None, tile, LANES), lambda i: (i // tiles_per_seq, i % tiles_per_seq, col))

    def prev(col):
        return pl.BlockSpec((None, BLOCK, LANES),
                            lambda i: (i // tiles_per_seq, jnp.maximum((i % tiles_per_seq) * n_blocks - 1, 0), col))

    q_spec = pl.BlockSpec((None, tile, SWA_Q_WIDTH), lambda i: (i // tiles_per_seq, i % tiles_per_seq, 0))
    return _Component(
        inputs=[sinks] + [qkvb3] * 5,
        in_specs=[pl.BlockSpec(memory_space=pltpu.SMEM), q_spec, prev(k_col), cur(k_col), prev(v_col), cur(v_col)],
        out_specs=[q_spec],
        out_shapes=[jax.ShapeDtypeStruct((b, s, SWA_Q_WIDTH), _BF16)],
        scratch_shapes=[pltpu.VMEM((SWA_KV_WIDTH // HEAD_DIM, BLOCK + tile, LANES), _BF16)] * 2,
        stream=functools.partial(_swa_stream, tiles_per_seq=tiles_per_seq))


def _attention_components(proj, batch, seq, sinks2):
    qkv0, qkv1, qkv2, qkvb, _ = proj
    comps = []
    for (window, d), qkv in zip(DIL_PAIRS, (qkv0, qkv1, qkv2)):
        comps.append(_dilated_component(qkv.reshape(batch * d, seq // d, GROUP_QKV_WIDTH), window, d))
    comps.append(_swa_component(qkvb.reshape(batch, seq, QKV_B_WIDTH), sinks2))
    return comps


def _attention_outputs(outs, batch, seq):
    attn_a = []
    for (_, d), (o, lse) in zip(DIL_PAIRS, outs[:3]):
        shape = (batch * seq, GROUP_WIDTH) if d == 1 else (batch, d, seq // d, GROUP_WIDTH)
        attn_a.append((o.reshape(shape), lse.reshape(shape)))
    return attn_a, outs[3][0].reshape(batch * seq, SWA_Q_WIDTH)


def _out_mlp_stream(in_refs, out_refs, scratch_refs, step):
    del step
    (x_ref, o0_ref, l0_ref, o1_ref, l1_ref, o2_ref, l2_ref, ob_ref, gate_ref,
     wa_ref, wb_ref, wo_ref, ln_ref, wup_ref, wdn_ref) = in_refs
    (out_ref,) = out_refs
    (slab_ref,) = scratch_refs
    tm = x_ref.shape[0]
    n_slabs = GROUP_WIDTH // LANES

    def interleaved(src_ref, base):
        d = src_ref.shape[0]
        for r in range(d):
            for slab in range(n_slabs):
                slab_ref[base + slab, pl.ds(r, tm // d, stride=d), :] = (
                    src_ref[r, :, slab * LANES:(slab + 1) * LANES].astype(_F32))
        return jnp.concatenate([slab_ref[base + slab] for slab in range(n_slabs)], axis=1)

    o0, l0 = o0_ref[...].astype(_F32), l0_ref[...]
    o1, l1 = interleaved(o1_ref, 0), interleaved(l1_ref, n_slabs)
    o2, l2 = interleaved(o2_ref, 2 * n_slabs), interleaved(l2_ref, 3 * n_slabs)
    m = jnp.maximum(jnp.maximum(l0, l1), l2)
    e0, e1, e2 = jnp.exp2(l0 - m), jnp.exp2(l1 - m), jnp.exp2(l2 - m)
    oa = ((e0 * o0 + e1 * o1 + e2 * o2) / (e0 + e1 + e2)).astype(_BF16)

    ya = jnp.dot(oa, wa_ref[...], preferred_element_type=_F32)
    yb = jnp.dot(ob_ref[...], wb_ref[...], preferred_element_type=_F32)
    yield
    gate_a = gate_ref[:, :D_MODEL].astype(_F32)
    gate_b = gate_ref[:, D_MODEL:].astype(_F32)
    mix = (gate_a * ya + gate_b * yb).astype(_BF16)
    x1 = x_ref[...] + jnp.dot(mix, wo_ref[...], preferred_element_type=_F32)
    yield

    ms = jnp.mean(x1 * x1, axis=-1, keepdims=True)
    h2 = (x1 * lax.rsqrt(ms + EPS) * ln_ref[...]).astype(_BF16)
    ff_chunk = 512
    acc = x1
    for c in range(D_FF // ff_chunk):
        u = jnp.dot(h2, wup_ref[:, c * ff_chunk:(c + 1) * ff_chunk], preferred_element_type=_F32)
        yield
        a = jnp.square(jnp.maximum(u, 0.0)).astype(_BF16)
        acc = acc + jnp.dot(a, wdn_ref[c * ff_chunk:(c + 1) * ff_chunk, :], preferred_element_type=_F32)
        yield
    out_ref[...] = acc


def _out_mlp_component(x2, attn_a, ob, gates, wa, wb, wo, ln2, wup, wdn, seq):
    tm = TOKEN_TILE
    tiles_per_seq = seq // tm
    row = lambda width: pl.BlockSpec((tm, width), lambda i: (i, 0))
    full_row = row(D_MODEL)

    def deinterleaved(d):
        return pl.BlockSpec((None, d, tm // d, GROUP_WIDTH), lambda i: (i // tiles_per_seq, 0, i % tiles_per_seq, 0))

    group_specs = [row(GROUP_WIDTH)] * 2
    for _, d in DIL_PAIRS[1:]:
        group_specs += [deinterleaved(d)] * 2
    return _Component(
        inputs=[x2] + [a for pair in attn_a for a in pair] + [ob, gates, wa, wb, wo, ln2, wup, wdn],
        in_specs=[full_row] + group_specs + [row(SWA_Q_WIDTH), row(GATE_WIDTH),
                  _resident(wa.shape), _resident(wb.shape), _resident(wo.shape), _resident((1, D_MODEL)),
                  _resident(wup.shape), _resident(wdn.shape)],
        out_specs=[full_row],
        out_shapes=[jax.ShapeDtypeStruct(x2.shape, _F32)],
        scratch_shapes=[pltpu.VMEM((4 * GROUP_WIDTH // LANES, tm, LANES), _F32)],
        stream=_out_mlp_stream)


def _rope_inv_freq():
    return (ROPE_THETA ** (-jnp.arange(0, ROPE_DIM, 2, dtype=_F32) / ROPE_DIM))[:, None]


def kernel(x, positions, ln1_g, w_in, q_norm_a, k_norm_a, q_norm_b, k_norm_b, sinks,
           w_branch_a, w_branch_b, w_out, ln2_g, w_up, w_down):
    b, s, d_model = x.shape
    assert d_model == D_MODEL and ln1_g.shape[0] == 1 and s % (16 * BLOCK) == 0
    assert s % TOKEN_TILE == 0 and s % IN_PROJ_TILE == 0
    n_tiles = b * s // TOKEN_TILE
    x2 = x.reshape(b * s, d_model)
    log2e = np.float32(np.log2(np.e))
    scale = np.float32(log2e / np.sqrt(HEAD_DIM))
    gains = jnp.stack([jnp.tile(q_norm_a[0] * scale, 2), jnp.tile(k_norm_a[0], 2),
                       jnp.tile(q_norm_b[0] * scale, 2), jnp.tile(k_norm_b[0], 2)])[:, None, :]
    head = np.arange(LANES) // HEAD_DIM
    seg = jnp.asarray(head[:, None] == head[None, :], _BF16)
    pos_rows = positions.reshape(b * s // IN_PROJ_TILE, 1, IN_PROJ_TILE)
    sinks2 = sinks[0] * log2e

    (proj,) = _run_streams(
        [_in_proj_component(x2, pos_rows, ln1_g, w_in[0].astype(_BF16), gains, _rope_inv_freq(), seg, b, s)],
        b * s // IN_PROJ_TILE, "in_proj")
    attn_a, ob = _attention_outputs(_run_streams(_attention_components(proj, b, s, sinks2), n_tiles, "attention"),
                                    b, s)
    ((out,),) = _run_streams(
        [_out_mlp_component(x2, attn_a, ob, proj[4],
                            w_branch_a[0].astype(_BF16), w_branch_b[0].astype(_BF16), w_out[0].astype(_BF16),
                            ln2_g, w_up[0].astype(_BF16), w_down[0].astype(_BF16), s)],
        n_tiles, "out_mlp")
    return out.reshape(b, s, d_model)
```

```python
import functools
import itertools

import numpy as np
import jax
import jax.numpy as jnp
from jax import lax
from jax.experimental import pallas as pl
from jax.experimental.pallas import tpu as pltpu

D_MODEL = 1024
HEAD_DIM = 64
DIL_PAIRS = ((128, 1), (512, 4), (2048, 16))
GROUP_HEADS = 4
GROUP_WIDTH = GROUP_HEADS * HEAD_DIM
DIL_WIDTH = 768
N_DIL_GROUPS = DIL_WIDTH // GROUP_WIDTH
GROUP_QKV_WIDTH = 3 * GROUP_WIDTH
SWA_WINDOW = 128
SWA_Q_WIDTH = 512
SWA_KV_WIDTH = 128
QKV_A_WIDTH = 3 * DIL_WIDTH
QKV_B_WIDTH = SWA_Q_WIDTH + 2 * SWA_KV_WIDTH
GATE_WIDTH = 2 * D_MODEL
D_FF = 4 * D_MODEL
ROPE_THETA = 500000.0
ROPE_DIM = HEAD_DIM // 4
ROPE_HALF = ROPE_DIM // 2
BLOCK = 128
EPS = 1e-6
NEG = -1e30

LANES = 128
BF16_SUBLANES = 16
CHUNK = 256
TOKEN_TILE = 512
ATTN_TILE = 1024
IN_PROJ_TILE = 1024
PROJ_LOOKAHEAD = 2
ATTN_LOOKAHEAD = 1
STREAM_INTERLEAVE = 2
VMEM_LIMIT_BYTES = 56 * 1024 * 1024

_BF16 = jnp.bfloat16
_F32 = jnp.float32


def _resident(shape):
    return pl.BlockSpec(shape, lambda *_: (0,) * len(shape), pipeline_mode=pl.Buffered(1))


class _Component:
    def __init__(self, inputs, in_specs, out_specs, out_shapes, scratch_shapes, stream):
        self.inputs, self.in_specs, self.out_specs = list(inputs), list(in_specs), list(out_specs)
        self.out_shapes, self.scratch_shapes, self.stream = list(out_shapes), list(scratch_shapes), stream


def _interleave(primary, secondary, ratio):
    live = {"p": primary, "s": secondary}

    def advance(key):
        if live[key] is not None:
            try:
                next(live[key])
            except StopIteration:
                live[key] = None

    while live["p"] is not None or live["s"] is not None:
        advance("p")
        for _ in range(ratio):
            advance("s")


def _run_streams(components, n_steps, name):
    n_in = [len(c.inputs) for c in components]
    n_out = [len(c.out_specs) for c in components]
    n_scr = [len(c.scratch_shapes) for c in components]

    def body(*refs):
        refs = list(refs)
        ins, outs, scrs = [], [], []
        pos = 0
        for group, counts in ((ins, n_in), (outs, n_out), (scrs, n_scr)):
            for k in counts:
                group.append(refs[pos:pos + k])
                pos += k
        step = pl.program_id(0)
        streams = [c.stream(i, o, s, step) for c, i, o, s in zip(components, ins, outs, scrs)]
        _interleave(streams[0], itertools.chain(*streams[1:]), STREAM_INTERLEAVE)

    outs = pl.pallas_call(
        body,
        grid=(n_steps,),
        in_specs=[s for c in components for s in c.in_specs],
        out_specs=[s for c in components for s in c.out_specs],
        out_shape=[s for c in components for s in c.out_shapes],
        scratch_shapes=[s for c in components for s in c.scratch_shapes],
        compiler_params=pltpu.CompilerParams(dimension_semantics=("arbitrary",), vmem_limit_bytes=VMEM_LIMIT_BYTES),
        name=name,
    )(*[a for c in components for a in c.inputs])
    split, pos = [], 0
    for k in n_out:
        split.append(outs[pos:pos + k])
        pos += k
    return split


def _in_proj_stream(in_refs, out_refs, scratch_refs, step):
    del step
    x_ref, pos_ref, ln_ref, w_ref, gain_ref, freq_ref, seg_ref = in_refs
    qkv0_ref, qkv1_ref, qkv2_ref, qkvb_ref, gate_ref = out_refs
    slab1_ref, slab2_ref = scratch_refs
    tm = x_ref.shape[0]
    x = x_ref[...]
    ms = jnp.mean(x * x, axis=-1, keepdims=True)
    h = (x * lax.rsqrt(ms + EPS) * ln_ref[...]).astype(_BF16)

    ang = freq_ref[...] * pos_ref[...].astype(_F32)
    packed = jnp.concatenate([jnp.cos(ang), jnp.sin(ang), jnp.ones_like(ang),
                              jnp.zeros((LANES - 3 * ROPE_HALF, tm), _F32)], axis=0).T
    lane = lax.broadcasted_iota(jnp.int32, (tm, LANES), 1) % HEAD_DIM
    rotary = lane < ROPE_DIM
    cos_t = jnp.take_along_axis(packed, jnp.where(rotary, lane % ROPE_HALF, 2 * ROPE_HALF), axis=1,
                                mode="promise_in_bounds")
    sin_t = jnp.take_along_axis(packed, jnp.where(rotary, ROPE_HALF + lane % ROPE_HALF, 3 * ROPE_HALF), axis=1,
                                mode="promise_in_bounds")
    first_half = lane < ROPE_HALF
    sin_from_hi = jnp.where(first_half, -sin_t, 0.0)
    sin_from_lo = jnp.where(first_half, 0.0, sin_t)
    seg = seg_ref[...]

    def norm_rope(y, gain):
        ss = jnp.dot((y * y).astype(_BF16), seg, preferred_element_type=_F32)
        yn = y * lax.rsqrt(ss * (1.0 / HEAD_DIM) + EPS) * gain
        up = pltpu.roll(yn, LANES - ROPE_HALF, 1)
        dn = pltpu.roll(yn, ROPE_HALF, 1)
        return yn * cos_t + up * sin_from_hi + dn * sin_from_lo

    def project(w_col):
        return jnp.dot(h, w_ref[:, w_col:w_col + CHUNK], preferred_element_type=_F32)

    tasks = []
    group_out = (qkv0_ref, slab1_ref, slab2_ref)

    def dilated_store(g, slab):
        def store(yh):
            if g == 0:
                qkv0_ref[:, slab * LANES:(slab + 1) * LANES] = yh.astype(_BF16)
            else:
                group_out[g][slab] = yh
        return store

    def row_store(out_ref, col):
        def store(yh):
            out_ref[:, col:col + LANES] = yh.astype(_BF16)
        return store

    def gate_store(col):
        def store(yh):
            gate_ref[:, col:col + LANES] = (0.5 * jnp.tanh(0.5 * yh) + 0.5).astype(_BF16)
        return store

    n_halves = CHUNK // LANES
    for part in range(3):
        for g in range(N_DIL_GROUPS):
            tasks.append((part * DIL_WIDTH + g * GROUP_WIDTH,
                          [(part if part < 2 else None, dilated_store(g, part * n_halves + i))
                           for i in range(n_halves)]))
    for c in range(QKV_B_WIDTH // CHUNK):
        cols = [c * CHUNK + i * LANES for i in range(n_halves)]
        tasks.append((QKV_A_WIDTH + c * CHUNK,
                      [(2 if col < SWA_Q_WIDTH else (3 if col < SWA_Q_WIDTH + SWA_KV_WIDTH else None),
                        row_store(qkvb_ref, col)) for col in cols]))
    for c in range(GATE_WIDTH // CHUNK):
        tasks.append((QKV_A_WIDTH + QKV_B_WIDTH + c * CHUNK,
                      [(None, gate_store(c * CHUNK + i * LANES)) for i in range(n_halves)]))

    def epilogue(y, stores):
        for i, (gain_idx, store) in enumerate(stores):
            yh = y[:, i * LANES:(i + 1) * LANES]
            store(yh if gain_idx is None else norm_rope(yh, gain_ref[gain_idx]))

    in_flight = []
    for w_col, stores in tasks:
        in_flight.append((project(w_col), stores))
        if len(in_flight) > PROJ_LOOKAHEAD:
            epilogue(*in_flight.pop(0))
        yield
    for y, stores in in_flight:
        epilogue(y, stores)

    for g, out_ref, slab_ref in ((1, qkv1_ref, slab1_ref), (2, qkv2_ref, slab2_ref)):
        d = DIL_PAIRS[g][1]
        for r in range(d):
            for slab in range(GROUP_QKV_WIDTH // LANES):
                rows = slab_ref[slab, pl.ds(r, tm // d, stride=d), :]
                out_ref[r, :, slab * LANES:(slab + 1) * LANES] = rows.astype(_BF16)
    yield


def _in_proj_component(x2, pos_rows, ln1, w_in, gains, freq, seg, batch, seq):
    tm = IN_PROJ_TILE
    n = batch * seq
    tiles_per_seq = seq // tm
    out_row = lambda width: pl.BlockSpec((tm, width), lambda i: (i, 0))

    def deinterleaved(d):
        return pl.BlockSpec((None, d, tm // d, GROUP_QKV_WIDTH),
                            lambda i: (i // tiles_per_seq, 0, i % tiles_per_seq, 0))

    d1, d2 = DIL_PAIRS[1][1], DIL_PAIRS[2][1]
    n_slabs = GROUP_QKV_WIDTH // LANES
    return _Component(
        inputs=[x2, pos_rows, ln1, w_in, gains, freq, seg],
        in_specs=[out_row(D_MODEL), pl.BlockSpec((None, 1, tm), lambda i: (i, 0, 0)),
                  _resident((1, D_MODEL)), _resident(w_in.shape), _resident(gains.shape), _resident(freq.shape),
                  _resident(seg.shape)],
        out_specs=[out_row(GROUP_QKV_WIDTH), deinterleaved(d1), deinterleaved(d2), out_row(QKV_B_WIDTH),
                   out_row(GATE_WIDTH)],
        out_shapes=[jax.ShapeDtypeStruct((n, GROUP_QKV_WIDTH), _BF16),
                    jax.ShapeDtypeStruct((batch, d1, seq // d1, GROUP_QKV_WIDTH), _BF16),
                    jax.ShapeDtypeStruct((batch, d2, seq // d2, GROUP_QKV_WIDTH), _BF16),
                    jax.ShapeDtypeStruct((n, QKV_B_WIDTH), _BF16),
                    jax.ShapeDtypeStruct((n, GATE_WIDTH), _BF16)],
        scratch_shapes=[pltpu.VMEM((n_slabs, tm, LANES), _F32), pltpu.VMEM((n_slabs, tm, LANES), _F32)],
        stream=_in_proj_stream)


_NT = (((1,), (1,)), ((), ()))
_STACK = GROUP_HEADS * BLOCK


def _stacked_scores(q, k_win):
    head_of_lane = lax.broadcasted_iota(jnp.int32, q.shape, 1) // HEAD_DIM
    zero = jnp.zeros_like(q)
    q_stack = jnp.concatenate([jnp.where(head_of_lane == hd, q, zero) for hd in range(GROUP_HEADS)], axis=0)
    return lax.dot_general(q_stack, k_win, _NT, preferred_element_type=_F32)


def _band_bias(max_dist, first_key_col):
    row = lax.broadcasted_iota(jnp.int32, (_STACK, 2 * BLOCK), 0) % BLOCK
    col = lax.broadcasted_iota(jnp.int32, (_STACK, 2 * BLOCK), 1)
    dist = row + BLOCK - col
    return jnp.where((dist >= 0) & (dist <= max_dist) & (col >= first_key_col), 0.0, NEG).astype(_F32)


def _head_rows(hd):
    return slice(hd * BLOCK, (hd + 1) * BLOCK)


def _softmax2(s, bias):
    s = s + bias
    m = jnp.max(s, axis=1, keepdims=True)
    p = jnp.exp2(s - m)
    return p, m, jnp.sum(p, axis=1, keepdims=True)


def _merge_heads(stack_col):
    low = lax.broadcasted_iota(jnp.int32, (BLOCK, LANES), 1) < HEAD_DIM
    return jnp.concatenate([jnp.where(low, stack_col(2 * pair, pair), stack_col(2 * pair + 1, pair))
                            for pair in range(GROUP_HEADS // 2)], axis=1)


def _attend(s, bias, v_win):
    p, m, denom = _softmax2(s, bias)
    p = p.astype(_BF16)
    pair_rows = 2 * BLOCK
    pv = [jnp.dot(p[pair * pair_rows:(pair + 1) * pair_rows], v_win[:, pair * LANES:(pair + 1) * LANES],
                  preferred_element_type=_F32) for pair in range(GROUP_HEADS // 2)]
    spread = lambda col: _merge_heads(lambda hd, pair: jnp.broadcast_to(col[_head_rows(hd)], (BLOCK, LANES)))
    denom = spread(denom)
    o = _merge_heads(lambda hd, pair: pv[pair][(hd % 2) * BLOCK:(hd % 2 + 1) * BLOCK]) / denom
    return o, spread(m) + jnp.log2(denom)


def _unit_stream(units, scores_fn, finish_fn):
    in_flight = []
    for unit in units:
        in_flight.append((unit, scores_fn(unit)))
        yield
        if len(in_flight) > ATTN_LOOKAHEAD:
            finish_fn(*in_flight.pop(0))
            yield
    for unit, s in in_flight:
        finish_fn(unit, s)
        yield


def _window(qb, prev_ref, cur_ref, seq=None):
    idx = (lambda rows: (rows, slice(None))) if seq is None else (lambda rows: (seq, rows, slice(None)))
    rows = slice(qb * BLOCK, (qb + 1) * BLOCK)
    if qb == 0:
        prev = prev_ref[idx(slice(0, BLOCK))]
    else:
        prev = cur_ref[idx(slice((qb - 1) * BLOCK, qb * BLOCK))]
    return jnp.concatenate([prev, cur_ref[idx(rows)]], axis=0)


def _dilated_stream(in_refs, out_refs, scratch_refs, step, *, max_dist, tiles_per_seq):
    del scratch_refs
    q_ref, kprev_ref, k_ref, vprev_ref, v_ref = in_refs
    o_ref, lse_ref = out_refs
    n_seqs, tile, _ = q_ref.shape
    first_tile = (step % tiles_per_seq) == 0
    units = [(seq, qb) for seq in range(n_seqs) for qb in range(tile // BLOCK)]
    bias_first = _band_bias(max_dist, jnp.where(first_tile, BLOCK, 0))
    bias_inner = _band_bias(max_dist, 0)

    def scores(unit):
        seq, qb = unit
        return _stacked_scores(q_ref[seq, qb * BLOCK:(qb + 1) * BLOCK, :], _window(qb, kprev_ref, k_ref, seq))

    def finish(unit, s):
        seq, qb = unit
        o, lse = _attend(s, bias_first if qb == 0 else bias_inner, _window(qb, vprev_ref, v_ref, seq))
        o_ref[seq, qb * BLOCK:(qb + 1) * BLOCK, :] = o.astype(o_ref.dtype)
        lse_ref[seq, qb * BLOCK:(qb + 1) * BLOCK, :] = lse

    return _unit_stream(units, scores, finish)


def _dilated_component(qkv, window, dilation):
    n_seqs, length, _ = qkv.shape
    tile = min(ATTN_TILE, length)
    seqs = ATTN_TILE // tile
    n_blocks = tile // BLOCK
    tiles_per_seq = length // tile

    def cur(col):
        return pl.BlockSpec((seqs, tile, GROUP_WIDTH), lambda i: (i // tiles_per_seq, i % tiles_per_seq, col))

    def prev(col):
        return pl.BlockSpec((seqs, BLOCK, GROUP_WIDTH),
                            lambda i: (i // tiles_per_seq, jnp.maximum((i % tiles_per_seq) * n_blocks - 1, 0), col))

    out_spec = pl.BlockSpec((seqs, tile, GROUP_WIDTH), lambda i: (i // tiles_per_seq, i % tiles_per_seq, 0))
    return _Component(
        inputs=[qkv] * 5,
        in_specs=[cur(0), prev(1), cur(1), prev(2), cur(2)],
        out_specs=[out_spec, out_spec],
        out_shapes=[jax.ShapeDtypeStruct((n_seqs, length, GROUP_WIDTH), _BF16),
                    jax.ShapeDtypeStruct((n_seqs, length, GROUP_WIDTH), _F32)],
        scratch_shapes=[],
        stream=functools.partial(_dilated_stream, max_dist=window // dilation, tiles_per_seq=tiles_per_seq))


def _swa_stream(in_refs, out_refs, scratch_refs, step, *, tiles_per_seq):
    sink_ref, q_ref, kprev_ref, k_ref, vprev_ref, v_ref = in_refs
    (o_ref,) = out_refs
    kdup_ref, vdup_ref = scratch_refs
    tile = q_ref.shape[0]
    first_tile = (step % tiles_per_seq) == 0
    n_kv_heads = SWA_KV_WIDTH // HEAD_DIM
    units = [(qb, kv) for qb in range(tile // BLOCK) for kv in range(n_kv_heads)]

    def duplicate(dst_ref, row0, t):
        low = lax.broadcasted_iota(jnp.int32, t.shape, 1) < HEAD_DIM
        t32 = t.astype(_F32)
        swapped = pltpu.roll(t32, HEAD_DIM, 1)
        dst_ref[0, row0:row0 + t.shape[0], :] = jnp.where(low, t32, swapped).astype(_BF16)
        dst_ref[1, row0:row0 + t.shape[0], :] = jnp.where(low, swapped, t32).astype(_BF16)

    for dst_ref, prev_ref, cur_ref in ((kdup_ref, kprev_ref, k_ref), (vdup_ref, vprev_ref, v_ref)):
        duplicate(dst_ref, 0, prev_ref[...])
        duplicate(dst_ref, BLOCK, cur_ref[...])

    low = lax.broadcasted_iota(jnp.int32, (BLOCK, LANES), 1) < HEAD_DIM

    def scores(unit):
        qb, kv = unit
        rows = slice(qb * BLOCK, (qb + 1) * BLOCK)
        parts = []
        for pair in range(GROUP_HEADS // 2):
            lane0 = kv * GROUP_WIDTH + pair * LANES
            q_pair = q_ref[rows, lane0:lane0 + LANES]
            zero = jnp.zeros_like(q_pair)
            parts += [jnp.where(low, q_pair, zero), jnp.where(low, zero, q_pair)]
        q_stack = jnp.concatenate(parts, axis=0)
        k_win = kdup_ref[kv, qb * BLOCK:(qb + 2) * BLOCK, :]
        return lax.dot_general(q_stack, k_win, _NT, preferred_element_type=_F32)

    sink_col = lax.broadcasted_iota(jnp.int32, (_STACK, 2 * BLOCK), 1) == 0
    bias_first = jnp.where(sink_col, 0.0, _band_bias(SWA_WINDOW - 1, jnp.where(first_tile, BLOCK, 0)))
    bias_inner = jnp.where(sink_col, 0.0, _band_bias(SWA_WINDOW - 1, 0))
    sink_lane = lax.broadcasted_iota(jnp.int32, (BLOCK, LANES), 1) == 0
    sink_lane_stack = lax.broadcasted_iota(jnp.int32, (_STACK, LANES), 1) == 0

    def finish(unit, s):
        qb, kv = unit
        rows = slice(qb * BLOCK, (qb + 1) * BLOCK)
        s_left = jnp.concatenate([jnp.where(sink_lane, sink_ref[kv * GROUP_HEADS + hd], s[_head_rows(hd), :LANES])
                                  for hd in range(GROUP_HEADS)], axis=0)
        s = jnp.concatenate([s_left, s[:, LANES:]], axis=1)
        p, _, denom = _softmax2(s, bias_first if qb == 0 else bias_inner)
        p = jnp.concatenate([jnp.where(sink_lane_stack, 0.0, p[:, :LANES]), p[:, LANES:]], axis=1).astype(_BF16)
        pv = jnp.dot(p, vdup_ref[kv, qb * BLOCK:(qb + 2) * BLOCK, :], preferred_element_type=_F32)
        for pair in range(GROUP_HEADS // 2):
            lane0 = kv * GROUP_WIDTH + pair * LANES
            lo, hi = _head_rows(2 * pair), _head_rows(2 * pair + 1)
            denom_pair = jnp.where(low, jnp.broadcast_to(denom[lo], (BLOCK, LANES)),
                                   jnp.broadcast_to(denom[hi], (BLOCK, LANES)))
            o_pair = jnp.where(low, pv[lo], pv[hi]) / denom_pair
            o_ref[rows, lane0:lane0 + LANES] = o_pair.astype(o_ref.dtype)

    return _unit_stream(units, scores, finish)


def _swa_component(qkvb3, sinks):
    b, s, _ = qkvb3.shape
    tile = ATTN_TILE
    n_blocks = tile // BLOCK
    tiles_per_seq = s // tile
    k_col = SWA_Q_WIDTH // LANES
    v_col = k_col + SWA_KV_WIDTH // LANES

    def cur(col):
        return pl.BlockSpec((None, tile, LANES), lambda i: (i // tiles_per_seq, i % tiles_per_seq, col))

    def prev(col):
        return pl.BlockSpec((None, BLOCK, LANES),
                            lambda i: (i // tiles_per_seq, jnp.maximum((i % tiles_per_seq) * n_blocks - 1, 0), col))

    q_spec = pl.BlockSpec((None, tile, SWA_Q_WIDTH), lambda i: (i // tiles_per_seq, i % tiles_per_seq, 0))
    return _Component(
        inputs=[sinks] + [qkvb3] * 5,
        in_specs=[pl.BlockSpec(memory_space=pltpu.SMEM), q_spec, prev(k_col), cur(k_col), prev(v_col), cur(v_col)],
        out_specs=[q_spec],
        out_shapes=[jax.ShapeDtypeStruct((b, s, SWA_Q_WIDTH), _BF16)],
        scratch_shapes=[pltpu.VMEM((SWA_KV_WIDTH // HEAD_DIM, BLOCK + tile, LANES), _BF16)] * 2,
        stream=functools.partial(_swa_stream, tiles_per_seq=tiles_per_seq))


def _cast_stream(in_refs, out_refs, scratch_refs, step):
    del scratch_refs, step
    for src_ref, dst_ref in zip(in_refs, out_refs):
        dst_ref[...] = src_ref[...].astype(dst_ref.dtype)
    yield


def _cast_component(weights, n_steps):
    specs = []
    for w in weights:
        rows, cols = w.shape
        block_rows = max(BF16_SUBLANES, rows // n_steps)
        assert rows % block_rows == 0 and rows // block_rows <= n_steps
        last = rows // block_rows - 1
        specs.append(pl.BlockSpec((block_rows, cols), lambda i, last=last: (jnp.minimum(i, last), 0)))
    return _Component(inputs=weights, in_specs=specs, out_specs=specs,
                      out_shapes=[jax.ShapeDtypeStruct(w.shape, _BF16) for w in weights],
                      scratch_shapes=[], stream=_cast_stream)


def _attention_components(proj, batch, seq, sinks2):
    qkv0, qkv1, qkv2, qkvb, _ = proj
    comps = []
    for (window, d), qkv in zip(DIL_PAIRS, (qkv0, qkv1, qkv2)):
        comps.append(_dilated_component(qkv.reshape(batch * d, seq // d, GROUP_QKV_WIDTH), window, d))
    comps.append(_swa_component(qkvb.reshape(batch, seq, QKV_B_WIDTH), sinks2))
    return comps


def _attention_outputs(outs, batch, seq):
    attn_a = []
    for (_, d), (o, lse) in zip(DIL_PAIRS, outs[:3]):
        shape = (batch * seq, GROUP_WIDTH) if d == 1 else (batch, d, seq // d, GROUP_WIDTH)
        attn_a.append((o.reshape(shape), lse.reshape(shape)))
    return attn_a, outs[3][0].reshape(batch * seq, SWA_Q_WIDTH)


def _out_mlp_stream(in_refs, out_refs, scratch_refs, step):
    del step
    (x_ref, o0_ref, l0_ref, o1_ref, l1_ref, o2_ref, l2_ref, ob_ref, gate_ref,
     wa_ref, wb_ref, wo_ref, ln_ref, wup_ref, wdn_ref) = in_refs
    (out_ref,) = out_refs
    (slab_ref,) = scratch_refs
    tm = x_ref.shape[0]
    n_slabs = GROUP_WIDTH // LANES

    def interleaved(src_ref, base):
        d = src_ref.shape[0]
        for r in range(d):
            for slab in range(n_slabs):
                slab_ref[base + slab, pl.ds(r, tm // d, stride=d), :] = (
                    src_ref[r, :, slab * LANES:(slab + 1) * LANES].astype(_F32))
        return jnp.concatenate([slab_ref[base + slab] for slab in range(n_slabs)], axis=1)

    o0, l0 = o0_ref[...].astype(_F32), l0_ref[...]
    o1, l1 = interleaved(o1_ref, 0), interleaved(l1_ref, n_slabs)
    o2, l2 = interleaved(o2_ref, 2 * n_slabs), interleaved(l2_ref, 3 * n_slabs)
    m = jnp.maximum(jnp.maximum(l0, l1), l2)
    e0, e1, e2 = jnp.exp2(l0 - m), jnp.exp2(l1 - m), jnp.exp2(l2 - m)
    oa = ((e0 * o0 + e1 * o1 + e2 * o2) / (e0 + e1 + e2)).astype(_BF16)

    ya = jnp.dot(oa, wa_ref[...], preferred_element_type=_F32)
    yb = jnp.dot(ob_ref[...], wb_ref[...], preferred_element_type=_F32)
    yield
    gate_a = gate_ref[:, :D_MODEL].astype(_F32)
    gate_b = gate_ref[:, D_MODEL:].astype(_F32)
    mix = (gate_a * ya + gate_b * yb).astype(_BF16)
    x1 = x_ref[...] + jnp.dot(mix, wo_ref[...], preferred_element_type=_F32)
    yield

    ms = jnp.mean(x1 * x1, axis=-1, keepdims=True)
    h2 = (x1 * lax.rsqrt(ms + EPS) * ln_ref[...]).astype(_BF16)
    ff_chunk = 512
    acc = x1
    for c in range(D_FF // ff_chunk):
        u = jnp.dot(h2, wup_ref[:, c * ff_chunk:(c + 1) * ff_chunk], preferred_element_type=_F32)
        yield
        a = jnp.square(jnp.maximum(u, 0.0)).astype(_BF16)
        acc = acc + jnp.dot(a, wdn_ref[c * ff_chunk:(c + 1) * ff_chunk, :], preferred_element_type=_F32)
        yield
    out_ref[...] = acc


def _out_mlp_component(x2, attn_a, ob, gates, wa, wb, wo, ln2, wup, wdn, seq):
    tm = TOKEN_TILE
    tiles_per_seq = seq // tm
    row = lambda width: pl.BlockSpec((tm, width), lambda i: (i, 0))
    full_row = row(D_MODEL)

    def deinterleaved(d):
        return pl.BlockSpec((None, d, tm // d, GROUP_WIDTH), lambda i: (i // tiles_per_seq, 0, i % tiles_per_seq, 0))

    group_specs = [row(GROUP_WIDTH)] * 2
    for _, d in DIL_PAIRS[1:]:
        group_specs += [deinterleaved(d)] * 2
    return _Component(
        inputs=[x2] + [a for pair in attn_a for a in pair] + [ob, gates, wa, wb, wo, ln2, wup, wdn],
        in_specs=[full_row] + group_specs + [row(SWA_Q_WIDTH), row(GATE_WIDTH),
                  _resident(wa.shape), _resident(wb.shape), _resident(wo.shape), _resident((1, D_MODEL)),
                  _resident(wup.shape), _resident(wdn.shape)],
        out_specs=[full_row],
        out_shapes=[jax.ShapeDtypeStruct(x2.shape, _F32)],
        scratch_shapes=[pltpu.VMEM((4 * GROUP_WIDTH // LANES, tm, LANES), _F32)],
        stream=_out_mlp_stream)


def _rope_inv_freq():
    return (ROPE_THETA ** (-jnp.arange(0, ROPE_DIM, 2, dtype=_F32) / ROPE_DIM))[:, None]


def kernel(x, positions, ln1_g, w_in, q_norm_a, k_norm_a, q_norm_b, k_norm_b, sinks,
           w_branch_a, w_branch_b, w_out, ln2_g, w_up, w_down):
    b, s, d_model = x.shape
    assert d_model == D_MODEL and ln1_g.shape[0] == 1 and s % (16 * BLOCK) == 0
    assert s % TOKEN_TILE == 0 and s % IN_PROJ_TILE == 0 and s % ATTN_TILE == 0
    n_tiles = b * s // TOKEN_TILE
    attn_steps = b * s // ATTN_TILE
    x2 = x.reshape(b * s, d_model)
    log2e = np.float32(np.log2(np.e))
    scale = np.float32(log2e / np.sqrt(HEAD_DIM))
    gains = jnp.stack([jnp.tile(q_norm_a[0] * scale, 2), jnp.tile(k_norm_a[0], 2),
                       jnp.tile(q_norm_b[0] * scale, 2), jnp.tile(k_norm_b[0], 2)])[:, None, :]
    head = np.arange(LANES) // HEAD_DIM
    seg = jnp.asarray(head[:, None] == head[None, :], _BF16)
    pos_rows = positions.reshape(b * s // IN_PROJ_TILE, 1, IN_PROJ_TILE)
    sinks2 = sinks[0] * log2e

    (proj,) = _run_streams(
        [_in_proj_component(x2, pos_rows, ln1_g, w_in[0].astype(_BF16), gains, _rope_inv_freq(), seg, b, s)],
        b * s // IN_PROJ_TILE, "in_proj")
    mlp_weights = [w_branch_a[0], w_branch_b[0], w_out[0], w_up[0], w_down[0]]
    *attn_outs, mlp_weights = _run_streams(
        _attention_components(proj, b, s, sinks2) + [_cast_component(mlp_weights, attn_steps)], attn_steps,
        "attention")
    attn_a, ob = _attention_outputs(attn_outs, b, s)
    ((out,),) = _run_streams([_out_mlp_component(x2, attn_a, ob, proj[4], *mlp_weights[:3], ln2_g, *mlp_weights[3:], s)],
                             n_tiles, "out_mlp")
    return out.reshape(b, s, d_model)
```

```python
import functools

import numpy as np
import jax
import jax.numpy as jnp
from jax import lax
from jax.experimental import pallas as pl
from jax.experimental.pallas import tpu as pltpu

D_MODEL = 1024
HEAD_DIM = 64
DIL_PAIRS = ((128, 1), (512, 4), (2048, 16))
GROUP_HEADS = 4
GROUP_WIDTH = GROUP_HEADS * HEAD_DIM
DIL_WIDTH = 768
N_DIL_GROUPS = DIL_WIDTH // GROUP_WIDTH
GROUP_QKV_WIDTH = 3 * GROUP_WIDTH
SWA_WINDOW = 128
SWA_Q_WIDTH = 512
SWA_KV_WIDTH = 128
QKV_A_WIDTH = 3 * DIL_WIDTH
QKV_B_WIDTH = SWA_Q_WIDTH + 2 * SWA_KV_WIDTH
GATE_WIDTH = 2 * D_MODEL
D_FF = 4 * D_MODEL
ROPE_THETA = 500000.0
ROPE_DIM = HEAD_DIM // 4
ROPE_HALF = ROPE_DIM // 2
BLOCK = 128
EPS = 1e-6
NEG = -1e30

LANES = 128
BF16_SUBLANES = 16
CHUNK = 256
TOKEN_TILE = 512
ATTN_TILE = 1024
IN_PROJ_TILE = 1024
PROJ_LOOKAHEAD = 2
VMEM_LIMIT_BYTES = 56 * 1024 * 1024

_BF16 = jnp.bfloat16
_F32 = jnp.float32


def _resident(shape):
    return pl.BlockSpec(shape, lambda *_: (0,) * len(shape), pipeline_mode=pl.Buffered(1))


class _Component:
    def __init__(self, inputs, in_specs, out_specs, out_shapes, scratch_shapes, body):
        self.inputs, self.in_specs, self.out_specs = list(inputs), list(in_specs), list(out_specs)
        self.out_shapes, self.scratch_shapes, self.body = list(out_shapes), list(scratch_shapes), body


def _run_components(components, n_steps, name):
    n_in = [len(c.inputs) for c in components]
    n_out = [len(c.out_specs) for c in components]
    n_scr = [len(c.scratch_shapes) for c in components]

    def body(*refs):
        refs = list(refs)
        ins, outs, scrs = [], [], []
        pos = 0
        for group, counts in ((ins, n_in), (outs, n_out), (scrs, n_scr)):
            for k in counts:
                group.append(refs[pos:pos + k])
                pos += k
        step = pl.program_id(0)
        for c, i, o, s in zip(components, ins, outs, scrs):
            c.body(i, o, s, step)

    outs = pl.pallas_call(
        body,
        grid=(n_steps,),
        in_specs=[s for c in components for s in c.in_specs],
        out_specs=[s for c in components for s in c.out_specs],
        out_shape=[s for c in components for s in c.out_shapes],
        scratch_shapes=[s for c in components for s in c.scratch_shapes],
        compiler_params=pltpu.CompilerParams(dimension_semantics=("arbitrary",), vmem_limit_bytes=VMEM_LIMIT_BYTES),
        name=name,
    )(*[a for c in components for a in c.inputs])
    split, pos = [], 0
    for k in n_out:
        split.append(outs[pos:pos + k])
        pos += k
    return split


def _in_proj_body(in_refs, out_refs, scratch_refs, step):
    del step
    x_ref, pos_ref, ln_ref, w_ref, gain_ref, freq_ref, seg_ref = in_refs
    qkv0_ref, qkv1_ref, qkv2_ref, qkvb_ref, gate_ref = out_refs
    slab1_ref, slab2_ref = scratch_refs
    tm = x_ref.shape[0]
    x = x_ref[...]
    ms = jnp.mean(x * x, axis=-1, keepdims=True)
    h = (x * lax.rsqrt(ms + EPS) * ln_ref[...]).astype(_BF16)

    ang = freq_ref[...] * pos_ref[...].astype(_F32)
    packed = jnp.concatenate([jnp.cos(ang), jnp.sin(ang), jnp.ones_like(ang),
                              jnp.zeros((LANES - 3 * ROPE_HALF, tm), _F32)], axis=0).T
    lane = lax.broadcasted_iota(jnp.int32, (tm, LANES), 1) % HEAD_DIM
    rotary = lane < ROPE_DIM
    cos_t = jnp.take_along_axis(packed, jnp.where(rotary, lane % ROPE_HALF, 2 * ROPE_HALF), axis=1,
                                mode="promise_in_bounds")
    sin_t = jnp.take_along_axis(packed, jnp.where(rotary, ROPE_HALF + lane % ROPE_HALF, 3 * ROPE_HALF), axis=1,
                                mode="promise_in_bounds")
    first_half = lane < ROPE_HALF
    sin_from_hi = jnp.where(first_half, -sin_t, 0.0)
    sin_from_lo = jnp.where(first_half, 0.0, sin_t)
    seg = seg_ref[...]

    def norm_rope(y, gain):
        ss = jnp.dot((y * y).astype(_BF16), seg, preferred_element_type=_F32)
        yn = y * lax.rsqrt(ss * (1.0 / HEAD_DIM) + EPS) * gain
        up = pltpu.roll(yn, LANES - ROPE_HALF, 1)
        dn = pltpu.roll(yn, ROPE_HALF, 1)
        return yn * cos_t + up * sin_from_hi + dn * sin_from_lo

    def project(w_col):
        return jnp.dot(h, w_ref[:, w_col:w_col + CHUNK], preferred_element_type=_F32)

    tasks = []
    group_out = (qkv0_ref, slab1_ref, slab2_ref)

    def dilated_store(g, slab):
        def store(yh):
            if g == 0:
                qkv0_ref[:, slab * LANES:(slab + 1) * LANES] = yh.astype(_BF16)
            else:
                group_out[g][slab] = yh
        return store

    def row_store(out_ref, col):
        def store(yh):
            out_ref[:, col:col + LANES] = yh.astype(_BF16)
        return store

    def gate_store(col):
        def store(yh):
            gate_ref[:, col:col + LANES] = (0.5 * jnp.tanh(0.5 * yh) + 0.5).astype(_BF16)
        return store

    n_halves = CHUNK // LANES
    for part in range(3):
        for g in range(N_DIL_GROUPS):
            tasks.append((part * DIL_WIDTH + g * GROUP_WIDTH,
                          [(part if part < 2 else None, dilated_store(g, part * n_halves + i))
                           for i in range(n_halves)]))
    for c in range(QKV_B_WIDTH // CHUNK):
        cols = [c * CHUNK + i * LANES for i in range(n_halves)]
        tasks.append((QKV_A_WIDTH + c * CHUNK,
                      [(2 if col < SWA_Q_WIDTH else (3 if col < SWA_Q_WIDTH + SWA_KV_WIDTH else None),
                        row_store(qkvb_ref, col)) for col in cols]))
    for c in range(GATE_WIDTH // CHUNK):
        tasks.append((QKV_A_WIDTH + QKV_B_WIDTH + c * CHUNK,
                      [(None, gate_store(c * CHUNK + i * LANES)) for i in range(n_halves)]))

    def epilogue(y, stores):
        for i, (gain_idx, store) in enumerate(stores):
            yh = y[:, i * LANES:(i + 1) * LANES]
            store(yh if gain_idx is None else norm_rope(yh, gain_ref[gain_idx]))

    in_flight = []
    for w_col, stores in tasks:
        in_flight.append((project(w_col), stores))
        if len(in_flight) > PROJ_LOOKAHEAD:
            epilogue(*in_flight.pop(0))
    for y, stores in in_flight:
        epilogue(y, stores)

    for g, out_ref, slab_ref in ((1, qkv1_ref, slab1_ref), (2, qkv2_ref, slab2_ref)):
        d = DIL_PAIRS[g][1]
        for r in range(d):
            for slab in range(GROUP_QKV_WIDTH // LANES):
                rows = slab_ref[slab, pl.ds(r, tm // d, stride=d), :]
                out_ref[r, :, slab * LANES:(slab + 1) * LANES] = rows.astype(_BF16)


def _in_proj_component(x2, pos_rows, ln1, w_in, gains, freq, seg, batch, seq):
    tm = IN_PROJ_TILE
    n = batch * seq
    tiles_per_seq = seq // tm
    out_row = lambda width: pl.BlockSpec((tm, width), lambda i: (i, 0))

    def deinterleaved(d):
        return pl.BlockSpec((None, d, tm // d, GROUP_QKV_WIDTH),
                            lambda i: (i // tiles_per_seq, 0, i % tiles_per_seq, 0))

    d1, d2 = DIL_PAIRS[1][1], DIL_PAIRS[2][1]
    n_slabs = GROUP_QKV_WIDTH // LANES
    return _Component(
        inputs=[x2, pos_rows, ln1, w_in, gains, freq, seg],
        in_specs=[out_row(D_MODEL), pl.BlockSpec((None, 1, tm), lambda i: (i, 0, 0)),
                  _resident((1, D_MODEL)), _resident(w_in.shape), _resident(gains.shape), _resident(freq.shape),
                  _resident(seg.shape)],
        out_specs=[out_row(GROUP_QKV_WIDTH), deinterleaved(d1), deinterleaved(d2), out_row(QKV_B_WIDTH),
                   out_row(GATE_WIDTH)],
        out_shapes=[jax.ShapeDtypeStruct((n, GROUP_QKV_WIDTH), _BF16),
                    jax.ShapeDtypeStruct((batch, d1, seq // d1, GROUP_QKV_WIDTH), _BF16),
                    jax.ShapeDtypeStruct((batch, d2, seq // d2, GROUP_QKV_WIDTH), _BF16),
                    jax.ShapeDtypeStruct((n, QKV_B_WIDTH), _BF16),
                    jax.ShapeDtypeStruct((n, GATE_WIDTH), _BF16)],
        scratch_shapes=[pltpu.VMEM((n_slabs, tm, LANES), _F32), pltpu.VMEM((n_slabs, tm, LANES), _F32)],
        body=_in_proj_body)


_NT = (((1,), (1,)), ((), ()))
_STACK = GROUP_HEADS * BLOCK


def _stacked_scores(q, k_win):
    head_of_lane = lax.broadcasted_iota(jnp.int32, q.shape, 1) // HEAD_DIM
    zero = jnp.zeros_like(q)
    q_stack = jnp.concatenate([jnp.where(head_of_lane == hd, q, zero) for hd in range(GROUP_HEADS)], axis=0)
    return lax.dot_general(q_stack, k_win, _NT, preferred_element_type=_F32)


def _band_bias(max_dist, first_key_col):
    row = lax.broadcasted_iota(jnp.int32, (_STACK, 2 * BLOCK), 0) % BLOCK
    col = lax.broadcasted_iota(jnp.int32, (_STACK, 2 * BLOCK), 1)
    dist = row + BLOCK - col
    return jnp.where((dist >= 0) & (dist <= max_dist) & (col >= first_key_col), 0.0, NEG).astype(_F32)


def _head_rows(hd):
    return slice(hd * BLOCK, (hd + 1) * BLOCK)


def _softmax2(s, bias):
    s = s + bias
    m = jnp.max(s, axis=1, keepdims=True)
    p = jnp.exp2(s - m)
    return p, m, jnp.sum(p, axis=1, keepdims=True)


def _merge_heads(stack_col):
    low = lax.broadcasted_iota(jnp.int32, (BLOCK, LANES), 1) < HEAD_DIM
    return jnp.concatenate([jnp.where(low, stack_col(2 * pair, pair), stack_col(2 * pair + 1, pair))
                            for pair in range(GROUP_HEADS // 2)], axis=1)


def _attend(s, bias, v_win):
    p, m, denom = _softmax2(s, bias)
    p = p.astype(_BF16)
    pair_rows = 2 * BLOCK
    pv = [jnp.dot(p[pair * pair_rows:(pair + 1) * pair_rows], v_win[:, pair * LANES:(pair + 1) * LANES],
                  preferred_element_type=_F32) for pair in range(GROUP_HEADS // 2)]
    spread = lambda col: _merge_heads(lambda hd, pair: jnp.broadcast_to(col[_head_rows(hd)], (BLOCK, LANES)))
    denom = spread(denom)
    o = _merge_heads(lambda hd, pair: pv[pair][(hd % 2) * BLOCK:(hd % 2 + 1) * BLOCK]) / denom
    return o, spread(m) + jnp.log2(denom)


def _window(qb, prev_ref, cur_ref, seq=None):
    idx = (lambda rows: (rows, slice(None))) if seq is None else (lambda rows: (seq, rows, slice(None)))
    rows = slice(qb * BLOCK, (qb + 1) * BLOCK)
    if qb == 0:
        prev = prev_ref[idx(slice(0, BLOCK))]
    else:
        prev = cur_ref[idx(slice((qb - 1) * BLOCK, qb * BLOCK))]
    return jnp.concatenate([prev, cur_ref[idx(rows)]], axis=0)


def _dilated_body(in_refs, out_refs, scratch_refs, step, *, max_dist, tiles_per_seq):
    del scratch_refs
    q_ref, kprev_ref, k_ref, vprev_ref, v_ref = in_refs
    o_ref, lse_ref = out_refs
    n_seqs, tile, _ = q_ref.shape
    first_tile = (step % tiles_per_seq) == 0
    bias_first = _band_bias(max_dist, jnp.where(first_tile, BLOCK, 0))
    bias_inner = _band_bias(max_dist, 0)
    for seq in range(n_seqs):
        for qb in range(tile // BLOCK):
            rows = slice(qb * BLOCK, (qb + 1) * BLOCK)
            s = _stacked_scores(q_ref[seq, rows, :], _window(qb, kprev_ref, k_ref, seq))
            o, lse = _attend(s, bias_first if qb == 0 else bias_inner, _window(qb, vprev_ref, v_ref, seq))
            o_ref[seq, rows, :] = o.astype(o_ref.dtype)
            lse_ref[seq, rows, :] = lse


def _dilated_component(qkv, window, dilation):
    n_seqs, length, _ = qkv.shape
    tile = min(ATTN_TILE, length)
    seqs = ATTN_TILE // tile
    n_blocks = tile // BLOCK
    tiles_per_seq = length // tile

    def cur(col):
        return pl.BlockSpec((seqs, tile, GROUP_WIDTH), lambda i: (i // tiles_per_seq, i % tiles_per_seq, col))

    def prev(col):
        return pl.BlockSpec((seqs, BLOCK, GROUP_WIDTH),
                            lambda i: (i // tiles_per_seq, jnp.maximum((i % tiles_per_seq) * n_blocks - 1, 0), col))

    out_spec = pl.BlockSpec((seqs, tile, GROUP_WIDTH), lambda i: (i // tiles_per_seq, i % tiles_per_seq, 0))
    return _Component(
        inputs=[qkv] * 5,
        in_specs=[cur(0), prev(1), cur(1), prev(2), cur(2)],
        out_specs=[out_spec, out_spec],
        out_shapes=[jax.ShapeDtypeStruct((n_seqs, length, GROUP_WIDTH), _BF16),
                    jax.ShapeDtypeStruct((n_seqs, length, GROUP_WIDTH), _F32)],
        scratch_shapes=[],
        body=functools.partial(_dilated_body, max_dist=window // dilation, tiles_per_seq=tiles_per_seq))


def _swa_body(in_refs, out_refs, scratch_refs, step, *, tiles_per_seq):
    sink_ref, q_ref, kprev_ref, k_ref, vprev_ref, v_ref = in_refs
    (o_ref,) = out_refs
    kdup_ref, vdup_ref = scratch_refs
    tile = q_ref.shape[0]
    first_tile = (step % tiles_per_seq) == 0
    n_kv_heads = SWA_KV_WIDTH // HEAD_DIM

    def duplicate(dst_ref, row0, t):
        low = lax.broadcasted_iota(jnp.int32, t.shape, 1) < HEAD_DIM
        t32 = t.astype(_F32)
        swapped = pltpu.roll(t32, HEAD_DIM, 1)
        dst_ref[0, row0:row0 + t.shape[0], :] = jnp.where(low, t32, swapped).astype(_BF16)
        dst_ref[1, row0:row0 + t.shape[0], :] = jnp.where(low, swapped, t32).astype(_BF16)

    for dst_ref, prev_ref, cur_ref in ((kdup_ref, kprev_ref, k_ref), (vdup_ref, vprev_ref, v_ref)):
        duplicate(dst_ref, 0, prev_ref[...])
        duplicate(dst_ref, BLOCK, cur_ref[...])

    low = lax.broadcasted_iota(jnp.int32, (BLOCK, LANES), 1) < HEAD_DIM

    def scores(qb, kv):
        rows = slice(qb * BLOCK, (qb + 1) * BLOCK)
        parts = []
        for pair in range(GROUP_HEADS // 2):
            lane0 = kv * GROUP_WIDTH + pair * LANES
            q_pair = q_ref[rows, lane0:lane0 + LANES]
            zero = jnp.zeros_like(q_pair)
            parts += [jnp.where(low, q_pair, zero), jnp.where(low, zero, q_pair)]
        q_stack = jnp.concatenate(parts, axis=0)
        k_win = kdup_ref[kv, qb * BLOCK:(qb + 2) * BLOCK, :]
        return lax.dot_general(q_stack, k_win, _NT, preferred_element_type=_F32)

    sink_col = lax.broadcasted_iota(jnp.int32, (_STACK, 2 * BLOCK), 1) == 0
    bias_first = jnp.where(sink_col, 0.0, _band_bias(SWA_WINDOW - 1, jnp.where(first_tile, BLOCK, 0)))
    bias_inner = jnp.where(sink_col, 0.0, _band_bias(SWA_WINDOW - 1, 0))
    sink_lane = lax.broadcasted_iota(jnp.int32, (BLOCK, LANES), 1) == 0
    sink_lane_stack = lax.broadcasted_iota(jnp.int32, (_STACK, LANES), 1) == 0

    def finish(qb, kv, s):
        rows = slice(qb * BLOCK, (qb + 1) * BLOCK)
        s_left = jnp.concatenate([jnp.where(sink_lane, sink_ref[kv * GROUP_HEADS + hd], s[_head_rows(hd), :LANES])
                                  for hd in range(GROUP_HEADS)], axis=0)
        s = jnp.concatenate([s_left, s[:, LANES:]], axis=1)
        p, _, denom = _softmax2(s, bias_first if qb == 0 else bias_inner)
        p = jnp.concatenate([jnp.where(sink_lane_stack, 0.0, p[:, :LANES]), p[:, LANES:]], axis=1).astype(_BF16)
        pv = jnp.dot(p, vdup_ref[kv, qb * BLOCK:(qb + 2) * BLOCK, :], preferred_element_type=_F32)
        for pair in range(GROUP_HEADS // 2):
            lane0 = kv * GROUP_WIDTH + pair * LANES
            lo, hi = _head_rows(2 * pair), _head_rows(2 * pair + 1)
            denom_pair = jnp.where(low, jnp.broadcast_to(denom[lo], (BLOCK, LANES)),
                                   jnp.broadcast_to(denom[hi], (BLOCK, LANES)))
            o_pair = jnp.where(low, pv[lo], pv[hi]) / denom_pair
            o_ref[rows, lane0:lane0 + LANES] = o_pair.astype(o_ref.dtype)

    for qb in range(tile // BLOCK):
        for kv in range(n_kv_heads):
            finish(qb, kv, scores(qb, kv))


def _swa_component(qkvb3, sinks):
    b, s, _ = qkvb3.shape
    tile = ATTN_TILE
    n_blocks = tile // BLOCK
    tiles_per_seq = s // tile
    k_col = SWA_Q_WIDTH // LANES
    v_col = k_col + SWA_KV_WIDTH // LANES

    def cur(col):
        return pl.BlockSpec((None, tile, LANES), lambda i: (i // tiles_per_seq, i % tiles_per_seq, col))

    def prev(col):
        return pl.BlockSpec((None, BLOCK, LANES),
                            lambda i: (i // tiles_per_seq, jnp.maximum((i % tiles_per_seq) * n_blocks - 1, 0), col))

    q_spec = pl.BlockSpec((None, tile, SWA_Q_WIDTH), lambda i: (i // tiles_per_seq, i % tiles_per_seq, 0))
    return _Component(
        inputs=[sinks] + [qkvb3] * 5,
        in_specs=[pl.BlockSpec(memory_space=pltpu.SMEM), q_spec, prev(k_col), cur(k_col), prev(v_col), cur(v_col)],
        out_specs=[q_spec],
        out_shapes=[jax.ShapeDtypeStruct((b, s, SWA_Q_WIDTH), _BF16)],
        scratch_shapes=[pltpu.VMEM((SWA_KV_WIDTH // HEAD_DIM, BLOCK + tile, LANES), _BF16)] * 2,
        body=functools.partial(_swa_body, tiles_per_seq=tiles_per_seq))


def _cast_body(in_refs, out_refs, scratch_refs, step):
    del scratch_refs, step
    for src_ref, dst_ref in zip(in_refs, out_refs):
        dst_ref[...] = src_ref[...].astype(dst_ref.dtype)


def _cast_component(weights, n_steps):
    specs = []
    for w in weights:
        rows, cols = w.shape
        block_rows = max(BF16_SUBLANES, rows // n_steps)
        assert rows % block_rows == 0 and rows // block_rows <= n_steps
        last = rows // block_rows - 1
        specs.append(pl.BlockSpec((block_rows, cols), lambda i, last=last: (jnp.minimum(i, last), 0)))
    return _Component(inputs=weights, in_specs=specs, out_specs=specs,
                      out_shapes=[jax.ShapeDtypeStruct(w.shape, _BF16) for w in weights],
                      scratch_shapes=[], body=_cast_body)


def _attention_components(proj, batch, seq, sinks2):
    qkv0, qkv1, qkv2, qkvb, _ = proj
    comps = []
    for (window, d), qkv in zip(DIL_PAIRS, (qkv0, qkv1, qkv2)):
        comps.append(_dilated_component(qkv.reshape(batch * d, seq // d, GROUP_QKV_WIDTH), window, d))
    comps.append(_swa_component(qkvb.reshape(batch, seq, QKV_B_WIDTH), sinks2))
    return comps


def _attention_outputs(outs, batch, seq):
    attn_a = []
    for (_, d), (o, lse) in zip(DIL_PAIRS, outs[:3]):
        shape = (batch * seq, GROUP_WIDTH) if d == 1 else (batch, d, seq // d, GROUP_WIDTH)
        attn_a.append((o.reshape(shape), lse.reshape(shape)))
    return attn_a, outs[3][0].reshape(batch * seq, SWA_Q_WIDTH)


def _out_mlp_body(in_refs, out_refs, scratch_refs, step):
    del step
    (x_ref, o0_ref, l0_ref, o1_ref, l1_ref, o2_ref, l2_ref, ob_ref, gate_ref,
     wa_ref, wb_ref, wo_ref, ln_ref, wup_ref, wdn_ref) = in_refs
    (out_ref,) = out_refs
    (slab_ref,) = scratch_refs
    tm = x_ref.shape[0]
    n_slabs = GROUP_WIDTH // LANES

    def interleaved(src_ref, base):
        d = src_ref.shape[0]
        for r in range(d):
            for slab in range(n_slabs):
                slab_ref[base + slab, pl.ds(r, tm // d, stride=d), :] = (
                    src_ref[r, :, slab * LANES:(slab + 1) * LANES].astype(_F32))
        return jnp.concatenate([slab_ref[base + slab] for slab in range(n_slabs)], axis=1)

    o0, l0 = o0_ref[...].astype(_F32), l0_ref[...]
    o1, l1 = interleaved(o1_ref, 0), interleaved(l1_ref, n_slabs)
    o2, l2 = interleaved(o2_ref, 2 * n_slabs), interleaved(l2_ref, 3 * n_slabs)
    m = jnp.maximum(jnp.maximum(l0, l1), l2)
    e0, e1, e2 = jnp.exp2(l0 - m), jnp.exp2(l1 - m), jnp.exp2(l2 - m)
    oa = ((e0 * o0 + e1 * o1 + e2 * o2) / (e0 + e1 + e2)).astype(_BF16)

    ya = jnp.dot(oa, wa_ref[...], preferred_element_type=_F32)
    yb = jnp.dot(ob_ref[...], wb_ref[...], preferred_element_type=_F32)
    gate_a = gate_ref[:, :D_MODEL].astype(_F32)
    gate_b = gate_ref[:, D_MODEL:].astype(_F32)
    mix = (gate_a * ya + gate_b * yb).astype(_BF16)
    x1 = x_ref[...] + jnp.dot(mix, wo_ref[...], preferred_element_type=_F32)

    ms = jnp.mean(x1 * x1, axis=-1, keepdims=True)
    h2 = (x1 * lax.rsqrt(ms + EPS) * ln_ref[...]).astype(_BF16)
    ff_chunk = 512
    acc = x1
    for c in range(D_FF // ff_chunk):
        u = jnp.dot(h2, wup_ref[:, c * ff_chunk:(c + 1) * ff_chunk], preferred_element_type=_F32)
        a = jnp.square(jnp.maximum(u, 0.0)).astype(_BF16)
        acc = acc + jnp.dot(a, wdn_ref[c * ff_chunk:(c + 1) * ff_chunk, :], preferred_element_type=_F32)
    out_ref[...] = acc


def _out_mlp_component(x2, attn_a, ob, gates, wa, wb, wo, ln2, wup, wdn, seq):
    tm = TOKEN_TILE
    tiles_per_seq = seq // tm
    row = lambda width: pl.BlockSpec((tm, width), lambda i: (i, 0))
    full_row = row(D_MODEL)

    def deinterleaved(d):
        return pl.BlockSpec((None, d, tm // d, GROUP_WIDTH), lambda i: (i // tiles_per_seq, 0, i % tiles_per_seq, 0))

    group_specs = [row(GROUP_WIDTH)] * 2
    for _, d in DIL_PAIRS[1:]:
        group_specs += [deinterleaved(d)] * 2
    return _Component(
        inputs=[x2] + [a for pair in attn_a for a in pair] + [ob, gates, wa, wb, wo, ln2, wup, wdn],
        in_specs=[full_row] + group_specs + [row(SWA_Q_WIDTH), row(GATE_WIDTH),
                  _resident(wa.shape), _resident(wb.shape), _resident(wo.shape), _resident((1, D_MODEL)),
                  _resident(wup.shape), _resident(wdn.shape)],
        out_specs=[full_row],
        out_shapes=[jax.ShapeDtypeStruct(x2.shape, _F32)],
        scratch_shapes=[pltpu.VMEM((4 * GROUP_WIDTH // LANES, tm, LANES), _F32)],
        body=_out_mlp_body)


def _rope_inv_freq():
    return (ROPE_THETA ** (-jnp.arange(0, ROPE_DIM, 2, dtype=_F32) / ROPE_DIM))[:, None]


def kernel(x, positions, ln1_g, w_in, q_norm_a, k_norm_a, q_norm_b, k_norm_b, sinks,
           w_branch_a, w_branch_b, w_out, ln2_g, w_up, w_down):
    b, s, d_model = x.shape
    assert d_model == D_MODEL and ln1_g.shape[0] == 1 and s % (16 * BLOCK) == 0
    assert s % TOKEN_TILE == 0 and s % IN_PROJ_TILE == 0 and s % ATTN_TILE == 0
    n_tiles = b * s // TOKEN_TILE
    attn_steps = b * s // ATTN_TILE
    x2 = x.reshape(b * s, d_model)
    log2e = np.float32(np.log2(np.e))
    scale = np.float32(log2e / np.sqrt(HEAD_DIM))
    gains = jnp.stack([jnp.tile(q_norm_a[0] * scale, 2), jnp.tile(k_norm_a[0], 2),
                       jnp.tile(q_norm_b[0] * scale, 2), jnp.tile(k_norm_b[0], 2)])[:, None, :]
    head = np.arange(LANES) // HEAD_DIM
    seg = jnp.asarray(head[:, None] == head[None, :], _BF16)
    pos_rows = positions.reshape(b * s // IN_PROJ_TILE, 1, IN_PROJ_TILE)
    sinks2 = sinks[0] * log2e

    (proj,) = _run_components(
        [_in_proj_component(x2, pos_rows, ln1_g, w_in[0].astype(_BF16), gains, _rope_inv_freq(), seg, b, s)],
        b * s // IN_PROJ_TILE, "in_proj")
    mlp_weights = [w_branch_a[0], w_branch_b[0], w_out[0], w_up[0], w_down[0]]
    *attn_outs, mlp_weights = _run_components(
        _attention_components(proj, b, s, sinks2) + [_cast_component(mlp_weights, attn_steps)], attn_steps,
        "attention")
    attn_a, ob = _attention_outputs(attn_outs, b, s)
    ((out,),) = _run_components(
        [_out_mlp_component(x2, attn_a, ob, proj[4], *mlp_weights[:3], ln2_g, *mlp_weights[3:], s)],
        n_tiles, "out_mlp")
    return out.reshape(b, s, d_model)
```

```python
import functools

import numpy as np
import jax
import jax.numpy as jnp
from jax import lax
from jax.experimental import pallas as pl
from jax.experimental.pallas import tpu as pltpu

D_MODEL = 1024
HEAD_DIM = 64
DIL_PAIRS = ((128, 1), (512, 4), (2048, 16))
GROUP_HEADS = 4
GROUP_WIDTH = GROUP_HEADS * HEAD_DIM
DIL_WIDTH = 768
N_DIL_GROUPS = DIL_WIDTH // GROUP_WIDTH
GROUP_QKV_WIDTH = 3 * GROUP_WIDTH
SWA_WINDOW = 128
SWA_Q_WIDTH = 512
SWA_KV_WIDTH = 128
QKV_A_WIDTH = 3 * DIL_WIDTH
QKV_B_WIDTH = SWA_Q_WIDTH + 2 * SWA_KV_WIDTH
GATE_WIDTH = 2 * D_MODEL
D_FF = 4 * D_MODEL
ROPE_THETA = 500000.0
ROPE_DIM = HEAD_DIM // 4
ROPE_HALF = ROPE_DIM // 2
BLOCK = 128
EPS = 1e-6
NEG = -1e30

LANES = 128
BF16_SUBLANES = 16
CHUNK = 256
TOKEN_TILE = 512
ATTN_TILE = 1024
IN_PROJ_TILE = 1024
PROJ_LOOKAHEAD = 2
VMEM_LIMIT_BYTES = 56 * 1024 * 1024

_BF16 = jnp.bfloat16
_F32 = jnp.float32


def _resident(shape):
    return pl.BlockSpec(shape, lambda *_: (0,) * len(shape), pipeline_mode=pl.Buffered(1))


class _Component:
    def __init__(self, inputs, in_specs, out_specs, out_shapes, scratch_shapes, body):
        self.inputs, self.in_specs, self.out_specs = list(inputs), list(in_specs), list(out_specs)
        self.out_shapes, self.scratch_shapes, self.body = list(out_shapes), list(scratch_shapes), body


def _run_components(components, n_steps, name):
    n_in = [len(c.inputs) for c in components]
    n_out = [len(c.out_specs) for c in components]
    n_scr = [len(c.scratch_shapes) for c in components]

    def body(*refs):
        refs = list(refs)
        ins, outs, scrs = [], [], []
        pos = 0
        for group, counts in ((ins, n_in), (outs, n_out), (scrs, n_scr)):
            for k in counts:
                group.append(refs[pos:pos + k])
                pos += k
        step = pl.program_id(0)
        for c, i, o, s in zip(components, ins, outs, scrs):
            c.body(i, o, s, step)

    outs = pl.pallas_call(
        body,
        grid=(n_steps,),
        in_specs=[s for c in components for s in c.in_specs],
        out_specs=[s for c in components for s in c.out_specs],
        out_shape=[s for c in components for s in c.out_shapes],
        scratch_shapes=[s for c in components for s in c.scratch_shapes],
        compiler_params=pltpu.CompilerParams(dimension_semantics=("arbitrary",), vmem_limit_bytes=VMEM_LIMIT_BYTES),
        name=name,
    )(*[a for c in components for a in c.inputs])
    split, pos = [], 0
    for k in n_out:
        split.append(outs[pos:pos + k])
        pos += k
    return split


def _in_proj_body(in_refs, out_refs, scratch_refs, step):
    del step
    x_ref, pos_ref, ln_ref, w_ref, gain_ref, freq_ref, seg_ref = in_refs
    qkv0_ref, qkv1_ref, qkv2_ref, qkvb_ref, gate_ref = out_refs
    slab1_ref, slab2_ref = scratch_refs
    tm = x_ref.shape[0]
    x = x_ref[...]
    ms = jnp.mean(x * x, axis=-1, keepdims=True)
    h = (x * lax.rsqrt(ms + EPS) * ln_ref[...]).astype(_BF16)

    ang = freq_ref[...] * pos_ref[...].astype(_F32)
    packed = jnp.concatenate([jnp.cos(ang), jnp.sin(ang), jnp.ones_like(ang),
                              jnp.zeros((LANES - 3 * ROPE_HALF, tm), _F32)], axis=0).T
    lane = lax.broadcasted_iota(jnp.int32, (tm, LANES), 1) % HEAD_DIM
    rotary = lane < ROPE_DIM
    cos_t = jnp.take_along_axis(packed, jnp.where(rotary, lane % ROPE_HALF, 2 * ROPE_HALF), axis=1,
                                mode="promise_in_bounds")
    sin_t = jnp.take_along_axis(packed, jnp.where(rotary, ROPE_HALF + lane % ROPE_HALF, 3 * ROPE_HALF), axis=1,
                                mode="promise_in_bounds")
    first_half = lane < ROPE_HALF
    sin_from_hi = jnp.where(first_half, -sin_t, 0.0)
    sin_from_lo = jnp.where(first_half, 0.0, sin_t)
    seg = seg_ref[...]

    def norm_rope(y, gain):
        ss = jnp.dot((y * y).astype(_BF16), seg, preferred_element_type=_F32)
        yn = y * lax.rsqrt(ss * (1.0 / HEAD_DIM) + EPS) * gain
        up = pltpu.roll(yn, LANES - ROPE_HALF, 1)
        dn = pltpu.roll(yn, ROPE_HALF, 1)
        return yn * cos_t + up * sin_from_hi + dn * sin_from_lo

    def project(w_col):
        return jnp.dot(h, w_ref[:, w_col:w_col + CHUNK], preferred_element_type=_F32)

    tasks = []
    group_out = (qkv0_ref, slab1_ref, slab2_ref)

    def dilated_store(g, slab):
        def store(yh):
            if g == 0:
                qkv0_ref[:, slab * LANES:(slab + 1) * LANES] = yh.astype(_BF16)
            else:
                group_out[g][slab] = yh
        return store

    def row_store(out_ref, col):
        def store(yh):
            out_ref[:, col:col + LANES] = yh.astype(_BF16)
        return store

    def gate_store(col):
        def store(yh):
            gate_ref[:, col:col + LANES] = (0.5 * jnp.tanh(0.5 * yh) + 0.5).astype(_BF16)
        return store

    n_halves = CHUNK // LANES
    for part in range(3):
        for g in range(N_DIL_GROUPS):
            tasks.append((part * DIL_WIDTH + g * GROUP_WIDTH,
                          [(part if part < 2 else None, dilated_store(g, part * n_halves + i))
                           for i in range(n_halves)]))
    for c in range(QKV_B_WIDTH // CHUNK):
        cols = [c * CHUNK + i * LANES for i in range(n_halves)]
        tasks.append((QKV_A_WIDTH + c * CHUNK,
                      [(2 if col < SWA_Q_WIDTH else (3 if col < SWA_Q_WIDTH + SWA_KV_WIDTH else None),
                        row_store(qkvb_ref, col)) for col in cols]))
    for c in range(GATE_WIDTH // CHUNK):
        tasks.append((QKV_A_WIDTH + QKV_B_WIDTH + c * CHUNK,
                      [(None, gate_store(c * CHUNK + i * LANES)) for i in range(n_halves)]))

    def epilogue(y, stores):
        for i, (gain_idx, store) in enumerate(stores):
            yh = y[:, i * LANES:(i + 1) * LANES]
            store(yh if gain_idx is None else norm_rope(yh, gain_ref[gain_idx]))

    in_flight = []
    for w_col, stores in tasks:
        in_flight.append((project(w_col), stores))
        if len(in_flight) > PROJ_LOOKAHEAD:
            epilogue(*in_flight.pop(0))
    for y, stores in in_flight:
        epilogue(y, stores)

    for g, out_ref, slab_ref in ((1, qkv1_ref, slab1_ref), (2, qkv2_ref, slab2_ref)):
        d = DIL_PAIRS[g][1]
        for r in range(d):
            for slab in range(GROUP_QKV_WIDTH // LANES):
                rows = slab_ref[slab, pl.ds(r, tm // d, stride=d), :]
                out_ref[r, :, slab * LANES:(slab + 1) * LANES] = rows.astype(_BF16)


def _in_proj_component(x2, pos_rows, ln1, w_in, gains, freq, seg, batch, seq):
    tm = IN_PROJ_TILE
    n = batch * seq
    tiles_per_seq = seq // tm
    out_row = lambda width: pl.BlockSpec((tm, width), lambda i: (i, 0))

    def deinterleaved(d):
        return pl.BlockSpec((None, d, tm // d, GROUP_QKV_WIDTH),
                            lambda i: (i // tiles_per_seq, 0, i % tiles_per_seq, 0))

    d1, d2 = DIL_PAIRS[1][1], DIL_PAIRS[2][1]
    n_slabs = GROUP_QKV_WIDTH // LANES
    return _Component(
        inputs=[x2, pos_rows, ln1, w_in, gains, freq, seg],
        in_specs=[out_row(D_MODEL), pl.BlockSpec((None, 1, tm), lambda i: (i, 0, 0)),
                  _resident((1, D_MODEL)), _resident(w_in.shape), _resident(gains.shape), _resident(freq.shape),
                  _resident(seg.shape)],
        out_specs=[out_row(GROUP_QKV_WIDTH), deinterleaved(d1), deinterleaved(d2), out_row(QKV_B_WIDTH),
                   out_row(GATE_WIDTH)],
        out_shapes=[jax.ShapeDtypeStruct((n, GROUP_QKV_WIDTH), _BF16),
                    jax.ShapeDtypeStruct((batch, d1, seq // d1, GROUP_QKV_WIDTH), _BF16),
                    jax.ShapeDtypeStruct((batch, d2, seq // d2, GROUP_QKV_WIDTH), _BF16),
                    jax.ShapeDtypeStruct((n, QKV_B_WIDTH), _BF16),
                    jax.ShapeDtypeStruct((n, GATE_WIDTH), _BF16)],
        scratch_shapes=[pltpu.VMEM((n_slabs, tm, LANES), _F32), pltpu.VMEM((n_slabs, tm, LANES), _F32)],
        body=_in_proj_body)


_NT = (((1,), (1,)), ((), ()))
_STACK = GROUP_HEADS * BLOCK


def _stacked_scores(q, k_win):
    head_of_lane = lax.broadcasted_iota(jnp.int32, q.shape, 1) // HEAD_DIM
    zero = jnp.zeros_like(q)
    q_stack = jnp.concatenate([jnp.where(head_of_lane == hd, q, zero) for hd in range(GROUP_HEADS)], axis=0)
    return lax.dot_general(q_stack, k_win, _NT, preferred_element_type=_F32)


def _band_bias(max_dist, first_key_col):
    row = lax.broadcasted_iota(jnp.int32, (_STACK, 2 * BLOCK), 0) % BLOCK
    col = lax.broadcasted_iota(jnp.int32, (_STACK, 2 * BLOCK), 1)
    dist = row + BLOCK - col
    return jnp.where((dist >= 0) & (dist <= max_dist) & (col >= first_key_col), 0.0, NEG).astype(_F32)


def _head_rows(hd):
    return slice(hd * BLOCK, (hd + 1) * BLOCK)


def _merge_heads(stack_col):
    low = lax.broadcasted_iota(jnp.int32, (BLOCK, LANES), 1) < HEAD_DIM
    return jnp.concatenate([jnp.where(low, stack_col(2 * pair, pair), stack_col(2 * pair + 1, pair))
                            for pair in range(GROUP_HEADS // 2)], axis=1)


def _attend(s, bias, v_win):
    s = s + bias
    m = jnp.max(s, axis=1, keepdims=True)
    p = jnp.exp2(s - m).astype(_BF16)
    pair_rows = 2 * BLOCK
    ones = jnp.ones((v_win.shape[0], LANES), _BF16)
    pv = [jnp.dot(p[pair * pair_rows:(pair + 1) * pair_rows],
                  jnp.concatenate([v_win[:, pair * LANES:(pair + 1) * LANES], ones], axis=1),
                  preferred_element_type=_F32) for pair in range(GROUP_HEADS // 2)]
    head_block = lambda hd: slice((hd % 2) * BLOCK, (hd % 2 + 1) * BLOCK)
    denom = _merge_heads(lambda hd, pair: pv[pair][head_block(hd), LANES:])
    o = _merge_heads(lambda hd, pair: pv[pair][head_block(hd), :LANES]) / denom
    m = _merge_heads(lambda hd, pair: jnp.broadcast_to(m[_head_rows(hd)], (BLOCK, LANES)))
    return o, m + jnp.log2(denom)


def _window(qb, prev_ref, cur_ref, seq=None):
    idx = (lambda rows: (rows, slice(None))) if seq is None else (lambda rows: (seq, rows, slice(None)))
    rows = slice(qb * BLOCK, (qb + 1) * BLOCK)
    if qb == 0:
        prev = prev_ref[idx(slice(0, BLOCK))]
    else:
        prev = cur_ref[idx(slice((qb - 1) * BLOCK, qb * BLOCK))]
    return jnp.concatenate([prev, cur_ref[idx(rows)]], axis=0)


def _dilated_body(in_refs, out_refs, scratch_refs, step, *, max_dist, tiles_per_seq):
    del scratch_refs
    q_ref, kprev_ref, k_ref, vprev_ref, v_ref = in_refs
    o_ref, lse_ref = out_refs
    n_seqs, tile, _ = q_ref.shape
    first_tile = (step % tiles_per_seq) == 0
    bias_first = _band_bias(max_dist, jnp.where(first_tile, BLOCK, 0))
    bias_inner = _band_bias(max_dist, 0)
    for seq in range(n_seqs):
        for qb in range(tile // BLOCK):
            rows = slice(qb * BLOCK, (qb + 1) * BLOCK)
            s = _stacked_scores(q_ref[seq, rows, :], _window(qb, kprev_ref, k_ref, seq))
            o, lse = _attend(s, bias_first if qb == 0 else bias_inner, _window(qb, vprev_ref, v_ref, seq))
            o_ref[seq, rows, :] = o.astype(o_ref.dtype)
            lse_ref[seq, rows, :] = lse


def _dilated_component(qkv, window, dilation):
    n_seqs, length, _ = qkv.shape
    tile = min(ATTN_TILE, length)
    seqs = ATTN_TILE // tile
    n_blocks = tile // BLOCK
    tiles_per_seq = length // tile

    def cur(col):
        return pl.BlockSpec((seqs, tile, GROUP_WIDTH), lambda i: (i // tiles_per_seq, i % tiles_per_seq, col))

    def prev(col):
        return pl.BlockSpec((seqs, BLOCK, GROUP_WIDTH),
                            lambda i: (i // tiles_per_seq, jnp.maximum((i % tiles_per_seq) * n_blocks - 1, 0), col))

    out_spec = pl.BlockSpec((seqs, tile, GROUP_WIDTH), lambda i: (i // tiles_per_seq, i % tiles_per_seq, 0))
    return _Component(
        inputs=[qkv] * 5,
        in_specs=[cur(0), prev(1), cur(1), prev(2), cur(2)],
        out_specs=[out_spec, out_spec],
        out_shapes=[jax.ShapeDtypeStruct((n_seqs, length, GROUP_WIDTH), _BF16),
                    jax.ShapeDtypeStruct((n_seqs, length, GROUP_WIDTH), _F32)],
        scratch_shapes=[],
        body=functools.partial(_dilated_body, max_dist=window // dilation, tiles_per_seq=tiles_per_seq))


def _swa_body(in_refs, out_refs, scratch_refs, step, *, tiles_per_seq):
    sink_ref, q_ref, kprev_ref, k_ref, vprev_ref, v_ref = in_refs
    (o_ref,) = out_refs
    kdup_ref, vdup_ref = scratch_refs
    tile = q_ref.shape[0]
    first_tile = (step % tiles_per_seq) == 0
    n_kv_heads = SWA_KV_WIDTH // HEAD_DIM

    def duplicate(dst_ref, row0, t):
        low = lax.broadcasted_iota(jnp.int32, t.shape, 1) < HEAD_DIM
        t32 = t.astype(_F32)
        swapped = pltpu.roll(t32, HEAD_DIM, 1)
        dst_ref[0, row0:row0 + t.shape[0], :] = jnp.where(low, t32, swapped).astype(_BF16)
        dst_ref[1, row0:row0 + t.shape[0], :] = jnp.where(low, swapped, t32).astype(_BF16)

    for dst_ref, prev_ref, cur_ref in ((kdup_ref, kprev_ref, k_ref), (vdup_ref, vprev_ref, v_ref)):
        duplicate(dst_ref, 0, prev_ref[...])
        duplicate(dst_ref, BLOCK, cur_ref[...])

    low = lax.broadcasted_iota(jnp.int32, (BLOCK, LANES), 1) < HEAD_DIM

    def scores(qb, kv):
        rows = slice(qb * BLOCK, (qb + 1) * BLOCK)
        parts = []
        for pair in range(GROUP_HEADS // 2):
            lane0 = kv * GROUP_WIDTH + pair * LANES
            q_pair = q_ref[rows, lane0:lane0 + LANES]
            zero = jnp.zeros_like(q_pair)
            parts += [jnp.where(low, q_pair, zero), jnp.where(low, zero, q_pair)]
        q_stack = jnp.concatenate(parts, axis=0)
        k_win = kdup_ref[kv, qb * BLOCK:(qb + 2) * BLOCK, :]
        return lax.dot_general(q_stack, k_win, _NT, preferred_element_type=_F32)

    sink_col = lax.broadcasted_iota(jnp.int32, (_STACK, 2 * BLOCK), 1) == 0
    bias_first = jnp.where(sink_col, 0.0, _band_bias(SWA_WINDOW - 1, jnp.where(first_tile, BLOCK, 0)))
    bias_inner = jnp.where(sink_col, 0.0, _band_bias(SWA_WINDOW - 1, 0))
    sink_lane = lax.broadcasted_iota(jnp.int32, (BLOCK, LANES), 1) == 0
    sink_row = lax.broadcasted_iota(jnp.int32, (2 * BLOCK, LANES), 0) == 0

    def finish(qb, kv, s):
        rows = slice(qb * BLOCK, (qb + 1) * BLOCK)
        s_left = jnp.concatenate([jnp.where(sink_lane, sink_ref[kv * GROUP_HEADS + hd], s[_head_rows(hd), :LANES])
                                  for hd in range(GROUP_HEADS)], axis=0)
        s = jnp.concatenate([s_left, s[:, LANES:]], axis=1) + (bias_first if qb == 0 else bias_inner)
        p = jnp.exp2(s - jnp.max(s, axis=1, keepdims=True)).astype(_BF16)
        v_win = jnp.where(sink_row, jnp.zeros((), _BF16), vdup_ref[kv, qb * BLOCK:(qb + 2) * BLOCK, :])
        pv = jnp.dot(p, jnp.concatenate([v_win, jnp.ones_like(v_win)], axis=1),
                     preferred_element_type=_F32)
        for pair in range(GROUP_HEADS // 2):
            lane0 = kv * GROUP_WIDTH + pair * LANES
            lo, hi = pv[_head_rows(2 * pair)], pv[_head_rows(2 * pair + 1)]
            o_pair = jnp.where(low, lo[:, :LANES], hi[:, :LANES]) / jnp.where(low, lo[:, LANES:], hi[:, LANES:])
            o_ref[rows, lane0:lane0 + LANES] = o_pair.astype(o_ref.dtype)

    for qb in range(tile // BLOCK):
        for kv in range(n_kv_heads):
            finish(qb, kv, scores(qb, kv))


def _swa_component(qkvb3, sinks):
    b, s, _ = qkvb3.shape
    tile = ATTN_TILE
    n_blocks = tile // BLOCK
    tiles_per_seq = s // tile
    k_col = SWA_Q_WIDTH // LANES
    v_col = k_col + SWA_KV_WIDTH // LANES

    def cur(col):
        return pl.BlockSpec((None, tile, LANES), lambda i: (i // tiles_per_seq, i % tiles_per_seq, col))

    def prev(col):
        return pl.BlockSpec((None, BLOCK, LANES),
                            lambda i: (i // tiles_per_seq, jnp.maximum((i % tiles_per_seq) * n_blocks - 1, 0), col))

    q_spec = pl.BlockSpec((None, tile, SWA_Q_WIDTH), lambda i: (i // tiles_per_seq, i % tiles_per_seq, 0))
    return _Component(
        inputs=[sinks] + [qkvb3] * 5,
        in_specs=[pl.BlockSpec(memory_space=pltpu.SMEM), q_spec, prev(k_col), cur(k_col), prev(v_col), cur(v_col)],
        out_specs=[q_spec],
        out_shapes=[jax.ShapeDtypeStruct((b, s, SWA_Q_WIDTH), _BF16)],
        scratch_shapes=[pltpu.VMEM((SWA_KV_WIDTH // HEAD_DIM, BLOCK + tile, LANES), _BF16)] * 2,
        body=functools.partial(_swa_body, tiles_per_seq=tiles_per_seq))


def _cast_body(in_refs, out_refs, scratch_refs, step):
    del scratch_refs, step
    for src_ref, dst_ref in zip(in_refs, out_refs):
        dst_ref[...] = src_ref[...].astype(dst_ref.dtype)


def _cast_component(weights, n_steps):
    specs = []
    for w in weights:
        rows, cols = w.shape
        block_rows = max(BF16_SUBLANES, rows // n_steps)
        assert rows % block_rows == 0 and rows // block_rows <= n_steps
        last = rows // block_rows - 1
        specs.append(pl.BlockSpec((block_rows, cols), lambda i, last=last: (jnp.minimum(i, last), 0)))
    return _Component(inputs=weights, in_specs=specs, out_specs=specs,
                      out_shapes=[jax.ShapeDtypeStruct(w.shape, _BF16) for w in weights],
                      scratch_shapes=[], body=_cast_body)


def _attention_components(proj, batch, seq, sinks2):
    qkv0, qkv1, qkv2, qkvb, _ = proj
    comps = []
    for (window, d), qkv in zip(DIL_PAIRS, (qkv0, qkv1, qkv2)):
        comps.append(_dilated_component(qkv.reshape(batch * d, seq // d, GROUP_QKV_WIDTH), window, d))
    comps.append(_swa_component(qkvb.reshape(batch, seq, QKV_B_WIDTH), sinks2))
    return comps


def _attention_outputs(outs, batch, seq):
    attn_a = []
    for (_, d), (o, lse) in zip(DIL_PAIRS, outs[:3]):
        shape = (batch * seq, GROUP_WIDTH) if d == 1 else (batch, d, seq // d, GROUP_WIDTH)
        attn_a.append((o.reshape(shape), lse.reshape(shape)))
    return attn_a, outs[3][0].reshape(batch * seq, SWA_Q_WIDTH)


def _out_mlp_body(in_refs, out_refs, scratch_refs, step):
    del step
    (x_ref, o0_ref, l0_ref, o1_ref, l1_ref, o2_ref, l2_ref, ob_ref, gate_ref,
     wa_ref, wb_ref, wo_ref, ln_ref, wup_ref, wdn_ref) = in_refs
    (out_ref,) = out_refs
    (slab_ref,) = scratch_refs
    tm = x_ref.shape[0]
    n_slabs = GROUP_WIDTH // LANES

    def interleaved(src_ref, base):
        d = src_ref.shape[0]
        for r in range(d):
            for slab in range(n_slabs):
                slab_ref[base + slab, pl.ds(r, tm // d, stride=d), :] = (
                    src_ref[r, :, slab * LANES:(slab + 1) * LANES].astype(_F32))
        return jnp.concatenate([slab_ref[base + slab] for slab in range(n_slabs)], axis=1)

    o0, l0 = o0_ref[...].astype(_F32), l0_ref[...]
    o1, l1 = interleaved(o1_ref, 0), interleaved(l1_ref, n_slabs)
    o2, l2 = interleaved(o2_ref, 2 * n_slabs), interleaved(l2_ref, 3 * n_slabs)
    m = jnp.maximum(jnp.maximum(l0, l1), l2)
    e0, e1, e2 = jnp.exp2(l0 - m), jnp.exp2(l1 - m), jnp.exp2(l2 - m)
    oa = ((e0 * o0 + e1 * o1 + e2 * o2) / (e0 + e1 + e2)).astype(_BF16)

    ya = jnp.dot(oa, wa_ref[...], preferred_element_type=_F32)
    yb = jnp.dot(ob_ref[...], wb_ref[...], preferred_element_type=_F32)
    gate_a = gate_ref[:, :D_MODEL].astype(_F32)
    gate_b = gate_ref[:, D_MODEL:].astype(_F32)
    mix = (gate_a * ya + gate_b * yb).astype(_BF16)
    x1 = x_ref[...] + jnp.dot(mix, wo_ref[...], preferred_element_type=_F32)

    ms = jnp.mean(x1 * x1, axis=-1, keepdims=True)
    h2 = (x1 * lax.rsqrt(ms + EPS) * ln_ref[...]).astype(_BF16)
    ff_chunk = 512
    acc = x1
    for c in range(D_FF // ff_chunk):
        u = jnp.dot(h2, wup_ref[:, c * ff_chunk:(c + 1) * ff_chunk], preferred_element_type=_F32)
        a = jnp.square(jnp.maximum(u, 0.0)).astype(_BF16)
        acc = acc + jnp.dot(a, wdn_ref[c * ff_chunk:(c + 1) * ff_chunk, :], preferred_element_type=_F32)
    out_ref[...] = acc


def _out_mlp_component(x2, attn_a, ob, gates, wa, wb, wo, ln2, wup, wdn, seq):
    tm = TOKEN_TILE
    tiles_per_seq = seq // tm
    row = lambda width: pl.BlockSpec((tm, width), lambda i: (i, 0))
    full_row = row(D_MODEL)

    def deinterleaved(d):
        return pl.BlockSpec((None, d, tm // d, GROUP_WIDTH), lambda i: (i // tiles_per_seq, 0, i % tiles_per_seq, 0))

    group_specs = [row(GROUP_WIDTH)] * 2
    for _, d in DIL_PAIRS[1:]:
        group_specs += [deinterleaved(d)] * 2
    return _Component(
        inputs=[x2] + [a for pair in attn_a for a in pair] + [ob, gates, wa, wb, wo, ln2, wup, wdn],
        in_specs=[full_row] + group_specs + [row(SWA_Q_WIDTH), row(GATE_WIDTH),
                  _resident(wa.shape), _resident(wb.shape), _resident(wo.shape), _resident((1, D_MODEL)),
                  _resident(wup.shape), _resident(wdn.shape)],
        out_specs=[full_row],
        out_shapes=[jax.ShapeDtypeStruct(x2.shape, _F32)],
        scratch_shapes=[pltpu.VMEM((4 * GROUP_WIDTH // LANES, tm, LANES), _F32)],
        body=_out_mlp_body)


def _rope_inv_freq():
    return (ROPE_THETA ** (-jnp.arange(0, ROPE_DIM, 2, dtype=_F32) / ROPE_DIM))[:, None]


def kernel(x, positions, ln1_g, w_in, q_norm_a, k_norm_a, q_norm_b, k_norm_b, sinks,
           w_branch_a, w_branch_b, w_out, ln2_g, w_up, w_down):
    b, s, d_model = x.shape
    assert d_model == D_MODEL and ln1_g.shape[0] == 1 and s % (16 * BLOCK) == 0
    assert s % TOKEN_TILE == 0 and s % IN_PROJ_TILE == 0 and s % ATTN_TILE == 0
    n_tiles = b * s // TOKEN_TILE
    attn_steps = b * s // ATTN_TILE
    x2 = x.reshape(b * s, d_model)
    log2e = np.float32(np.log2(np.e))
    scale = np.float32(log2e / np.sqrt(HEAD_DIM))
    gains = jnp.stack([jnp.tile(q_norm_a[0] * scale, 2), jnp.tile(k_norm_a[0], 2),
                       jnp.tile(q_norm_b[0] * scale, 2), jnp.tile(k_norm_b[0], 2)])[:, None, :]
    head = np.arange(LANES) // HEAD_DIM
    seg = jnp.asarray(head[:, None] == head[None, :], _BF16)
    pos_rows = positions.reshape(b * s // IN_PROJ_TILE, 1, IN_PROJ_TILE)
    sinks2 = sinks[0] * log2e

    (proj,) = _run_components(
        [_in_proj_component(x2, pos_rows, ln1_g, w_in[0].astype(_BF16), gains, _rope_inv_freq(), seg, b, s)],
        b * s // IN_PROJ_TILE, "in_proj")
    mlp_weights = [w_branch_a[0], w_branch_b[0], w_out[0], w_up[0], w_down[0]]
    *attn_outs, mlp_weights = _run_components(
        _attention_components(proj, b, s, sinks2) + [_cast_component(mlp_weights, attn_steps)], attn_steps,
        "attention")
    attn_a, ob = _attention_outputs(attn_outs, b, s)
    ((out,),) = _run_components(
        [_out_mlp_component(x2, attn_a, ob, proj[4], *mlp_weights[:3], ln2_g, *mlp_weights[3:], s)],
        n_tiles, "out_mlp")
    return out.reshape(b, s, d_model)
```

```python
import functools

import numpy as np
import jax
import jax.numpy as jnp
from jax import lax
from jax.experimental import pallas as pl
from jax.experimental.pallas import tpu as pltpu

D_MODEL = 1024
HEAD_DIM = 64
DIL_PAIRS = ((128, 1), (512, 4), (2048, 16))
GROUP_HEADS = 4
GROUP_WIDTH = GROUP_HEADS * HEAD_DIM
DIL_WIDTH = 768
N_DIL_GROUPS = DIL_WIDTH // GROUP_WIDTH
GROUP_QKV_WIDTH = 3 * GROUP_WIDTH
SWA_WINDOW = 128
SWA_Q_WIDTH = 512
SWA_KV_WIDTH = 128
QKV_A_WIDTH = 3 * DIL_WIDTH
QKV_B_WIDTH = SWA_Q_WIDTH + 2 * SWA_KV_WIDTH
GATE_WIDTH = 2 * D_MODEL
D_FF = 4 * D_MODEL
ROPE_THETA = 500000.0
ROPE_DIM = HEAD_DIM // 4
ROPE_HALF = ROPE_DIM // 2
BLOCK = 128
EPS = 1e-6
NEG = -1e30

LANES = 128
BF16_SUBLANES = 16
CHUNK = 256
FF_CHUNK = 512
TOKEN_TILE = 512
ATTN_TILE = 1024
IN_PROJ_TILE = 1024
PROJ_LOOKAHEAD = 2
VMEM_LIMIT_BYTES = 56 * 1024 * 1024

_BF16 = jnp.bfloat16
_F32 = jnp.float32


def _resident(shape):
    return pl.BlockSpec(shape, lambda *_: (0,) * len(shape), pipeline_mode=pl.Buffered(1))


class _Component:
    def __init__(self, inputs, in_specs, out_specs, out_shapes, scratch_shapes, body):
        self.inputs, self.in_specs, self.out_specs = list(inputs), list(in_specs), list(out_specs)
        self.out_shapes, self.scratch_shapes, self.body = list(out_shapes), list(scratch_shapes), body


def _run_components(components, n_steps, name):
    n_in = [len(c.inputs) for c in components]
    n_out = [len(c.out_specs) for c in components]
    n_scr = [len(c.scratch_shapes) for c in components]

    def body(*refs):
        refs = list(refs)
        ins, outs, scrs = [], [], []
        pos = 0
        for group, counts in ((ins, n_in), (outs, n_out), (scrs, n_scr)):
            for k in counts:
                group.append(refs[pos:pos + k])
                pos += k
        step = pl.program_id(0)
        for c, i, o, s in zip(components, ins, outs, scrs):
            c.body(i, o, s, step)

    outs = pl.pallas_call(
        body,
        grid=(n_steps,),
        in_specs=[s for c in components for s in c.in_specs],
        out_specs=[s for c in components for s in c.out_specs],
        out_shape=[s for c in components for s in c.out_shapes],
        scratch_shapes=[s for c in components for s in c.scratch_shapes],
        compiler_params=pltpu.CompilerParams(dimension_semantics=("arbitrary",), vmem_limit_bytes=VMEM_LIMIT_BYTES),
        name=name,
    )(*[a for c in components for a in c.inputs])
    split, pos = [], 0
    for k in n_out:
        split.append(outs[pos:pos + k])
        pos += k
    return split


def _in_proj_body(in_refs, out_refs, scratch_refs, step):
    del step
    x_ref, pos_ref, ln_ref, w_ref, gain_ref, freq_ref, seg_ref = in_refs
    qkv0_ref, qkv1_ref, qkv2_ref, qkvb_ref, gate_ref = out_refs
    slab1_ref, slab2_ref = scratch_refs
    tm = x_ref.shape[0]
    x = x_ref[...]
    ms = jnp.mean(x * x, axis=-1, keepdims=True)
    h = (x * lax.rsqrt(ms + EPS) * ln_ref[...]).astype(_BF16)

    ang = freq_ref[...] * pos_ref[...].astype(_F32)
    packed = jnp.concatenate([jnp.cos(ang), jnp.sin(ang), jnp.ones_like(ang),
                              jnp.zeros((LANES - 3 * ROPE_HALF, tm), _F32)], axis=0).T
    lane = lax.broadcasted_iota(jnp.int32, (tm, LANES), 1) % HEAD_DIM
    rotary = lane < ROPE_DIM
    cos_t = jnp.take_along_axis(packed, jnp.where(rotary, lane % ROPE_HALF, 2 * ROPE_HALF), axis=1,
                                mode="promise_in_bounds")
    sin_t = jnp.take_along_axis(packed, jnp.where(rotary, ROPE_HALF + lane % ROPE_HALF, 3 * ROPE_HALF), axis=1,
                                mode="promise_in_bounds")
    first_half = lane < ROPE_HALF
    sin_from_hi = jnp.where(first_half, -sin_t, 0.0)
    sin_from_lo = jnp.where(first_half, 0.0, sin_t)
    seg = seg_ref[...]

    def norm_rope(y, gain):
        ss = jnp.dot((y * y).astype(_BF16), seg, preferred_element_type=_F32)
        yn = y * lax.rsqrt(ss * (1.0 / HEAD_DIM) + EPS) * gain
        up = pltpu.roll(yn, LANES - ROPE_HALF, 1)
        dn = pltpu.roll(yn, ROPE_HALF, 1)
        return yn * cos_t + up * sin_from_hi + dn * sin_from_lo

    def project(w_col):
        return jnp.dot(h, w_ref[:, w_col:w_col + CHUNK], preferred_element_type=_F32)

    tasks = []
    group_out = (qkv0_ref, slab1_ref, slab2_ref)

    def dilated_store(g, slab):
        def store(yh):
            if g == 0:
                qkv0_ref[:, slab * LANES:(slab + 1) * LANES] = yh.astype(_BF16)
            else:
                group_out[g][slab] = yh
        return store

    def row_store(out_ref, col):
        def store(yh):
            out_ref[:, col:col + LANES] = yh.astype(_BF16)
        return store

    def gate_store(col):
        def store(yh):
            gate_ref[:, col:col + LANES] = (0.5 * jnp.tanh(0.5 * yh) + 0.5).astype(_BF16)
        return store

    n_halves = CHUNK // LANES
    for part in range(3):
        for g in range(N_DIL_GROUPS):
            tasks.append((part * DIL_WIDTH + g * GROUP_WIDTH,
                          [(part if part < 2 else None, dilated_store(g, part * n_halves + i))
                           for i in range(n_halves)]))
    for c in range(QKV_B_WIDTH // CHUNK):
        cols = [c * CHUNK + i * LANES for i in range(n_halves)]
        tasks.append((QKV_A_WIDTH + c * CHUNK,
                      [(2 if col < SWA_Q_WIDTH else (3 if col < SWA_Q_WIDTH + SWA_KV_WIDTH else None),
                        row_store(qkvb_ref, col)) for col in cols]))
    for c in range(GATE_WIDTH // CHUNK):
        tasks.append((QKV_A_WIDTH + QKV_B_WIDTH + c * CHUNK,
                      [(None, gate_store(c * CHUNK + i * LANES)) for i in range(n_halves)]))

    def epilogue(y, stores):
        for i, (gain_idx, store) in enumerate(stores):
            yh = y[:, i * LANES:(i + 1) * LANES]
            store(yh if gain_idx is None else norm_rope(yh, gain_ref[gain_idx]))

    in_flight = []
    for w_col, stores in tasks:
        in_flight.append((project(w_col), stores))
        if len(in_flight) > PROJ_LOOKAHEAD:
            epilogue(*in_flight.pop(0))
    for y, stores in in_flight:
        epilogue(y, stores)

    for g, out_ref, slab_ref in ((1, qkv1_ref, slab1_ref), (2, qkv2_ref, slab2_ref)):
        d = DIL_PAIRS[g][1]
        for r in range(d):
            for slab in range(GROUP_QKV_WIDTH // LANES):
                rows = slab_ref[slab, pl.ds(r, tm // d, stride=d), :]
                out_ref[r, :, slab * LANES:(slab + 1) * LANES] = rows.astype(_BF16)


def _in_proj_component(x2, pos_rows, ln1, w_in, gains, freq, seg, batch, seq):
    tm = IN_PROJ_TILE
    n = batch * seq
    tiles_per_seq = seq // tm
    out_row = lambda width: pl.BlockSpec((tm, width), lambda i: (i, 0))

    def deinterleaved(d):
        return pl.BlockSpec((None, d, tm // d, GROUP_QKV_WIDTH),
                            lambda i: (i // tiles_per_seq, 0, i % tiles_per_seq, 0))

    d1, d2 = DIL_PAIRS[1][1], DIL_PAIRS[2][1]
    n_slabs = GROUP_QKV_WIDTH // LANES
    return _Component(
        inputs=[x2, pos_rows, ln1, w_in, gains, freq, seg],
        in_specs=[out_row(D_MODEL), pl.BlockSpec((None, 1, tm), lambda i: (i, 0, 0)),
                  _resident((1, D_MODEL)), _resident(w_in.shape), _resident(gains.shape), _resident(freq.shape),
                  _resident(seg.shape)],
        out_specs=[out_row(GROUP_QKV_WIDTH), deinterleaved(d1), deinterleaved(d2), out_row(QKV_B_WIDTH),
                   out_row(GATE_WIDTH)],
        out_shapes=[jax.ShapeDtypeStruct((n, GROUP_QKV_WIDTH), _BF16),
                    jax.ShapeDtypeStruct((batch, d1, seq // d1, GROUP_QKV_WIDTH), _BF16),
                    jax.ShapeDtypeStruct((batch, d2, seq // d2, GROUP_QKV_WIDTH), _BF16),
                    jax.ShapeDtypeStruct((n, QKV_B_WIDTH), _BF16),
                    jax.ShapeDtypeStruct((n, GATE_WIDTH), _BF16)],
        scratch_shapes=[pltpu.VMEM((n_slabs, tm, LANES), _F32), pltpu.VMEM((n_slabs, tm, LANES), _F32)],
        body=_in_proj_body)


_NT = (((1,), (1,)), ((), ()))
_STACK = GROUP_HEADS * BLOCK


def _stacked_scores(q, k_win):
    head_of_lane = lax.broadcasted_iota(jnp.int32, q.shape, 1) // HEAD_DIM
    zero = jnp.zeros_like(q)
    q_stack = jnp.concatenate([jnp.where(head_of_lane == hd, q, zero) for hd in range(GROUP_HEADS)], axis=0)
    return lax.dot_general(q_stack, k_win, _NT, preferred_element_type=_F32)


def _band_bias(max_dist, first_key_col):
    row = lax.broadcasted_iota(jnp.int32, (_STACK, 2 * BLOCK), 0) % BLOCK
    col = lax.broadcasted_iota(jnp.int32, (_STACK, 2 * BLOCK), 1)
    dist = row + BLOCK - col
    return jnp.where((dist >= 0) & (dist <= max_dist) & (col >= first_key_col), 0.0, NEG).astype(_F32)


def _head_rows(hd):
    return slice(hd * BLOCK, (hd + 1) * BLOCK)


def _merge_heads(stack_col):
    low = lax.broadcasted_iota(jnp.int32, (BLOCK, LANES), 1) < HEAD_DIM
    return jnp.concatenate([jnp.where(low, stack_col(2 * pair, pair), stack_col(2 * pair + 1, pair))
                            for pair in range(GROUP_HEADS // 2)], axis=1)


def _attend(s, bias, v_win):
    s = s + bias
    m = jnp.max(s, axis=1, keepdims=True)
    p = jnp.exp2(s - m).astype(_BF16)
    pair_rows = 2 * BLOCK
    ones = jnp.ones((v_win.shape[0], LANES), _BF16)
    pv = [jnp.dot(p[pair * pair_rows:(pair + 1) * pair_rows],
                  jnp.concatenate([v_win[:, pair * LANES:(pair + 1) * LANES], ones], axis=1),
                  preferred_element_type=_F32) for pair in range(GROUP_HEADS // 2)]
    head_block = lambda hd: slice((hd % 2) * BLOCK, (hd % 2 + 1) * BLOCK)
    denom = _merge_heads(lambda hd, pair: pv[pair][head_block(hd), LANES:])
    o = _merge_heads(lambda hd, pair: pv[pair][head_block(hd), :LANES]) / denom
    m = _merge_heads(lambda hd, pair: jnp.broadcast_to(m[_head_rows(hd)], (BLOCK, LANES)))
    return o, m + jnp.log2(denom)


def _window(qb, prev_ref, cur_ref, seq=None):
    idx = (lambda rows: (rows, slice(None))) if seq is None else (lambda rows: (seq, rows, slice(None)))
    rows = slice(qb * BLOCK, (qb + 1) * BLOCK)
    if qb == 0:
        prev = prev_ref[idx(slice(0, BLOCK))]
    else:
        prev = cur_ref[idx(slice((qb - 1) * BLOCK, qb * BLOCK))]
    return jnp.concatenate([prev, cur_ref[idx(rows)]], axis=0)


def _dilated_body(in_refs, out_refs, scratch_refs, step, *, max_dist, tiles_per_seq):
    del scratch_refs
    q_ref, kprev_ref, k_ref, vprev_ref, v_ref = in_refs
    o_ref, lse_ref = out_refs
    n_seqs, tile, _ = q_ref.shape
    first_tile = (step % tiles_per_seq) == 0
    bias_first = _band_bias(max_dist, jnp.where(first_tile, BLOCK, 0))
    bias_inner = _band_bias(max_dist, 0)
    for seq in range(n_seqs):
        for qb in range(tile // BLOCK):
            rows = slice(qb * BLOCK, (qb + 1) * BLOCK)
            s = _stacked_scores(q_ref[seq, rows, :], _window(qb, kprev_ref, k_ref, seq))
            o, lse = _attend(s, bias_first if qb == 0 else bias_inner, _window(qb, vprev_ref, v_ref, seq))
            o_ref[seq, rows, :] = o.astype(o_ref.dtype)
            lse_ref[seq, rows, :] = lse


def _dilated_component(qkv, window, dilation):
    n_seqs, length, _ = qkv.shape
    tile = min(ATTN_TILE, length)
    seqs = ATTN_TILE // tile
    n_blocks = tile // BLOCK
    tiles_per_seq = length // tile

    def cur(col):
        return pl.BlockSpec((seqs, tile, GROUP_WIDTH), lambda i: (i // tiles_per_seq, i % tiles_per_seq, col))

    def prev(col):
        return pl.BlockSpec((seqs, BLOCK, GROUP_WIDTH),
                            lambda i: (i // tiles_per_seq, jnp.maximum((i % tiles_per_seq) * n_blocks - 1, 0), col))

    out_spec = pl.BlockSpec((seqs, tile, GROUP_WIDTH), lambda i: (i // tiles_per_seq, i % tiles_per_seq, 0))
    return _Component(
        inputs=[qkv] * 5,
        in_specs=[cur(0), prev(1), cur(1), prev(2), cur(2)],
        out_specs=[out_spec, out_spec],
        out_shapes=[jax.ShapeDtypeStruct((n_seqs, length, GROUP_WIDTH), _BF16),
                    jax.ShapeDtypeStruct((n_seqs, length, GROUP_WIDTH), _F32)],
        scratch_shapes=[],
        body=functools.partial(_dilated_body, max_dist=window // dilation, tiles_per_seq=tiles_per_seq))


def _swa_body(in_refs, out_refs, scratch_refs, step, *, tiles_per_seq):
    sink_ref, q_ref, kprev_ref, k_ref, vprev_ref, v_ref = in_refs
    (o_ref,) = out_refs
    kdup_ref, vdup_ref = scratch_refs
    tile = q_ref.shape[0]
    first_tile = (step % tiles_per_seq) == 0
    n_kv_heads = SWA_KV_WIDTH // HEAD_DIM

    def duplicate(dst_ref, row0, t):
        low = lax.broadcasted_iota(jnp.int32, t.shape, 1) < HEAD_DIM
        t32 = t.astype(_F32)
        swapped = pltpu.roll(t32, HEAD_DIM, 1)
        dst_ref[0, row0:row0 + t.shape[0], :] = jnp.where(low, t32, swapped).astype(_BF16)
        dst_ref[1, row0:row0 + t.shape[0], :] = jnp.where(low, swapped, t32).astype(_BF16)

    for dst_ref, prev_ref, cur_ref in ((kdup_ref, kprev_ref, k_ref), (vdup_ref, vprev_ref, v_ref)):
        duplicate(dst_ref, 0, prev_ref[...])
        duplicate(dst_ref, BLOCK, cur_ref[...])

    low = lax.broadcasted_iota(jnp.int32, (BLOCK, LANES), 1) < HEAD_DIM

    def scores(qb, kv):
        rows = slice(qb * BLOCK, (qb + 1) * BLOCK)
        parts = []
        for pair in range(GROUP_HEADS // 2):
            lane0 = kv * GROUP_WIDTH + pair * LANES
            q_pair = q_ref[rows, lane0:lane0 + LANES]
            zero = jnp.zeros_like(q_pair)
            parts += [jnp.where(low, q_pair, zero), jnp.where(low, zero, q_pair)]
        q_stack = jnp.concatenate(parts, axis=0)
        k_win = kdup_ref[kv, qb * BLOCK:(qb + 2) * BLOCK, :]
        return lax.dot_general(q_stack, k_win, _NT, preferred_element_type=_F32)

    sink_col = lax.broadcasted_iota(jnp.int32, (_STACK, 2 * BLOCK), 1) == 0
    bias_first = jnp.where(sink_col, 0.0, _band_bias(SWA_WINDOW - 1, jnp.where(first_tile, BLOCK, 0)))
    bias_inner = jnp.where(sink_col, 0.0, _band_bias(SWA_WINDOW - 1, 0))
    sink_lane = lax.broadcasted_iota(jnp.int32, (BLOCK, LANES), 1) == 0
    sink_row = lax.broadcasted_iota(jnp.int32, (2 * BLOCK, LANES), 0) == 0

    def finish(qb, kv, s):
        rows = slice(qb * BLOCK, (qb + 1) * BLOCK)
        s_left = jnp.concatenate([jnp.where(sink_lane, sink_ref[kv * GROUP_HEADS + hd], s[_head_rows(hd), :LANES])
                                  for hd in range(GROUP_HEADS)], axis=0)
        s = jnp.concatenate([s_left, s[:, LANES:]], axis=1) + (bias_first if qb == 0 else bias_inner)
        p = jnp.exp2(s - jnp.max(s, axis=1, keepdims=True)).astype(_BF16)
        v_win = jnp.where(sink_row, jnp.zeros((), _BF16), vdup_ref[kv, qb * BLOCK:(qb + 2) * BLOCK, :])
        pv = jnp.dot(p, jnp.concatenate([v_win, jnp.ones_like(v_win)], axis=1),
                     preferred_element_type=_F32)
        for pair in range(GROUP_HEADS // 2):
            lane0 = kv * GROUP_WIDTH + pair * LANES
            lo, hi = pv[_head_rows(2 * pair)], pv[_head_rows(2 * pair + 1)]
            o_pair = jnp.where(low, lo[:, :LANES], hi[:, :LANES]) / jnp.where(low, lo[:, LANES:], hi[:, LANES:])
            o_ref[rows, lane0:lane0 + LANES] = o_pair.astype(o_ref.dtype)

    for qb in range(tile // BLOCK):
        for kv in range(n_kv_heads):
            finish(qb, kv, scores(qb, kv))


def _swa_component(qkvb3, sinks):
    b, s, _ = qkvb3.shape
    tile = ATTN_TILE
    n_blocks = tile // BLOCK
    tiles_per_seq = s // tile
    k_col = SWA_Q_WIDTH // LANES
    v_col = k_col + SWA_KV_WIDTH // LANES

    def cur(col):
        return pl.BlockSpec((None, tile, LANES), lambda i: (i // tiles_per_seq, i % tiles_per_seq, col))

    def prev(col):
        return pl.BlockSpec((None, BLOCK, LANES),
                            lambda i: (i // tiles_per_seq, jnp.maximum((i % tiles_per_seq) * n_blocks - 1, 0), col))

    q_spec = pl.BlockSpec((None, tile, SWA_Q_WIDTH), lambda i: (i // tiles_per_seq, i % tiles_per_seq, 0))
    return _Component(
        inputs=[sinks] + [qkvb3] * 5,
        in_specs=[pl.BlockSpec(memory_space=pltpu.SMEM), q_spec, prev(k_col), cur(k_col), prev(v_col), cur(v_col)],
        out_specs=[q_spec],
        out_shapes=[jax.ShapeDtypeStruct((b, s, SWA_Q_WIDTH), _BF16)],
        scratch_shapes=[pltpu.VMEM((SWA_KV_WIDTH // HEAD_DIM, BLOCK + tile, LANES), _BF16)] * 2,
        body=functools.partial(_swa_body, tiles_per_seq=tiles_per_seq))


def _cast_body(in_refs, out_refs, scratch_refs, step):
    del scratch_refs, step
    for src_ref, dst_ref in zip(in_refs, out_refs):
        dst_ref[...] = src_ref[...].astype(dst_ref.dtype)


def _cast_component(weights, n_steps):
    specs = []
    for w in weights:
        rows, cols = w.shape
        block_rows = max(BF16_SUBLANES, rows // n_steps)
        assert rows % block_rows == 0 and rows // block_rows <= n_steps
        last = rows // block_rows - 1
        specs.append(pl.BlockSpec((block_rows, cols), lambda i, last=last: (jnp.minimum(i, last), 0)))
    return _Component(inputs=weights, in_specs=specs, out_specs=specs,
                      out_shapes=[jax.ShapeDtypeStruct(w.shape, _BF16) for w in weights],
                      scratch_shapes=[], body=_cast_body)


def _attention_components(proj, batch, seq, sinks2):
    qkv0, qkv1, qkv2, qkvb, _ = proj
    comps = []
    for (window, d), qkv in zip(DIL_PAIRS, (qkv0, qkv1, qkv2)):
        comps.append(_dilated_component(qkv.reshape(batch * d, seq // d, GROUP_QKV_WIDTH), window, d))
    comps.append(_swa_component(qkvb.reshape(batch, seq, QKV_B_WIDTH), sinks2))
    return comps


def _attention_outputs(outs, batch, seq):
    attn_a = []
    for (_, d), (o, lse) in zip(DIL_PAIRS, outs[:3]):
        shape = (batch * seq, GROUP_WIDTH) if d == 1 else (batch, d, seq // d, GROUP_WIDTH)
        attn_a.append((o.reshape(shape), lse.reshape(shape)))
    return attn_a, outs[3][0].reshape(batch * seq, SWA_Q_WIDTH)


def _out_mlp_body(in_refs, out_refs, scratch_refs, step):
    del step
    (x_ref, o0_ref, l0_ref, o1_ref, l1_ref, o2_ref, l2_ref, ob_ref, gate_ref,
     wa_ref, wb_ref, wo_ref, ln_ref, wup_ref, wdn_ref) = in_refs
    (out_ref,) = out_refs
    (slab_ref,) = scratch_refs
    tm = x_ref.shape[0]
    n_slabs = GROUP_WIDTH // LANES

    def interleaved(src_ref, base):
        d = src_ref.shape[0]
        for r in range(d):
            for slab in range(n_slabs):
                slab_ref[base + slab, pl.ds(r, tm // d, stride=d), :] = (
                    src_ref[r, :, slab * LANES:(slab + 1) * LANES].astype(_F32))
        return jnp.concatenate([slab_ref[base + slab] for slab in range(n_slabs)], axis=1)

    o0, l0 = o0_ref[...].astype(_F32), l0_ref[...]
    o1, l1 = interleaved(o1_ref, 0), interleaved(l1_ref, n_slabs)
    o2, l2 = interleaved(o2_ref, 2 * n_slabs), interleaved(l2_ref, 3 * n_slabs)
    m = jnp.maximum(jnp.maximum(l0, l1), l2)
    e0, e1, e2 = jnp.exp2(l0 - m), jnp.exp2(l1 - m), jnp.exp2(l2 - m)
    oa = ((e0 * o0 + e1 * o1 + e2 * o2) / (e0 + e1 + e2)).astype(_BF16)

    ya = jnp.dot(oa, wa_ref[...], preferred_element_type=_F32)
    yb = jnp.dot(ob_ref[...], wb_ref[...], preferred_element_type=_F32)
    gate_a = gate_ref[:, :D_MODEL].astype(_F32)
    gate_b = gate_ref[:, D_MODEL:].astype(_F32)
    mix = (gate_a * ya + gate_b * yb).astype(_BF16)
    x1 = x_ref[...] + jnp.dot(mix, wo_ref[...], preferred_element_type=_F32)

    ms = jnp.mean(x1 * x1, axis=-1, keepdims=True)
    h2 = (x1 * lax.rsqrt(ms + EPS) * ln_ref[...]).astype(_BF16)
    acc = x1
    for c in range(D_FF // FF_CHUNK):
        u = jnp.dot(h2, wup_ref[:, c * FF_CHUNK:(c + 1) * FF_CHUNK], preferred_element_type=_F32)
        a = jnp.square(jnp.maximum(u, 0.0)).astype(_BF16)
        acc = acc + jnp.dot(a, wdn_ref[c * FF_CHUNK:(c + 1) * FF_CHUNK, :], preferred_element_type=_F32)
    out_ref[...] = acc


def _out_mlp_component(x2, attn_a, ob, gates, wa, wb, wo, ln2, wup, wdn, seq):
    tm = TOKEN_TILE
    tiles_per_seq = seq // tm
    row = lambda width: pl.BlockSpec((tm, width), lambda i: (i, 0))
    full_row = row(D_MODEL)

    def deinterleaved(d):
        return pl.BlockSpec((None, d, tm // d, GROUP_WIDTH), lambda i: (i // tiles_per_seq, 0, i % tiles_per_seq, 0))

    group_specs = [row(GROUP_WIDTH)] * 2
    for _, d in DIL_PAIRS[1:]:
        group_specs += [deinterleaved(d)] * 2
    return _Component(
        inputs=[x2] + [a for pair in attn_a for a in pair] + [ob, gates, wa, wb, wo, ln2, wup, wdn],
        in_specs=[full_row] + group_specs + [row(SWA_Q_WIDTH), row(GATE_WIDTH),
                  _resident(wa.shape), _resident(wb.shape), _resident(wo.shape), _resident((1, D_MODEL)),
                  _resident(wup.shape), _resident(wdn.shape)],
        out_specs=[full_row],
        out_shapes=[jax.ShapeDtypeStruct(x2.shape, _F32)],
        scratch_shapes=[pltpu.VMEM((4 * GROUP_WIDTH // LANES, tm, LANES), _F32)],
        body=_out_mlp_body)


def _rope_inv_freq():
    return (ROPE_THETA ** (-jnp.arange(0, ROPE_DIM, 2, dtype=_F32) / ROPE_DIM))[:, None]


def kernel(x, positions, ln1_g, w_in, q_norm_a, k_norm_a, q_norm_b, k_norm_b, sinks,
           w_branch_a, w_branch_b, w_out, ln2_g, w_up, w_down):
    b, s, d_model = x.shape
    assert d_model == D_MODEL and ln1_g.shape[0] == 1 and s % (max(d for _, d in DIL_PAIRS) * BLOCK) == 0
    assert s % TOKEN_TILE == 0 and s % IN_PROJ_TILE == 0 and s % ATTN_TILE == 0
    n_tiles = b * s // TOKEN_TILE
    attn_steps = b * s // ATTN_TILE
    x2 = x.reshape(b * s, d_model)
    log2e = np.float32(np.log2(np.e))
    scale = np.float32(log2e / np.sqrt(HEAD_DIM))
    gains = jnp.stack([jnp.tile(q_norm_a[0] * scale, 2), jnp.tile(k_norm_a[0], 2),
                       jnp.tile(q_norm_b[0] * scale, 2), jnp.tile(k_norm_b[0], 2)])[:, None, :]
    head = np.arange(LANES) // HEAD_DIM
    seg = jnp.asarray(head[:, None] == head[None, :], _BF16)
    pos_rows = positions.reshape(b * s // IN_PROJ_TILE, 1, IN_PROJ_TILE)
    sinks2 = sinks[0] * log2e

    (proj,) = _run_components(
        [_in_proj_component(x2, pos_rows, ln1_g, w_in[0].astype(_BF16), gains, _rope_inv_freq(), seg, b, s)],
        b * s // IN_PROJ_TILE, "in_proj")
    mlp_weights = [w_branch_a[0], w_branch_b[0], w_out[0], w_up[0], w_down[0]]
    *attn_outs, mlp_weights = _run_components(
        _attention_components(proj, b, s, sinks2) + [_cast_component(mlp_weights, attn_steps)], attn_steps,
        "attention")
    attn_a, ob = _attention_outputs(attn_outs, b, s)
    ((out,),) = _run_components(
        [_out_mlp_component(x2, attn_a, ob, proj[4], *mlp_weights[:3], ln2_g, *mlp_weights[3:], s)],
        n_tiles, "out_mlp")
    return out.reshape(b, s, d_model)
```

```python
import functools

import numpy as np
import jax
import jax.numpy as jnp
from jax import lax
from jax.experimental import pallas as pl
from jax.experimental.pallas import tpu as pltpu

D_MODEL = 1024
HEAD_DIM = 64
DIL_PAIRS = ((128, 1), (512, 4), (2048, 16))
GROUP_HEADS = 4
GROUP_WIDTH = GROUP_HEADS * HEAD_DIM
DIL_WIDTH = 768
N_DIL_GROUPS = DIL_WIDTH // GROUP_WIDTH
GROUP_QKV_WIDTH = 3 * GROUP_WIDTH
SWA_WINDOW = 128
SWA_Q_WIDTH = 512
SWA_KV_WIDTH = 128
QKV_A_WIDTH = 3 * DIL_WIDTH
QKV_B_WIDTH = SWA_Q_WIDTH + 2 * SWA_KV_WIDTH
GATE_WIDTH = 2 * D_MODEL
D_FF = 4 * D_MODEL
ROPE_THETA = 500000.0
ROPE_DIM = HEAD_DIM // 4
ROPE_HALF = ROPE_DIM // 2
BLOCK = 128
EPS = 1e-6
NEG = -1e30

LANES = 128
BF16_SUBLANES = 16
CHUNK = 256
FF_CHUNK = 512
TOKEN_TILE = 512
ATTN_TILE = 1024
IN_PROJ_TILE = 1024
PROJ_LOOKAHEAD = 2
IN_PROJ_VMEM_BYTES = 56 * 1024 * 1024
ATTN_VMEM_BYTES = 42 * 1024 * 1024
OUT_MLP_VMEM_BYTES = 48 * 1024 * 1024

_BF16 = jnp.bfloat16
_F32 = jnp.float32


def _resident(shape):
    return pl.BlockSpec(shape, lambda *_: (0,) * len(shape), pipeline_mode=pl.Buffered(1))


class _Component:
    def __init__(self, inputs, in_specs, out_specs, out_shapes, scratch_shapes, body):
        self.inputs, self.in_specs, self.out_specs = list(inputs), list(in_specs), list(out_specs)
        self.out_shapes, self.scratch_shapes, self.body = list(out_shapes), list(scratch_shapes), body


def _run_components(components, n_steps, name, vmem_limit_bytes):
    n_in = [len(c.inputs) for c in components]
    n_out = [len(c.out_specs) for c in components]
    n_scr = [len(c.scratch_shapes) for c in components]

    def body(*refs):
        refs = list(refs)
        ins, outs, scrs = [], [], []
        pos = 0
        for group, counts in ((ins, n_in), (outs, n_out), (scrs, n_scr)):
            for k in counts:
                group.append(refs[pos:pos + k])
                pos += k
        step = pl.program_id(0)
        for c, i, o, s in zip(components, ins, outs, scrs):
            c.body(i, o, s, step)

    outs = pl.pallas_call(
        body,
        grid=(n_steps,),
        in_specs=[s for c in components for s in c.in_specs],
        out_specs=[s for c in components for s in c.out_specs],
        out_shape=[s for c in components for s in c.out_shapes],
        scratch_shapes=[s for c in components for s in c.scratch_shapes],
        compiler_params=pltpu.CompilerParams(dimension_semantics=("arbitrary",), vmem_limit_bytes=vmem_limit_bytes),
        name=name,
    )(*[a for c in components for a in c.inputs])
    split, pos = [], 0
    for k in n_out:
        split.append(outs[pos:pos + k])
        pos += k
    return split


def _in_proj_body(in_refs, out_refs, scratch_refs, step):
    del step
    x_ref, pos_ref, ln_ref, w_ref, gain_ref, freq_ref, seg_ref = in_refs
    qkv0_ref, qkv1_ref, qkv2_ref, qkvb_ref, gate_ref = out_refs
    slab1_ref, slab2_ref = scratch_refs
    tm = x_ref.shape[0]
    x = x_ref[...]
    ms = jnp.mean(x * x, axis=-1, keepdims=True)
    h = (x * lax.rsqrt(ms + EPS) * ln_ref[...]).astype(_BF16)

    ang = freq_ref[...] * pos_ref[...].astype(_F32)
    packed = jnp.concatenate([jnp.cos(ang), jnp.sin(ang), jnp.ones_like(ang),
                              jnp.zeros((LANES - 3 * ROPE_HALF, tm), _F32)], axis=0).T
    lane = lax.broadcasted_iota(jnp.int32, (tm, LANES), 1) % HEAD_DIM
    rotary = lane < ROPE_DIM
    cos_t = jnp.take_along_axis(packed, jnp.where(rotary, lane % ROPE_HALF, 2 * ROPE_HALF), axis=1,
                                mode="promise_in_bounds")
    sin_t = jnp.take_along_axis(packed, jnp.where(rotary, ROPE_HALF + lane % ROPE_HALF, 3 * ROPE_HALF), axis=1,
                                mode="promise_in_bounds")
    first_half = lane < ROPE_HALF
    sin_from_hi = jnp.where(first_half, -sin_t, 0.0)
    sin_from_lo = jnp.where(first_half, 0.0, sin_t)
    seg = seg_ref[...]

    def norm_rope(y, gain):
        ss = jnp.dot((y * y).astype(_BF16), seg, preferred_element_type=_F32)
        yn = y * lax.rsqrt(ss * (1.0 / HEAD_DIM) + EPS) * gain
        up = pltpu.roll(yn, LANES - ROPE_HALF, 1)
        dn = pltpu.roll(yn, ROPE_HALF, 1)
        return yn * cos_t + up * sin_from_hi + dn * sin_from_lo

    def project(w_col):
        return jnp.dot(h, w_ref[:, w_col:w_col + CHUNK], preferred_element_type=_F32)

    tasks = []
    group_out = (qkv0_ref, slab1_ref, slab2_ref)

    def dilated_store(g, slab):
        def store(yh):
            if g == 0:
                qkv0_ref[:, slab * LANES:(slab + 1) * LANES] = yh.astype(_BF16)
            else:
                group_out[g][slab] = yh
        return store

    def row_store(out_ref, col):
        def store(yh):
            out_ref[:, col:col + LANES] = yh.astype(_BF16)
        return store

    def gate_store(col):
        def store(yh):
            gate_ref[:, col:col + LANES] = (0.5 * jnp.tanh(0.5 * yh) + 0.5).astype(_BF16)
        return store

    n_halves = CHUNK // LANES
    for part in range(3):
        for g in range(N_DIL_GROUPS):
            tasks.append((part * DIL_WIDTH + g * GROUP_WIDTH,
                          [(part if part < 2 else None, dilated_store(g, part * n_halves + i))
                           for i in range(n_halves)]))
    for c in range(QKV_B_WIDTH // CHUNK):
        cols = [c * CHUNK + i * LANES for i in range(n_halves)]
        tasks.append((QKV_A_WIDTH + c * CHUNK,
                      [(2 if col < SWA_Q_WIDTH else (3 if col < SWA_Q_WIDTH + SWA_KV_WIDTH else None),
                        row_store(qkvb_ref, col)) for col in cols]))
    for c in range(GATE_WIDTH // CHUNK):
        tasks.append((QKV_A_WIDTH + QKV_B_WIDTH + c * CHUNK,
                      [(None, gate_store(c * CHUNK + i * LANES)) for i in range(n_halves)]))

    def epilogue(y, stores):
        for i, (gain_idx, store) in enumerate(stores):
            yh = y[:, i * LANES:(i + 1) * LANES]
            store(yh if gain_idx is None else norm_rope(yh, gain_ref[gain_idx]))

    in_flight = []
    for w_col, stores in tasks:
        in_flight.append((project(w_col), stores))
        if len(in_flight) > PROJ_LOOKAHEAD:
            epilogue(*in_flight.pop(0))
    for y, stores in in_flight:
        epilogue(y, stores)

    for g, out_ref, slab_ref in ((1, qkv1_ref, slab1_ref), (2, qkv2_ref, slab2_ref)):
        d = DIL_PAIRS[g][1]
        for r in range(d):
            for slab in range(GROUP_QKV_WIDTH // LANES):
                rows = slab_ref[slab, pl.ds(r, tm // d, stride=d), :]
                out_ref[r, :, slab * LANES:(slab + 1) * LANES] = rows.astype(_BF16)


def _in_proj_component(x2, pos_rows, ln1, w_in, gains, freq, seg, batch, seq):
    tm = IN_PROJ_TILE
    n = batch * seq
    tiles_per_seq = seq // tm
    out_row = lambda width: pl.BlockSpec((tm, width), lambda i: (i, 0))

    def deinterleaved(d):
        return pl.BlockSpec((None, d, tm // d, GROUP_QKV_WIDTH),
                            lambda i: (i // tiles_per_seq, 0, i % tiles_per_seq, 0))

    d1, d2 = DIL_PAIRS[1][1], DIL_PAIRS[2][1]
    n_slabs = GROUP_QKV_WIDTH // LANES
    return _Component(
        inputs=[x2, pos_rows, ln1, w_in, gains, freq, seg],
        in_specs=[out_row(D_MODEL), pl.BlockSpec((None, 1, tm), lambda i: (i, 0, 0)),
                  _resident((1, D_MODEL)), _resident(w_in.shape), _resident(gains.shape), _resident(freq.shape),
                  _resident(seg.shape)],
        out_specs=[out_row(GROUP_QKV_WIDTH), deinterleaved(d1), deinterleaved(d2), out_row(QKV_B_WIDTH),
                   out_row(GATE_WIDTH)],
        out_shapes=[jax.ShapeDtypeStruct((n, GROUP_QKV_WIDTH), _BF16),
                    jax.ShapeDtypeStruct((batch, d1, seq // d1, GROUP_QKV_WIDTH), _BF16),
                    jax.ShapeDtypeStruct((batch, d2, seq // d2, GROUP_QKV_WIDTH), _BF16),
                    jax.ShapeDtypeStruct((n, QKV_B_WIDTH), _BF16),
                    jax.ShapeDtypeStruct((n, GATE_WIDTH), _BF16)],
        scratch_shapes=[pltpu.VMEM((n_slabs, tm, LANES), _F32), pltpu.VMEM((n_slabs, tm, LANES), _F32)],
        body=_in_proj_body)


_NT = (((1,), (1,)), ((), ()))
_STACK = GROUP_HEADS * BLOCK


def _stacked_scores(q, k_win):
    head_of_lane = lax.broadcasted_iota(jnp.int32, q.shape, 1) // HEAD_DIM
    zero = jnp.zeros_like(q)
    q_stack = jnp.concatenate([jnp.where(head_of_lane == hd, q, zero) for hd in range(GROUP_HEADS)], axis=0)
    return lax.dot_general(q_stack, k_win, _NT, preferred_element_type=_F32)


def _band_bias(max_dist, first_key_col):
    row = lax.broadcasted_iota(jnp.int32, (_STACK, 2 * BLOCK), 0) % BLOCK
    col = lax.broadcasted_iota(jnp.int32, (_STACK, 2 * BLOCK), 1)
    dist = row + BLOCK - col
    return jnp.where((dist >= 0) & (dist <= max_dist) & (col >= first_key_col), 0.0, NEG).astype(_F32)


def _head_rows(hd):
    return slice(hd * BLOCK, (hd + 1) * BLOCK)


def _merge_heads(stack_col):
    low = lax.broadcasted_iota(jnp.int32, (BLOCK, LANES), 1) < HEAD_DIM
    return jnp.concatenate([jnp.where(low, stack_col(2 * pair, pair), stack_col(2 * pair + 1, pair))
                            for pair in range(GROUP_HEADS // 2)], axis=1)


def _attend(s, bias, v_win):
    s = s + bias
    m = jnp.max(s, axis=1, keepdims=True)
    p = jnp.exp2(s - m).astype(_BF16)
    pair_rows = 2 * BLOCK
    ones = jnp.ones((v_win.shape[0], LANES), _BF16)
    pv = [jnp.dot(p[pair * pair_rows:(pair + 1) * pair_rows],
                  jnp.concatenate([v_win[:, pair * LANES:(pair + 1) * LANES], ones], axis=1),
                  preferred_element_type=_F32) for pair in range(GROUP_HEADS // 2)]
    head_block = lambda hd: slice((hd % 2) * BLOCK, (hd % 2 + 1) * BLOCK)
    denom = _merge_heads(lambda hd, pair: pv[pair][head_block(hd), LANES:])
    o = _merge_heads(lambda hd, pair: pv[pair][head_block(hd), :LANES]) / denom
    m = _merge_heads(lambda hd, pair: jnp.broadcast_to(m[_head_rows(hd)], (BLOCK, LANES)))
    return o, m + jnp.log2(denom)


def _window(qb, prev_ref, cur_ref, seq=None):
    idx = (lambda rows: (rows, slice(None))) if seq is None else (lambda rows: (seq, rows, slice(None)))
    rows = slice(qb * BLOCK, (qb + 1) * BLOCK)
    if qb == 0:
        prev = prev_ref[idx(slice(0, BLOCK))]
    else:
        prev = cur_ref[idx(slice((qb - 1) * BLOCK, qb * BLOCK))]
    return jnp.concatenate([prev, cur_ref[idx(rows)]], axis=0)


def _dilated_body(in_refs, out_refs, scratch_refs, step, *, max_dist, tiles_per_seq):
    del scratch_refs
    q_ref, kprev_ref, k_ref, vprev_ref, v_ref = in_refs
    o_ref, lse_ref = out_refs
    n_seqs, tile, _ = q_ref.shape
    first_tile = (step % tiles_per_seq) == 0
    bias_first = _band_bias(max_dist, jnp.where(first_tile, BLOCK, 0))
    bias_inner = _band_bias(max_dist, 0)
    for seq in range(n_seqs):
        for qb in range(tile // BLOCK):
            rows = slice(qb * BLOCK, (qb + 1) * BLOCK)
            s = _stacked_scores(q_ref[seq, rows, :], _window(qb, kprev_ref, k_ref, seq))
            o, lse = _attend(s, bias_first if qb == 0 else bias_inner, _window(qb, vprev_ref, v_ref, seq))
            o_ref[seq, rows, :] = o.astype(o_ref.dtype)
            lse_ref[seq, rows, :] = lse


def _dilated_component(qkv, window, dilation):
    n_seqs, length, _ = qkv.shape
    tile = min(ATTN_TILE, length)
    seqs = ATTN_TILE // tile
    n_blocks = tile // BLOCK
    tiles_per_seq = length // tile

    def cur(col):
        return pl.BlockSpec((seqs, tile, GROUP_WIDTH), lambda i: (i // tiles_per_seq, i % tiles_per_seq, col))

    def prev(col):
        return pl.BlockSpec((seqs, BLOCK, GROUP_WIDTH),
                            lambda i: (i // tiles_per_seq, jnp.maximum((i % tiles_per_seq) * n_blocks - 1, 0), col))

    out_spec = pl.BlockSpec((seqs, tile, GROUP_WIDTH), lambda i: (i // tiles_per_seq, i % tiles_per_seq, 0))
    return _Component(
        inputs=[qkv] * 5,
        in_specs=[cur(0), prev(1), cur(1), prev(2), cur(2)],
        out_specs=[out_spec, out_spec],
        out_shapes=[jax.ShapeDtypeStruct((n_seqs, length, GROUP_WIDTH), _BF16),
                    jax.ShapeDtypeStruct((n_seqs, length, GROUP_WIDTH), _F32)],
        scratch_shapes=[],
        body=functools.partial(_dilated_body, max_dist=window // dilation, tiles_per_seq=tiles_per_seq))


def _swa_body(in_refs, out_refs, scratch_refs, step, *, tiles_per_seq):
    sink_ref, q_ref, kprev_ref, k_ref, vprev_ref, v_ref = in_refs
    (o_ref,) = out_refs
    kdup_ref, vdup_ref = scratch_refs
    tile = q_ref.shape[0]
    first_tile = (step % tiles_per_seq) == 0
    n_kv_heads = SWA_KV_WIDTH // HEAD_DIM

    def duplicate(dst_ref, row0, t):
        low = lax.broadcasted_iota(jnp.int32, t.shape, 1) < HEAD_DIM
        t32 = t.astype(_F32)
        swapped = pltpu.roll(t32, HEAD_DIM, 1)
        dst_ref[0, row0:row0 + t.shape[0], :] = jnp.where(low, t32, swapped).astype(_BF16)
        dst_ref[1, row0:row0 + t.shape[0], :] = jnp.where(low, swapped, t32).astype(_BF16)

    for dst_ref, prev_ref, cur_ref in ((kdup_ref, kprev_ref, k_ref), (vdup_ref, vprev_ref, v_ref)):
        duplicate(dst_ref, 0, prev_ref[...])
        duplicate(dst_ref, BLOCK, cur_ref[...])

    low = lax.broadcasted_iota(jnp.int32, (BLOCK, LANES), 1) < HEAD_DIM

    def scores(qb, kv):
        rows = slice(qb * BLOCK, (qb + 1) * BLOCK)
        parts = []
        for pair in range(GROUP_HEADS // 2):
            lane0 = kv * GROUP_WIDTH + pair * LANES
            q_pair = q_ref[rows, lane0:lane0 + LANES]
            zero = jnp.zeros_like(q_pair)
            parts += [jnp.where(low, q_pair, zero), jnp.where(low, zero, q_pair)]
        q_stack = jnp.concatenate(parts, axis=0)
        k_win = kdup_ref[kv, qb * BLOCK:(qb + 2) * BLOCK, :]
        return lax.dot_general(q_stack, k_win, _NT, preferred_element_type=_F32)

    sink_col = lax.broadcasted_iota(jnp.int32, (_STACK, 2 * BLOCK), 1) == 0
    bias_first = jnp.where(sink_col, 0.0, _band_bias(SWA_WINDOW - 1, jnp.where(first_tile, BLOCK, 0)))
    bias_inner = jnp.where(sink_col, 0.0, _band_bias(SWA_WINDOW - 1, 0))
    sink_lane = lax.broadcasted_iota(jnp.int32, (BLOCK, LANES), 1) == 0
    sink_row = lax.broadcasted_iota(jnp.int32, (2 * BLOCK, LANES), 0) == 0

    def finish(qb, kv, s):
        rows = slice(qb * BLOCK, (qb + 1) * BLOCK)
        s_left = jnp.concatenate([jnp.where(sink_lane, sink_ref[kv * GROUP_HEADS + hd], s[_head_rows(hd), :LANES])
                                  for hd in range(GROUP_HEADS)], axis=0)
        s = jnp.concatenate([s_left, s[:, LANES:]], axis=1) + (bias_first if qb == 0 else bias_inner)
        p = jnp.exp2(s - jnp.max(s, axis=1, keepdims=True)).astype(_BF16)
        v_win = jnp.where(sink_row, jnp.zeros((), _BF16), vdup_ref[kv, qb * BLOCK:(qb + 2) * BLOCK, :])
        pv = jnp.dot(p, jnp.concatenate([v_win, jnp.ones_like(v_win)], axis=1),
                     preferred_element_type=_F32)
        for pair in range(GROUP_HEADS // 2):
            lane0 = kv * GROUP_WIDTH + pair * LANES
            lo, hi = pv[_head_rows(2 * pair)], pv[_head_rows(2 * pair + 1)]
            o_pair = jnp.where(low, lo[:, :LANES], hi[:, :LANES]) / jnp.where(low, lo[:, LANES:], hi[:, LANES:])
            o_ref[rows, lane0:lane0 + LANES] = o_pair.astype(o_ref.dtype)

    for qb in range(tile // BLOCK):
        for kv in range(n_kv_heads):
            finish(qb, kv, scores(qb, kv))


def _swa_component(qkvb3, sinks):
    b, s, _ = qkvb3.shape
    tile = ATTN_TILE
    n_blocks = tile // BLOCK
    tiles_per_seq = s // tile
    k_col = SWA_Q_WIDTH // LANES
    v_col = k_col + SWA_KV_WIDTH // LANES

    def cur(col):
        return pl.BlockSpec((None, tile, LANES), lambda i: (i // tiles_per_seq, i % tiles_per_seq, col))

    def prev(col):
        return pl.BlockSpec((None, BLOCK, LANES),
                            lambda i: (i // tiles_per_seq, jnp.maximum((i % tiles_per_seq) * n_blocks - 1, 0), col))

    q_spec = pl.BlockSpec((None, tile, SWA_Q_WIDTH), lambda i: (i // tiles_per_seq, i % tiles_per_seq, 0))
    return _Component(
        inputs=[sinks] + [qkvb3] * 5,
        in_specs=[pl.BlockSpec(memory_space=pltpu.SMEM), q_spec, prev(k_col), cur(k_col), prev(v_col), cur(v_col)],
        out_specs=[q_spec],
        out_shapes=[jax.ShapeDtypeStruct((b, s, SWA_Q_WIDTH), _BF16)],
        scratch_shapes=[pltpu.VMEM((SWA_KV_WIDTH // HEAD_DIM, BLOCK + tile, LANES), _BF16)] * 2,
        body=functools.partial(_swa_body, tiles_per_seq=tiles_per_seq))


def _cast_body(in_refs, out_refs, scratch_refs, step):
    del scratch_refs, step
    for src_ref, dst_ref in zip(in_refs, out_refs):
        dst_ref[...] = src_ref[...].astype(dst_ref.dtype)


def _cast_component(weights, n_steps):
    specs = []
    for w in weights:
        rows, cols = w.shape
        block_rows = max(BF16_SUBLANES, rows // n_steps)
        assert rows % block_rows == 0 and rows // block_rows <= n_steps
        last = rows // block_rows - 1
        specs.append(pl.BlockSpec((block_rows, cols), lambda i, last=last: (jnp.minimum(i, last), 0)))
    return _Component(inputs=weights, in_specs=specs, out_specs=specs,
                      out_shapes=[jax.ShapeDtypeStruct(w.shape, _BF16) for w in weights],
                      scratch_shapes=[], body=_cast_body)


def _attention_components(proj, batch, seq, sinks2):
    qkv0, qkv1, qkv2, qkvb, _ = proj
    comps = []
    for (window, d), qkv in zip(DIL_PAIRS, (qkv0, qkv1, qkv2)):
        comps.append(_dilated_component(qkv.reshape(batch * d, seq // d, GROUP_QKV_WIDTH), window, d))
    comps.append(_swa_component(qkvb.reshape(batch, seq, QKV_B_WIDTH), sinks2))
    return comps


def _attention_outputs(outs, batch, seq):
    attn_a = []
    for (_, d), (o, lse) in zip(DIL_PAIRS, outs[:3]):
        shape = (batch * seq, GROUP_WIDTH) if d == 1 else (batch, d, seq // d, GROUP_WIDTH)
        attn_a.append((o.reshape(shape), lse.reshape(shape)))
    return attn_a, outs[3][0].reshape(batch * seq, SWA_Q_WIDTH)


def _out_mlp_body(in_refs, out_refs, scratch_refs, step):
    del step
    (x_ref, o0_ref, l0_ref, o1_ref, l1_ref, o2_ref, l2_ref, ob_ref, gate_ref,
     wa_ref, wb_ref, wo_ref, ln_ref, wup_ref, wdn_ref) = in_refs
    (out_ref,) = out_refs
    (slab_ref,) = scratch_refs
    tm = x_ref.shape[0]
    n_slabs = GROUP_WIDTH // LANES

    def interleaved(src_ref, base):
        d = src_ref.shape[0]
        for r in range(d):
            for slab in range(n_slabs):
                slab_ref[base + slab, pl.ds(r, tm // d, stride=d), :] = (
                    src_ref[r, :, slab * LANES:(slab + 1) * LANES].astype(_F32))
        return jnp.concatenate([slab_ref[base + slab] for slab in range(n_slabs)], axis=1)

    o0, l0 = o0_ref[...].astype(_F32), l0_ref[...]
    o1, l1 = interleaved(o1_ref, 0), interleaved(l1_ref, n_slabs)
    o2, l2 = interleaved(o2_ref, 2 * n_slabs), interleaved(l2_ref, 3 * n_slabs)
    m = jnp.maximum(jnp.maximum(l0, l1), l2)
    e0, e1, e2 = jnp.exp2(l0 - m), jnp.exp2(l1 - m), jnp.exp2(l2 - m)
    oa = ((e0 * o0 + e1 * o1 + e2 * o2) / (e0 + e1 + e2)).astype(_BF16)

    ya = jnp.dot(oa, wa_ref[...], preferred_element_type=_F32)
    yb = jnp.dot(ob_ref[...], wb_ref[...], preferred_element_type=_F32)
    gate_a = gate_ref[:, :D_MODEL].astype(_F32)
    gate_b = gate_ref[:, D_MODEL:].astype(_F32)
    mix = (gate_a * ya + gate_b * yb).astype(_BF16)
    x1 = x_ref[...] + jnp.dot(mix, wo_ref[...], preferred_element_type=_F32)

    ms = jnp.mean(x1 * x1, axis=-1, keepdims=True)
    h2 = (x1 * lax.rsqrt(ms + EPS) * ln_ref[...]).astype(_BF16)
    acc = x1
    for c in range(D_FF // FF_CHUNK):
        u = jnp.dot(h2, wup_ref[:, c * FF_CHUNK:(c + 1) * FF_CHUNK], preferred_element_type=_F32)
        a = jnp.square(jnp.maximum(u, 0.0)).astype(_BF16)
        acc = acc + jnp.dot(a, wdn_ref[c * FF_CHUNK:(c + 1) * FF_CHUNK, :], preferred_element_type=_F32)
    out_ref[...] = acc


def _out_mlp_component(x2, attn_a, ob, gates, wa, wb, wo, ln2, wup, wdn, seq):
    tm = TOKEN_TILE
    tiles_per_seq = seq // tm
    row = lambda width: pl.BlockSpec((tm, width), lambda i: (i, 0))
    full_row = row(D_MODEL)

    def deinterleaved(d):
        return pl.BlockSpec((None, d, tm // d, GROUP_WIDTH), lambda i: (i // tiles_per_seq, 0, i % tiles_per_seq, 0))

    group_specs = [row(GROUP_WIDTH)] * 2
    for _, d in DIL_PAIRS[1:]:
        group_specs += [deinterleaved(d)] * 2
    return _Component(
        inputs=[x2] + [a for pair in attn_a for a in pair] + [ob, gates, wa, wb, wo, ln2, wup, wdn],
        in_specs=[full_row] + group_specs + [row(SWA_Q_WIDTH), row(GATE_WIDTH),
                  _resident(wa.shape), _resident(wb.shape), _resident(wo.shape), _resident((1, D_MODEL)),
                  _resident(wup.shape), _resident(wdn.shape)],
        out_specs=[full_row],
        out_shapes=[jax.ShapeDtypeStruct(x2.shape, _F32)],
        scratch_shapes=[pltpu.VMEM((4 * GROUP_WIDTH // LANES, tm, LANES), _F32)],
        body=_out_mlp_body)


def _rope_inv_freq():
    return (ROPE_THETA ** (-jnp.arange(0, ROPE_DIM, 2, dtype=_F32) / ROPE_DIM))[:, None]


def kernel(x, positions, ln1_g, w_in, q_norm_a, k_norm_a, q_norm_b, k_norm_b, sinks,
           w_branch_a, w_branch_b, w_out, ln2_g, w_up, w_down):
    b, s, d_model = x.shape
    assert d_model == D_MODEL and ln1_g.shape[0] == 1 and s % (max(d for _, d in DIL_PAIRS) * BLOCK) == 0
    assert s % TOKEN_TILE == 0 and s % IN_PROJ_TILE == 0 and s % ATTN_TILE == 0
    n_tiles = b * s // TOKEN_TILE
    attn_steps = b * s // ATTN_TILE
    x2 = x.reshape(b * s, d_model)
    log2e = np.float32(np.log2(np.e))
    scale = np.float32(log2e / np.sqrt(HEAD_DIM))
    gains = jnp.stack([jnp.tile(q_norm_a[0] * scale, 2), jnp.tile(k_norm_a[0], 2),
                       jnp.tile(q_norm_b[0] * scale, 2), jnp.tile(k_norm_b[0], 2)])[:, None, :]
    head = np.arange(LANES) // HEAD_DIM
    seg = jnp.asarray(head[:, None] == head[None, :], _BF16)
    pos_rows = positions.reshape(b * s // IN_PROJ_TILE, 1, IN_PROJ_TILE)
    sinks2 = sinks[0] * log2e

    (proj,) = _run_components(
        [_in_proj_component(x2, pos_rows, ln1_g, w_in[0].astype(_BF16), gains, _rope_inv_freq(), seg, b, s)],
        b * s // IN_PROJ_TILE, "in_proj", IN_PROJ_VMEM_BYTES)
    mlp_weights = [w_branch_a[0], w_branch_b[0], w_out[0], w_up[0], w_down[0]]
    *attn_outs, mlp_weights = _run_components(
        _attention_components(proj, b, s, sinks2) + [_cast_component(mlp_weights, attn_steps)], attn_steps,
        "attention", ATTN_VMEM_BYTES)
    attn_a, ob = _attention_outputs(attn_outs, b, s)
    ((out,),) = _run_components(
        [_out_mlp_component(x2, attn_a, ob, proj[4], *mlp_weights[:3], ln2_g, *mlp_weights[3:], s)],
        n_tiles, "out_mlp", OUT_MLP_VMEM_BYTES)
    return out.reshape(b, s, d_model)
```

```python
import functools

import numpy as np
import jax
import jax.numpy as jnp
from jax import lax
from jax.experimental import pallas as pl
from jax.experimental.pallas import tpu as pltpu

D_MODEL = 1024
HEAD_DIM = 64
DIL_PAIRS = ((128, 1), (512, 4), (2048, 16))
GROUP_HEADS = 4
GROUP_WIDTH = GROUP_HEADS * HEAD_DIM
DIL_WIDTH = 768
N_DIL_GROUPS = DIL_WIDTH // GROUP_WIDTH
GROUP_QKV_WIDTH = 3 * GROUP_WIDTH
SWA_WINDOW = 128
SWA_Q_WIDTH = 512
SWA_KV_WIDTH = 128
QKV_A_WIDTH = 3 * DIL_WIDTH
QKV_B_WIDTH = SWA_Q_WIDTH + 2 * SWA_KV_WIDTH
GATE_WIDTH = 2 * D_MODEL
D_FF = 4 * D_MODEL
ROPE_THETA = 500000.0
ROPE_DIM = HEAD_DIM // 4
ROPE_HALF = ROPE_DIM // 2
BLOCK = 128
EPS = 1e-6
NEG = -1e30

LANES = 128
BF16_SUBLANES = 16
CHUNK = 256
FF_CHUNK = 512
TOKEN_TILE = 512
ATTN_TILE = 1024
IN_PROJ_TILE = 1024
PROJ_LOOKAHEAD = 2
IN_PROJ_VMEM_BYTES = 56 * 1024 * 1024
ATTN_VMEM_BYTES = 34 * 1024 * 1024
OUT_MLP_VMEM_BYTES = 34 * 1024 * 1024

_BF16 = jnp.bfloat16
_F32 = jnp.float32


def _resident(shape):
    return pl.BlockSpec(shape, lambda *_: (0,) * len(shape), pipeline_mode=pl.Buffered(1))


class _Component:
    def __init__(self, inputs, in_specs, out_specs, out_shapes, scratch_shapes, body):
        self.inputs, self.in_specs, self.out_specs = list(inputs), list(in_specs), list(out_specs)
        self.out_shapes, self.scratch_shapes, self.body = list(out_shapes), list(scratch_shapes), body


def _run_components(components, n_steps, name, vmem_limit_bytes):
    n_in = [len(c.inputs) for c in components]
    n_out = [len(c.out_specs) for c in components]
    n_scr = [len(c.scratch_shapes) for c in components]

    def body(*refs):
        refs = list(refs)
        ins, outs, scrs = [], [], []
        pos = 0
        for group, counts in ((ins, n_in), (outs, n_out), (scrs, n_scr)):
            for k in counts:
                group.append(refs[pos:pos + k])
                pos += k
        step = pl.program_id(0)
        for c, i, o, s in zip(components, ins, outs, scrs):
            c.body(i, o, s, step)

    outs = pl.pallas_call(
        body,
        grid=(n_steps,),
        in_specs=[s for c in components for s in c.in_specs],
        out_specs=[s for c in components for s in c.out_specs],
        out_shape=[s for c in components for s in c.out_shapes],
        scratch_shapes=[s for c in components for s in c.scratch_shapes],
        compiler_params=pltpu.CompilerParams(dimension_semantics=("arbitrary",), vmem_limit_bytes=vmem_limit_bytes),
        name=name,
    )(*[a for c in components for a in c.inputs])
    split, pos = [], 0
    for k in n_out:
        split.append(outs[pos:pos + k])
        pos += k
    return split


def _in_proj_body(in_refs, out_refs, scratch_refs, step):
    del step
    x_ref, pos_ref, ln_ref, w_ref, gain_ref, freq_ref, seg_ref = in_refs
    qkv0_ref, qkv1_ref, qkv2_ref, qkvb_ref, gate_ref = out_refs
    slab1_ref, slab2_ref = scratch_refs
    tm = x_ref.shape[0]
    x = x_ref[...]
    ms = jnp.mean(x * x, axis=-1, keepdims=True)
    h = (x * lax.rsqrt(ms + EPS) * ln_ref[...]).astype(_BF16)

    ang = freq_ref[...] * pos_ref[...].astype(_F32)
    packed = jnp.concatenate([jnp.cos(ang), jnp.sin(ang), jnp.ones_like(ang),
                              jnp.zeros((LANES - 3 * ROPE_HALF, tm), _F32)], axis=0).T
    lane = lax.broadcasted_iota(jnp.int32, (tm, LANES), 1) % HEAD_DIM
    rotary = lane < ROPE_DIM
    cos_t = jnp.take_along_axis(packed, jnp.where(rotary, lane % ROPE_HALF, 2 * ROPE_HALF), axis=1,
                                mode="promise_in_bounds")
    sin_t = jnp.take_along_axis(packed, jnp.where(rotary, ROPE_HALF + lane % ROPE_HALF, 3 * ROPE_HALF), axis=1,
                                mode="promise_in_bounds")
    first_half = lane < ROPE_HALF
    sin_from_hi = jnp.where(first_half, -sin_t, 0.0)
    sin_from_lo = jnp.where(first_half, 0.0, sin_t)
    seg = seg_ref[...]

    def norm_rope(y, gain):
        ss = jnp.dot((y * y).astype(_BF16), seg, preferred_element_type=_F32)
        yn = y * lax.rsqrt(ss * (1.0 / HEAD_DIM) + EPS) * gain
        up = pltpu.roll(yn, LANES - ROPE_HALF, 1)
        dn = pltpu.roll(yn, ROPE_HALF, 1)
        return yn * cos_t + up * sin_from_hi + dn * sin_from_lo

    def project(w_col):
        return jnp.dot(h, w_ref[:, w_col:w_col + CHUNK], preferred_element_type=_F32)

    tasks = []
    group_out = (qkv0_ref, slab1_ref, slab2_ref)

    def dilated_store(g, slab):
        def store(yh):
            if g == 0:
                qkv0_ref[:, slab * LANES:(slab + 1) * LANES] = yh.astype(_BF16)
            else:
                group_out[g][slab] = yh
        return store

    def row_store(out_ref, col):
        def store(yh):
            out_ref[:, col:col + LANES] = yh.astype(_BF16)
        return store

    def gate_store(col):
        def store(yh):
            gate_ref[:, col:col + LANES] = (0.5 * jnp.tanh(0.5 * yh) + 0.5).astype(_BF16)
        return store

    n_halves = CHUNK // LANES
    for part in range(3):
        for g in range(N_DIL_GROUPS):
            tasks.append((part * DIL_WIDTH + g * GROUP_WIDTH,
                          [(part if part < 2 else None, dilated_store(g, part * n_halves + i))
                           for i in range(n_halves)]))
    for c in range(QKV_B_WIDTH // CHUNK):
        cols = [c * CHUNK + i * LANES for i in range(n_halves)]
        tasks.append((QKV_A_WIDTH + c * CHUNK,
                      [(2 if col < SWA_Q_WIDTH else (3 if col < SWA_Q_WIDTH + SWA_KV_WIDTH else None),
                        row_store(qkvb_ref, col)) for col in cols]))
    for c in range(GATE_WIDTH // CHUNK):
        tasks.append((QKV_A_WIDTH + QKV_B_WIDTH + c * CHUNK,
                      [(None, gate_store(c * CHUNK + i * LANES)) for i in range(n_halves)]))

    def epilogue(y, stores):
        for i, (gain_idx, store) in enumerate(stores):
            yh = y[:, i * LANES:(i + 1) * LANES]
            store(yh if gain_idx is None else norm_rope(yh, gain_ref[gain_idx]))

    in_flight = []
    for w_col, stores in tasks:
        in_flight.append((project(w_col), stores))
        if len(in_flight) > PROJ_LOOKAHEAD:
            epilogue(*in_flight.pop(0))
    for y, stores in in_flight:
        epilogue(y, stores)

    for g, out_ref, slab_ref in ((1, qkv1_ref, slab1_ref), (2, qkv2_ref, slab2_ref)):
        d = DIL_PAIRS[g][1]
        for r in range(d):
            for slab in range(GROUP_QKV_WIDTH // LANES):
                rows = slab_ref[slab, pl.ds(r, tm // d, stride=d), :]
                out_ref[r, :, slab * LANES:(slab + 1) * LANES] = rows.astype(_BF16)


def _in_proj_component(x2, pos_rows, ln1, w_in, gains, freq, seg, batch, seq):
    tm = IN_PROJ_TILE
    n = batch * seq
    tiles_per_seq = seq // tm
    out_row = lambda width: pl.BlockSpec((tm, width), lambda i: (i, 0))

    def deinterleaved(d):
        return pl.BlockSpec((None, d, tm // d, GROUP_QKV_WIDTH),
                            lambda i: (i // tiles_per_seq, 0, i % tiles_per_seq, 0))

    d1, d2 = DIL_PAIRS[1][1], DIL_PAIRS[2][1]
    n_slabs = GROUP_QKV_WIDTH // LANES
    return _Component(
        inputs=[x2, pos_rows, ln1, w_in, gains, freq, seg],
        in_specs=[out_row(D_MODEL), pl.BlockSpec((None, 1, tm), lambda i: (i, 0, 0)),
                  _resident((1, D_MODEL)), _resident(w_in.shape), _resident(gains.shape), _resident(freq.shape),
                  _resident(seg.shape)],
        out_specs=[out_row(GROUP_QKV_WIDTH), deinterleaved(d1), deinterleaved(d2), out_row(QKV_B_WIDTH),
                   out_row(GATE_WIDTH)],
        out_shapes=[jax.ShapeDtypeStruct((n, GROUP_QKV_WIDTH), _BF16),
                    jax.ShapeDtypeStruct((batch, d1, seq // d1, GROUP_QKV_WIDTH), _BF16),
                    jax.ShapeDtypeStruct((batch, d2, seq // d2, GROUP_QKV_WIDTH), _BF16),
                    jax.ShapeDtypeStruct((n, QKV_B_WIDTH), _BF16),
                    jax.ShapeDtypeStruct((n, GATE_WIDTH), _BF16)],
        scratch_shapes=[pltpu.VMEM((n_slabs, tm, LANES), _F32), pltpu.VMEM((n_slabs, tm, LANES), _F32)],
        body=_in_proj_body)


_NT = (((1,), (1,)), ((), ()))
_STACK = GROUP_HEADS * BLOCK


def _stacked_scores(q, k_win):
    head_of_lane = lax.broadcasted_iota(jnp.int32, q.shape, 1) // HEAD_DIM
    zero = jnp.zeros_like(q)
    q_stack = jnp.concatenate([jnp.where(head_of_lane == hd, q, zero) for hd in range(GROUP_HEADS)], axis=0)
    return lax.dot_general(q_stack, k_win, _NT, preferred_element_type=_F32)


def _band_bias(max_dist, first_key_col):
    row = lax.broadcasted_iota(jnp.int32, (_STACK, 2 * BLOCK), 0) % BLOCK
    col = lax.broadcasted_iota(jnp.int32, (_STACK, 2 * BLOCK), 1)
    dist = row + BLOCK - col
    return jnp.where((dist >= 0) & (dist <= max_dist) & (col >= first_key_col), 0.0, NEG).astype(_F32)


def _head_rows(hd):
    return slice(hd * BLOCK, (hd + 1) * BLOCK)


def _merge_heads(stack_col):
    low = lax.broadcasted_iota(jnp.int32, (BLOCK, LANES), 1) < HEAD_DIM
    return jnp.concatenate([jnp.where(low, stack_col(2 * pair, pair), stack_col(2 * pair + 1, pair))
                            for pair in range(GROUP_HEADS // 2)], axis=1)


def _attend(s, bias, v_win):
    s = s + bias
    m = jnp.max(s, axis=1, keepdims=True)
    p = jnp.exp2(s - m).astype(_BF16)
    pair_rows = 2 * BLOCK
    ones = jnp.ones((v_win.shape[0], LANES), _BF16)
    pv = [jnp.dot(p[pair * pair_rows:(pair + 1) * pair_rows],
                  jnp.concatenate([v_win[:, pair * LANES:(pair + 1) * LANES], ones], axis=1),
                  preferred_element_type=_F32) for pair in range(GROUP_HEADS // 2)]
    head_block = lambda hd: slice((hd % 2) * BLOCK, (hd % 2 + 1) * BLOCK)
    denom = _merge_heads(lambda hd, pair: pv[pair][head_block(hd), LANES:])
    o = _merge_heads(lambda hd, pair: pv[pair][head_block(hd), :LANES]) / denom
    m = _merge_heads(lambda hd, pair: jnp.broadcast_to(m[_head_rows(hd)], (BLOCK, LANES)))
    return o, m + jnp.log2(denom)


def _window(qb, prev_ref, cur_ref, seq=None):
    idx = (lambda rows: (rows, slice(None))) if seq is None else (lambda rows: (seq, rows, slice(None)))
    rows = slice(qb * BLOCK, (qb + 1) * BLOCK)
    if qb == 0:
        prev = prev_ref[idx(slice(0, BLOCK))]
    else:
        prev = cur_ref[idx(slice((qb - 1) * BLOCK, qb * BLOCK))]
    return jnp.concatenate([prev, cur_ref[idx(rows)]], axis=0)


def _dilated_body(in_refs, out_refs, scratch_refs, step, *, max_dist, tiles_per_seq):
    del scratch_refs
    q_ref, kprev_ref, k_ref, vprev_ref, v_ref = in_refs
    o_ref, lse_ref = out_refs
    n_seqs, tile, _ = q_ref.shape
    first_tile = (step % tiles_per_seq) == 0
    bias_first = _band_bias(max_dist, jnp.where(first_tile, BLOCK, 0))
    bias_inner = _band_bias(max_dist, 0)
    for seq in range(n_seqs):
        for qb in range(tile // BLOCK):
            rows = slice(qb * BLOCK, (qb + 1) * BLOCK)
            s = _stacked_scores(q_ref[seq, rows, :], _window(qb, kprev_ref, k_ref, seq))
            o, lse = _attend(s, bias_first if qb == 0 else bias_inner, _window(qb, vprev_ref, v_ref, seq))
            o_ref[seq, rows, :] = o.astype(o_ref.dtype)
            lse_ref[seq, rows, :] = lse


def _dilated_component(qkv, window, dilation):
    n_seqs, length, _ = qkv.shape
    tile = min(ATTN_TILE, length)
    seqs = ATTN_TILE // tile
    n_blocks = tile // BLOCK
    tiles_per_seq = length // tile

    def cur(col):
        return pl.BlockSpec((seqs, tile, GROUP_WIDTH), lambda i: (i // tiles_per_seq, i % tiles_per_seq, col))

    def prev(col):
        return pl.BlockSpec((seqs, BLOCK, GROUP_WIDTH),
                            lambda i: (i // tiles_per_seq, jnp.maximum((i % tiles_per_seq) * n_blocks - 1, 0), col))

    out_spec = pl.BlockSpec((seqs, tile, GROUP_WIDTH), lambda i: (i // tiles_per_seq, i % tiles_per_seq, 0))
    return _Component(
        inputs=[qkv] * 5,
        in_specs=[cur(0), prev(1), cur(1), prev(2), cur(2)],
        out_specs=[out_spec, out_spec],
        out_shapes=[jax.ShapeDtypeStruct((n_seqs, length, GROUP_WIDTH), _BF16),
                    jax.ShapeDtypeStruct((n_seqs, length, GROUP_WIDTH), _F32)],
        scratch_shapes=[],
        body=functools.partial(_dilated_body, max_dist=window // dilation, tiles_per_seq=tiles_per_seq))


def _swa_body(in_refs, out_refs, scratch_refs, step, *, tiles_per_seq):
    sink_ref, q_ref, kprev_ref, k_ref, vprev_ref, v_ref = in_refs
    (o_ref,) = out_refs
    kdup_ref, vdup_ref = scratch_refs
    tile = q_ref.shape[0]
    first_tile = (step % tiles_per_seq) == 0
    n_kv_heads = SWA_KV_WIDTH // HEAD_DIM

    def duplicate(dst_ref, row0, t):
        low = lax.broadcasted_iota(jnp.int32, t.shape, 1) < HEAD_DIM
        t32 = t.astype(_F32)
        swapped = pltpu.roll(t32, HEAD_DIM, 1)
        dst_ref[0, row0:row0 + t.shape[0], :] = jnp.where(low, t32, swapped).astype(_BF16)
        dst_ref[1, row0:row0 + t.shape[0], :] = jnp.where(low, swapped, t32).astype(_BF16)

    for dst_ref, prev_ref, cur_ref in ((kdup_ref, kprev_ref, k_ref), (vdup_ref, vprev_ref, v_ref)):
        duplicate(dst_ref, 0, prev_ref[...])
        duplicate(dst_ref, BLOCK, cur_ref[...])

    low = lax.broadcasted_iota(jnp.int32, (BLOCK, LANES), 1) < HEAD_DIM

    def scores(qb, kv):
        rows = slice(qb * BLOCK, (qb + 1) * BLOCK)
        parts = []
        for pair in range(GROUP_HEADS // 2):
            lane0 = kv * GROUP_WIDTH + pair * LANES
            q_pair = q_ref[rows, lane0:lane0 + LANES]
            zero = jnp.zeros_like(q_pair)
            parts += [jnp.where(low, q_pair, zero), jnp.where(low, zero, q_pair)]
        q_stack = jnp.concatenate(parts, axis=0)
        k_win = kdup_ref[kv, qb * BLOCK:(qb + 2) * BLOCK, :]
        return lax.dot_general(q_stack, k_win, _NT, preferred_element_type=_F32)

    sink_col = lax.broadcasted_iota(jnp.int32, (_STACK, 2 * BLOCK), 1) == 0
    bias_first = jnp.where(sink_col, 0.0, _band_bias(SWA_WINDOW - 1, jnp.where(first_tile, BLOCK, 0)))
    bias_inner = jnp.where(sink_col, 0.0, _band_bias(SWA_WINDOW - 1, 0))
    sink_lane = lax.broadcasted_iota(jnp.int32, (BLOCK, LANES), 1) == 0
    sink_row = lax.broadcasted_iota(jnp.int32, (2 * BLOCK, LANES), 0) == 0

    def finish(qb, kv, s):
        rows = slice(qb * BLOCK, (qb + 1) * BLOCK)
        s_left = jnp.concatenate([jnp.where(sink_lane, sink_ref[kv * GROUP_HEADS + hd], s[_head_rows(hd), :LANES])
                                  for hd in range(GROUP_HEADS)], axis=0)
        s = jnp.concatenate([s_left, s[:, LANES:]], axis=1) + (bias_first if qb == 0 else bias_inner)
        p = jnp.exp2(s - jnp.max(s, axis=1, keepdims=True)).astype(_BF16)
        v_win = jnp.where(sink_row, jnp.zeros((), _BF16), vdup_ref[kv, qb * BLOCK:(qb + 2) * BLOCK, :])
        pv = jnp.dot(p, jnp.concatenate([v_win, jnp.ones_like(v_win)], axis=1),
                     preferred_element_type=_F32)
        for pair in range(GROUP_HEADS // 2):
            lane0 = kv * GROUP_WIDTH + pair * LANES
            lo, hi = pv[_head_rows(2 * pair)], pv[_head_rows(2 * pair + 1)]
            o_pair = jnp.where(low, lo[:, :LANES], hi[:, :LANES]) / jnp.where(low, lo[:, LANES:], hi[:, LANES:])
            o_ref[rows, lane0:lane0 + LANES] = o_pair.astype(o_ref.dtype)

    for qb in range(tile // BLOCK):
        for kv in range(n_kv_heads):
            finish(qb, kv, scores(qb, kv))


def _swa_component(qkvb3, sinks):
    b, s, _ = qkvb3.shape
    tile = ATTN_TILE
    n_blocks = tile // BLOCK
    tiles_per_seq = s // tile
    k_col = SWA_Q_WIDTH // LANES
    v_col = k_col + SWA_KV_WIDTH // LANES

    def cur(col):
        return pl.BlockSpec((None, tile, LANES), lambda i: (i // tiles_per_seq, i % tiles_per_seq, col))

    def prev(col):
        return pl.BlockSpec((None, BLOCK, LANES),
                            lambda i: (i // tiles_per_seq, jnp.maximum((i % tiles_per_seq) * n_blocks - 1, 0), col))

    q_spec = pl.BlockSpec((None, tile, SWA_Q_WIDTH), lambda i: (i // tiles_per_seq, i % tiles_per_seq, 0))
    return _Component(
        inputs=[sinks] + [qkvb3] * 5,
        in_specs=[pl.BlockSpec(memory_space=pltpu.SMEM), q_spec, prev(k_col), cur(k_col), prev(v_col), cur(v_col)],
        out_specs=[q_spec],
        out_shapes=[jax.ShapeDtypeStruct((b, s, SWA_Q_WIDTH), _BF16)],
        scratch_shapes=[pltpu.VMEM((SWA_KV_WIDTH // HEAD_DIM, BLOCK + tile, LANES), _BF16)] * 2,
        body=functools.partial(_swa_body, tiles_per_seq=tiles_per_seq))


def _cast_body(in_refs, out_refs, scratch_refs, step):
    del scratch_refs, step
    for src_ref, dst_ref in zip(in_refs, out_refs):
        dst_ref[...] = src_ref[...].astype(dst_ref.dtype)


def _cast_component(weights, n_steps):
    specs = []
    for w in weights:
        rows, cols = w.shape
        block_rows = max(BF16_SUBLANES, rows // n_steps)
        assert rows % block_rows == 0 and rows // block_rows <= n_steps
        last = rows // block_rows - 1
        specs.append(pl.BlockSpec((block_rows, cols), lambda i, last=last: (jnp.minimum(i, last), 0)))
    return _Component(inputs=weights, in_specs=specs, out_specs=specs,
                      out_shapes=[jax.ShapeDtypeStruct(w.shape, _BF16) for w in weights],
                      scratch_shapes=[], body=_cast_body)


def _attention_components(proj, batch, seq, sinks2):
    qkv0, qkv1, qkv2, qkvb, _ = proj
    comps = []
    for (window, d), qkv in zip(DIL_PAIRS, (qkv0, qkv1, qkv2)):
        comps.append(_dilated_component(qkv.reshape(batch * d, seq // d, GROUP_QKV_WIDTH), window, d))
    comps.append(_swa_component(qkvb.reshape(batch, seq, QKV_B_WIDTH), sinks2))
    return comps


def _attention_outputs(outs, batch, seq):
    attn_a = []
    for (_, d), (o, lse) in zip(DIL_PAIRS, outs[:3]):
        shape = (batch * seq, GROUP_WIDTH) if d == 1 else (batch, d, seq // d, GROUP_WIDTH)
        attn_a.append((o.reshape(shape), lse.reshape(shape)))
    return attn_a, outs[3][0].reshape(batch * seq, SWA_Q_WIDTH)


def _out_mlp_body(in_refs, out_refs, scratch_refs, step):
    del step
    (x_ref, o0_ref, l0_ref, o1_ref, l1_ref, o2_ref, l2_ref, ob_ref, gate_ref,
     wa_ref, wb_ref, wo_ref, ln_ref, wup_ref, wdn_ref) = in_refs
    (out_ref,) = out_refs
    (slab_ref,) = scratch_refs
    tm = x_ref.shape[0]
    n_slabs = GROUP_WIDTH // LANES

    def interleaved(src_ref, base):
        d = src_ref.shape[0]
        for r in range(d):
            for slab in range(n_slabs):
                slab_ref[base + slab, pl.ds(r, tm // d, stride=d), :] = (
                    src_ref[r, :, slab * LANES:(slab + 1) * LANES].astype(_F32))
        return jnp.concatenate([slab_ref[base + slab] for slab in range(n_slabs)], axis=1)

    o0, l0 = o0_ref[...].astype(_F32), l0_ref[...]
    o1, l1 = interleaved(o1_ref, 0), interleaved(l1_ref, n_slabs)
    o2, l2 = interleaved(o2_ref, 2 * n_slabs), interleaved(l2_ref, 3 * n_slabs)
    m = jnp.maximum(jnp.maximum(l0, l1), l2)
    e0, e1, e2 = jnp.exp2(l0 - m), jnp.exp2(l1 - m), jnp.exp2(l2 - m)
    oa = ((e0 * o0 + e1 * o1 + e2 * o2) / (e0 + e1 + e2)).astype(_BF16)

    ya = jnp.dot(oa, wa_ref[...], preferred_element_type=_F32)
    yb = jnp.dot(ob_ref[...], wb_ref[...], preferred_element_type=_F32)
    gate_a = gate_ref[:, :D_MODEL].astype(_F32)
    gate_b = gate_ref[:, D_MODEL:].astype(_F32)
    mix = (gate_a * ya + gate_b * yb).astype(_BF16)
    x1 = x_ref[...] + jnp.dot(mix, wo_ref[...], preferred_element_type=_F32)

    ms = jnp.mean(x1 * x1, axis=-1, keepdims=True)
    h2 = (x1 * lax.rsqrt(ms + EPS) * ln_ref[...]).astype(_BF16)
    acc = x1
    for c in range(D_FF // FF_CHUNK):
        u = jnp.dot(h2, wup_ref[:, c * FF_CHUNK:(c + 1) * FF_CHUNK], preferred_element_type=_F32)
        a = jnp.square(jnp.maximum(u, 0.0)).astype(_BF16)
        acc = acc + jnp.dot(a, wdn_ref[c * FF_CHUNK:(c + 1) * FF_CHUNK, :], preferred_element_type=_F32)
    out_ref[...] = acc


def _out_mlp_component(x2, attn_a, ob, gates, wa, wb, wo, ln2, wup, wdn, seq):
    tm = TOKEN_TILE
    tiles_per_seq = seq // tm
    row = lambda width: pl.BlockSpec((tm, width), lambda i: (i, 0))
    full_row = row(D_MODEL)

    def deinterleaved(d):
        return pl.BlockSpec((None, d, tm // d, GROUP_WIDTH), lambda i: (i // tiles_per_seq, 0, i % tiles_per_seq, 0))

    group_specs = [row(GROUP_WIDTH)] * 2
    for _, d in DIL_PAIRS[1:]:
        group_specs += [deinterleaved(d)] * 2
    return _Component(
        inputs=[x2] + [a for pair in attn_a for a in pair] + [ob, gates, wa, wb, wo, ln2, wup, wdn],
        in_specs=[full_row] + group_specs + [row(SWA_Q_WIDTH), row(GATE_WIDTH),
                  _resident(wa.shape), _resident(wb.shape), _resident(wo.shape), _resident((1, D_MODEL)),
                  _resident(wup.shape), _resident(wdn.shape)],
        out_specs=[full_row],
        out_shapes=[jax.ShapeDtypeStruct(x2.shape, _F32)],
        scratch_shapes=[pltpu.VMEM((4 * GROUP_WIDTH // LANES, tm, LANES), _F32)],
        body=_out_mlp_body)


def _rope_inv_freq():
    return (ROPE_THETA ** (-jnp.arange(0, ROPE_DIM, 2, dtype=_F32) / ROPE_DIM))[:, None]


def kernel(x, positions, ln1_g, w_in, q_norm_a, k_norm_a, q_norm_b, k_norm_b, sinks,
           w_branch_a, w_branch_b, w_out, ln2_g, w_up, w_down):
    b, s, d_model = x.shape
    assert d_model == D_MODEL and ln1_g.shape[0] == 1 and s % (max(d for _, d in DIL_PAIRS) * BLOCK) == 0
    assert s % TOKEN_TILE == 0 and s % IN_PROJ_TILE == 0 and s % ATTN_TILE == 0
    n_tiles = b * s // TOKEN_TILE
    attn_steps = b * s // ATTN_TILE
    x2 = x.reshape(b * s, d_model)
    log2e = np.float32(np.log2(np.e))
    scale = np.float32(log2e / np.sqrt(HEAD_DIM))
    gains = jnp.stack([jnp.tile(q_norm_a[0] * scale, 2), jnp.tile(k_norm_a[0], 2),
                       jnp.tile(q_norm_b[0] * scale, 2), jnp.tile(k_norm_b[0], 2)])[:, None, :]
    head = np.arange(LANES) // HEAD_DIM
    seg = jnp.asarray(head[:, None] == head[None, :], _BF16)
    pos_rows = positions.reshape(b * s // IN_PROJ_TILE, 1, IN_PROJ_TILE)
    sinks2 = sinks[0] * log2e

    (proj,) = _run_components(
        [_in_proj_component(x2, pos_rows, ln1_g, w_in[0].astype(_BF16), gains, _rope_inv_freq(), seg, b, s)],
        b * s // IN_PROJ_TILE, "in_proj", IN_PROJ_VMEM_BYTES)
    mlp_weights = [w_branch_a[0], w_branch_b[0], w_out[0], w_up[0], w_down[0]]
    *attn_outs, mlp_weights = _run_components(
        _attention_components(proj, b, s, sinks2) + [_cast_component(mlp_weights, attn_steps)], attn_steps,
        "attention", ATTN_VMEM_BYTES)
    attn_a, ob = _attention_outputs(attn_outs, b, s)
    ((out,),) = _run_components(
        [_out_mlp_component(x2, attn_a, ob, proj[4], *mlp_weights[:3], ln2_g, *mlp_weights[3:], s)],
        n_tiles, "out_mlp", OUT_MLP_VMEM_BYTES)
    return out.reshape(b, s, d_model)
```

```python
import functools

import numpy as np
import jax
import jax.numpy as jnp
from jax import lax
from jax.experimental import pallas as pl
from jax.experimental.pallas import tpu as pltpu

D_MODEL = 1024
HEAD_DIM = 64
DIL_PAIRS = ((128, 1), (512, 4), (2048, 16))
GROUP_HEADS = 4
GROUP_WIDTH = GROUP_HEADS * HEAD_DIM
DIL_WIDTH = 768
N_DIL_GROUPS = DIL_WIDTH // GROUP_WIDTH
GROUP_QKV_WIDTH = 3 * GROUP_WIDTH
SWA_WINDOW = 128
SWA_Q_WIDTH = 512
SWA_KV_WIDTH = 128
QKV_A_WIDTH = 3 * DIL_WIDTH
QKV_B_WIDTH = SWA_Q_WIDTH + 2 * SWA_KV_WIDTH
GATE_WIDTH = 2 * D_MODEL
D_FF = 4 * D_MODEL
ROPE_THETA = 500000.0
ROPE_DIM = HEAD_DIM // 4
ROPE_HALF = ROPE_DIM // 2
BLOCK = 128
EPS = 1e-6
NEG = -1e30

LANES = 128
BF16_SUBLANES = 16
CHUNK = 256
FF_CHUNK = 512
TOKEN_TILE = 512
ATTN_TILE = 1024
IN_PROJ_TILE = 1024
PROJ_LOOKAHEAD = 2
IN_PROJ_VMEM_BYTES = 56 * 1024 * 1024
ATTN_VMEM_BYTES = 42 * 1024 * 1024
OUT_MLP_VMEM_BYTES = 48 * 1024 * 1024

_BF16 = jnp.bfloat16
_F32 = jnp.float32


def _resident(shape):
    return pl.BlockSpec(shape, lambda *_: (0,) * len(shape), pipeline_mode=pl.Buffered(1))


class _Component:
    def __init__(self, inputs, in_specs, out_specs, out_shapes, scratch_shapes, body):
        self.inputs, self.in_specs, self.out_specs = list(inputs), list(in_specs), list(out_specs)
        self.out_shapes, self.scratch_shapes, self.body = list(out_shapes), list(scratch_shapes), body


def _run_components(components, n_steps, name, vmem_limit_bytes):
    n_in = [len(c.inputs) for c in components]
    n_out = [len(c.out_specs) for c in components]
    n_scr = [len(c.scratch_shapes) for c in components]

    def body(*refs):
        refs = list(refs)
        ins, outs, scrs = [], [], []
        pos = 0
        for group, counts in ((ins, n_in), (outs, n_out), (scrs, n_scr)):
            for k in counts:
                group.append(refs[pos:pos + k])
                pos += k
        step = pl.program_id(0)
        for c, i, o, s in zip(components, ins, outs, scrs):
            c.body(i, o, s, step)

    outs = pl.pallas_call(
        body,
        grid=(n_steps,),
        in_specs=[s for c in components for s in c.in_specs],
        out_specs=[s for c in components for s in c.out_specs],
        out_shape=[s for c in components for s in c.out_shapes],
        scratch_shapes=[s for c in components for s in c.scratch_shapes],
        compiler_params=pltpu.CompilerParams(dimension_semantics=("arbitrary",), vmem_limit_bytes=vmem_limit_bytes),
        name=name,
    )(*[a for c in components for a in c.inputs])
    split, pos = [], 0
    for k in n_out:
        split.append(outs[pos:pos + k])
        pos += k
    return split


def _in_proj_body(in_refs, out_refs, scratch_refs, step):
    del step
    x_ref, pos_ref, ln_ref, w_ref, gain_ref, freq_ref, seg_ref = in_refs
    qkv0_ref, qkv1_ref, qkv2_ref, qkvb_ref, gate_ref = out_refs
    slab1_ref, slab2_ref = scratch_refs
    tm = x_ref.shape[0]
    x = x_ref[...]
    ms = jnp.mean(x * x, axis=-1, keepdims=True)
    h = (x * lax.rsqrt(ms + EPS) * ln_ref[...]).astype(_BF16)

    ang = freq_ref[...] * pos_ref[...].astype(_F32)
    packed = jnp.concatenate([jnp.cos(ang), jnp.sin(ang), jnp.ones_like(ang),
                              jnp.zeros((LANES - 3 * ROPE_HALF, tm), _F32)], axis=0).T
    lane = lax.broadcasted_iota(jnp.int32, (tm, LANES), 1) % HEAD_DIM
    rotary = lane < ROPE_DIM
    cos_t = jnp.take_along_axis(packed, jnp.where(rotary, lane % ROPE_HALF, 2 * ROPE_HALF), axis=1,
                                mode="promise_in_bounds")
    sin_t = jnp.take_along_axis(packed, jnp.where(rotary, ROPE_HALF + lane % ROPE_HALF, 3 * ROPE_HALF), axis=1,
                                mode="promise_in_bounds")
    first_half = lane < ROPE_HALF
    sin_from_hi = jnp.where(first_half, -sin_t, 0.0)
    sin_from_lo = jnp.where(first_half, 0.0, sin_t)
    seg = seg_ref[...]

    def norm_rope(y, gain):
        ss = jnp.dot((y * y).astype(_BF16), seg, preferred_element_type=_F32)
        yn = y * lax.rsqrt(ss * (1.0 / HEAD_DIM) + EPS) * gain
        up = pltpu.roll(yn, LANES - ROPE_HALF, 1)
        dn = pltpu.roll(yn, ROPE_HALF, 1)
        return yn * cos_t + up * sin_from_hi + dn * sin_from_lo

    def project(w_col):
        return jnp.dot(h, w_ref[:, w_col:w_col + CHUNK], preferred_element_type=_F32)

    tasks = []
    group_out = (qkv0_ref, slab1_ref, slab2_ref)

    def dilated_store(g, slab):
        def store(yh):
            if g == 0:
                qkv0_ref[:, slab * LANES:(slab + 1) * LANES] = yh.astype(_BF16)
            else:
                group_out[g][slab] = yh
        return store

    def row_store(out_ref, col):
        def store(yh):
            out_ref[:, col:col + LANES] = yh.astype(_BF16)
        return store

    def gate_store(col):
        def store(yh):
            gate_ref[:, col:col + LANES] = (0.5 * jnp.tanh(0.5 * yh) + 0.5).astype(_BF16)
        return store

    n_halves = CHUNK // LANES
    for part in range(3):
        for g in range(N_DIL_GROUPS):
            tasks.append((part * DIL_WIDTH + g * GROUP_WIDTH,
                          [(part if part < 2 else None, dilated_store(g, part * n_halves + i))
                           for i in range(n_halves)]))
    for c in range(QKV_B_WIDTH // CHUNK):
        cols = [c * CHUNK + i * LANES for i in range(n_halves)]
        tasks.append((QKV_A_WIDTH + c * CHUNK,
                      [(2 if col < SWA_Q_WIDTH else (3 if col < SWA_Q_WIDTH + SWA_KV_WIDTH else None),
                        row_store(qkvb_ref, col)) for col in cols]))
    for c in range(GATE_WIDTH // CHUNK):
        tasks.append((QKV_A_WIDTH + QKV_B_WIDTH + c * CHUNK,
                      [(None, gate_store(c * CHUNK + i * LANES)) for i in range(n_halves)]))

    def epilogue(y, stores):
        for i, (gain_idx, store) in enumerate(stores):
            yh = y[:, i * LANES:(i + 1) * LANES]
            store(yh if gain_idx is None else norm_rope(yh, gain_ref[gain_idx]))

    in_flight = []
    for w_col, stores in tasks:
        in_flight.append((project(w_col), stores))
        if len(in_flight) > PROJ_LOOKAHEAD:
            epilogue(*in_flight.pop(0))
    for y, stores in in_flight:
        epilogue(y, stores)

    for g, out_ref, slab_ref in ((1, qkv1_ref, slab1_ref), (2, qkv2_ref, slab2_ref)):
        d = DIL_PAIRS[g][1]
        for r in range(d):
            for slab in range(GROUP_QKV_WIDTH // LANES):
                rows = slab_ref[slab, pl.ds(r, tm // d, stride=d), :]
                out_ref[r, :, slab * LANES:(slab + 1) * LANES] = rows.astype(_BF16)


def _in_proj_component(x2, pos_rows, ln1, w_in, gains, freq, seg, batch, seq):
    tm = IN_PROJ_TILE
    n = batch * seq
    tiles_per_seq = seq // tm
    out_row = lambda width: pl.BlockSpec((tm, width), lambda i: (i, 0))

    def deinterleaved(d):
        return pl.BlockSpec((None, d, tm // d, GROUP_QKV_WIDTH),
                            lambda i: (i // tiles_per_seq, 0, i % tiles_per_seq, 0))

    d1, d2 = DIL_PAIRS[1][1], DIL_PAIRS[2][1]
    n_slabs = GROUP_QKV_WIDTH // LANES
    return _Component(
        inputs=[x2, pos_rows, ln1, w_in, gains, freq, seg],
        in_specs=[out_row(D_MODEL), pl.BlockSpec((None, 1, tm), lambda i: (i, 0, 0)),
                  _resident((1, D_MODEL)), _resident(w_in.shape), _resident(gains.shape), _resident(freq.shape),
                  _resident(seg.shape)],
        out_specs=[out_row(GROUP_QKV_WIDTH), deinterleaved(d1), deinterleaved(d2), out_row(QKV_B_WIDTH),
                   out_row(GATE_WIDTH)],
        out_shapes=[jax.ShapeDtypeStruct((n, GROUP_QKV_WIDTH), _BF16),
                    jax.ShapeDtypeStruct((batch, d1, seq // d1, GROUP_QKV_WIDTH), _BF16),
                    jax.ShapeDtypeStruct((batch, d2, seq // d2, GROUP_QKV_WIDTH), _BF16),
                    jax.ShapeDtypeStruct((n, QKV_B_WIDTH), _BF16),
                    jax.ShapeDtypeStruct((n, GATE_WIDTH), _BF16)],
        scratch_shapes=[pltpu.VMEM((n_slabs, tm, LANES), _F32), pltpu.VMEM((n_slabs, tm, LANES), _F32)],
        body=_in_proj_body)


_NT = (((1,), (1,)), ((), ()))
_STACK = GROUP_HEADS * BLOCK


def _stacked_scores(q, k_win):
    head_of_lane = lax.broadcasted_iota(jnp.int32, q.shape, 1) // HEAD_DIM
    zero = jnp.zeros_like(q)
    q_stack = jnp.concatenate([jnp.where(head_of_lane == hd, q, zero) for hd in range(GROUP_HEADS)], axis=0)
    return lax.dot_general(q_stack, k_win, _NT, preferred_element_type=_F32)


def _band_bias(max_dist, first_key_col):
    row = lax.broadcasted_iota(jnp.int32, (BLOCK, 2 * BLOCK), 0)
    col = lax.broadcasted_iota(jnp.int32, (BLOCK, 2 * BLOCK), 1)
    dist = row + BLOCK - col
    one = jnp.where((dist >= 0) & (dist <= max_dist) & (col >= first_key_col), 0.0, NEG).astype(_F32)
    return jnp.concatenate([one] * GROUP_HEADS, axis=0)


def _head_rows(hd):
    return slice(hd * BLOCK, (hd + 1) * BLOCK)


def _merge_heads(stack_col):
    low = lax.broadcasted_iota(jnp.int32, (BLOCK, LANES), 1) < HEAD_DIM
    return jnp.concatenate([jnp.where(low, stack_col(2 * pair, pair), stack_col(2 * pair + 1, pair))
                            for pair in range(GROUP_HEADS // 2)], axis=1)


def _attend(s, bias, v_win):
    s = s + bias
    m = jnp.max(s, axis=1, keepdims=True)
    p = jnp.exp2(s - m).astype(_BF16)
    pair_rows = 2 * BLOCK
    ones = jnp.ones((v_win.shape[0], LANES), _BF16)
    pv = [jnp.dot(p[pair * pair_rows:(pair + 1) * pair_rows],
                  jnp.concatenate([v_win[:, pair * LANES:(pair + 1) * LANES], ones], axis=1),
                  preferred_element_type=_F32) for pair in range(GROUP_HEADS // 2)]
    head_block = lambda hd: slice((hd % 2) * BLOCK, (hd % 2 + 1) * BLOCK)
    denom = _merge_heads(lambda hd, pair: pv[pair][head_block(hd), LANES:])
    o = _merge_heads(lambda hd, pair: pv[pair][head_block(hd), :LANES]) / denom
    m = _merge_heads(lambda hd, pair: jnp.broadcast_to(m[_head_rows(hd)], (BLOCK, LANES)))
    return o, m + jnp.log2(denom)


def _window(qb, prev_ref, cur_ref, seq=None):
    idx = (lambda rows: (rows, slice(None))) if seq is None else (lambda rows: (seq, rows, slice(None)))
    rows = slice(qb * BLOCK, (qb + 1) * BLOCK)
    if qb == 0:
        prev = prev_ref[idx(slice(0, BLOCK))]
    else:
        prev = cur_ref[idx(slice((qb - 1) * BLOCK, qb * BLOCK))]
    return jnp.concatenate([prev, cur_ref[idx(rows)]], axis=0)


def _dilated_body(in_refs, out_refs, scratch_refs, step, *, max_dist, tiles_per_seq):
    del scratch_refs
    q_ref, kprev_ref, k_ref, vprev_ref, v_ref = in_refs
    o_ref, lse_ref = out_refs
    n_seqs, tile, _ = q_ref.shape
    first_tile = (step % tiles_per_seq) == 0
    bias_first = _band_bias(max_dist, jnp.where(first_tile, BLOCK, 0))
    bias_inner = _band_bias(max_dist, 0)
    for seq in range(n_seqs):
        for qb in range(tile // BLOCK):
            rows = slice(qb * BLOCK, (qb + 1) * BLOCK)
            s = _stacked_scores(q_ref[seq, rows, :], _window(qb, kprev_ref, k_ref, seq))
            o, lse = _attend(s, bias_first if qb == 0 else bias_inner, _window(qb, vprev_ref, v_ref, seq))
            o_ref[seq, rows, :] = o.astype(o_ref.dtype)
            lse_ref[seq, rows, :] = lse


def _dilated_component(qkv, window, dilation):
    n_seqs, length, _ = qkv.shape
    tile = min(ATTN_TILE, length)
    seqs = ATTN_TILE // tile
    n_blocks = tile // BLOCK
    tiles_per_seq = length // tile

    def cur(col):
        return pl.BlockSpec((seqs, tile, GROUP_WIDTH), lambda i: (i // tiles_per_seq, i % tiles_per_seq, col))

    def prev(col):
        return pl.BlockSpec((seqs, BLOCK, GROUP_WIDTH),
                            lambda i: (i // tiles_per_seq, jnp.maximum((i % tiles_per_seq) * n_blocks - 1, 0), col))

    out_spec = pl.BlockSpec((seqs, tile, GROUP_WIDTH), lambda i: (i // tiles_per_seq, i % tiles_per_seq, 0))
    return _Component(
        inputs=[qkv] * 5,
        in_specs=[cur(0), prev(1), cur(1), prev(2), cur(2)],
        out_specs=[out_spec, out_spec],
        out_shapes=[jax.ShapeDtypeStruct((n_seqs, length, GROUP_WIDTH), _BF16),
                    jax.ShapeDtypeStruct((n_seqs, length, GROUP_WIDTH), _F32)],
        scratch_shapes=[],
        body=functools.partial(_dilated_body, max_dist=window // dilation, tiles_per_seq=tiles_per_seq))


def _swa_body(in_refs, out_refs, scratch_refs, step, *, tiles_per_seq):
    sink_ref, q_ref, kprev_ref, k_ref, vprev_ref, v_ref = in_refs
    (o_ref,) = out_refs
    kdup_ref, vdup_ref = scratch_refs
    tile = q_ref.shape[0]
    first_tile = (step % tiles_per_seq) == 0
    n_kv_heads = SWA_KV_WIDTH // HEAD_DIM

    def duplicate(dst_ref, row0, t):
        low = lax.broadcasted_iota(jnp.int32, t.shape, 1) < HEAD_DIM
        t32 = t.astype(_F32)
        swapped = pltpu.roll(t32, HEAD_DIM, 1)
        dst_ref[0, row0:row0 + t.shape[0], :] = jnp.where(low, t32, swapped).astype(_BF16)
        dst_ref[1, row0:row0 + t.shape[0], :] = jnp.where(low, swapped, t32).astype(_BF16)

    for dst_ref, prev_ref, cur_ref in ((kdup_ref, kprev_ref, k_ref), (vdup_ref, vprev_ref, v_ref)):
        duplicate(dst_ref, 0, prev_ref[...])
        duplicate(dst_ref, BLOCK, cur_ref[...])

    low = lax.broadcasted_iota(jnp.int32, (BLOCK, LANES), 1) < HEAD_DIM

    def scores(qb, kv):
        rows = slice(qb * BLOCK, (qb + 1) * BLOCK)
        parts = []
        for pair in range(GROUP_HEADS // 2):
            lane0 = kv * GROUP_WIDTH + pair * LANES
            q_pair = q_ref[rows, lane0:lane0 + LANES]
            zero = jnp.zeros_like(q_pair)
            parts += [jnp.where(low, q_pair, zero), jnp.where(low, zero, q_pair)]
        q_stack = jnp.concatenate(parts, axis=0)
        k_win = kdup_ref[kv, qb * BLOCK:(qb + 2) * BLOCK, :]
        return lax.dot_general(q_stack, k_win, _NT, preferred_element_type=_F32)

    sink_col = lax.broadcasted_iota(jnp.int32, (_STACK, 2 * BLOCK), 1) == 0
    bias_first = jnp.where(sink_col, 0.0, _band_bias(SWA_WINDOW - 1, jnp.where(first_tile, BLOCK, 0)))
    bias_inner = jnp.where(sink_col, 0.0, _band_bias(SWA_WINDOW - 1, 0))
    sink_lane = lax.broadcasted_iota(jnp.int32, (BLOCK, LANES), 1) == 0
    sink_row = lax.broadcasted_iota(jnp.int32, (2 * BLOCK, LANES), 0) == 0

    def finish(qb, kv, s):
        rows = slice(qb * BLOCK, (qb + 1) * BLOCK)
        s_left = jnp.concatenate([jnp.where(sink_lane, sink_ref[kv * GROUP_HEADS + hd], s[_head_rows(hd), :LANES])
                                  for hd in range(GROUP_HEADS)], axis=0)
        s = jnp.concatenate([s_left, s[:, LANES:]], axis=1) + (bias_first if qb == 0 else bias_inner)
        p = jnp.exp2(s - jnp.max(s, axis=1, keepdims=True)).astype(_BF16)
        v_win = jnp.where(sink_row, jnp.zeros((), _BF16), vdup_ref[kv, qb * BLOCK:(qb + 2) * BLOCK, :])
        pv = jnp.dot(p, jnp.concatenate([v_win, jnp.ones_like(v_win)], axis=1),
                     preferred_element_type=_F32)
        for pair in range(GROUP_HEADS // 2):
            lane0 = kv * GROUP_WIDTH + pair * LANES
            lo, hi = pv[_head_rows(2 * pair)], pv[_head_rows(2 * pair + 1)]
            o_pair = jnp.where(low, lo[:, :LANES], hi[:, :LANES]) / jnp.where(low, lo[:, LANES:], hi[:, LANES:])
            o_ref[rows, lane0:lane0 + LANES] = o_pair.astype(o_ref.dtype)

    for qb in range(tile // BLOCK):
        for kv in range(n_kv_heads):
            finish(qb, kv, scores(qb, kv))


def _swa_component(qkvb3, sinks):
    b, s, _ = qkvb3.shape
    tile = ATTN_TILE
    n_blocks = tile // BLOCK
    tiles_per_seq = s // tile
    k_col = SWA_Q_WIDTH // LANES
    v_col = k_col + SWA_KV_WIDTH // LANES

    def cur(col):
        return pl.BlockSpec((None, tile, LANES), lambda i: (i // tiles_per_seq, i % tiles_per_seq, col))

    def prev(col):
        return pl.BlockSpec((None, BLOCK, LANES),
                            lambda i: (i // tiles_per_seq, jnp.maximum((i % tiles_per_seq) * n_blocks - 1, 0), col))

    q_spec = pl.BlockSpec((None, tile, SWA_Q_WIDTH), lambda i: (i // tiles_per_seq, i % tiles_per_seq, 0))
    return _Component(
        inputs=[sinks] + [qkvb3] * 5,
        in_specs=[pl.BlockSpec(memory_space=pltpu.SMEM), q_spec, prev(k_col), cur(k_col), prev(v_col), cur(v_col)],
        out_specs=[q_spec],
        out_shapes=[jax.ShapeDtypeStruct((b, s, SWA_Q_WIDTH), _BF16)],
        scratch_shapes=[pltpu.VMEM((SWA_KV_WIDTH // HEAD_DIM, BLOCK + tile, LANES), _BF16)] * 2,
        body=functools.partial(_swa_body, tiles_per_seq=tiles_per_seq))


def _cast_body(in_refs, out_refs, scratch_refs, step):
    del scratch_refs, step
    for src_ref, dst_ref in zip(in_refs, out_refs):
        dst_ref[...] = src_ref[...].astype(dst_ref.dtype)


def _cast_component(weights, n_steps):
    specs = []
    for w in weights:
        rows, cols = w.shape
        block_rows = max(BF16_SUBLANES, rows // n_steps)
        assert rows % block_rows == 0 and rows // block_rows <= n_steps
        last = rows // block_rows - 1
        specs.append(pl.BlockSpec((block_rows, cols), lambda i, last=last: (jnp.minimum(i, last), 0)))
    return _Component(inputs=weights, in_specs=specs, out_specs=specs,
                      out_shapes=[jax.ShapeDtypeStruct(w.shape, _BF16) for w in weights],
                      scratch_shapes=[], body=_cast_body)


def _attention_components(proj, batch, seq, sinks2):
    qkv0, qkv1, qkv2, qkvb, _ = proj
    comps = []
    for (window, d), qkv in zip(DIL_PAIRS, (qkv0, qkv1, qkv2)):
        comps.append(_dilated_component(qkv.reshape(batch * d, seq // d, GROUP_QKV_WIDTH), window, d))
    comps.append(_swa_component(qkvb.reshape(batch, seq, QKV_B_WIDTH), sinks2))
    return comps


def _attention_outputs(outs, batch, seq):
    attn_a = []
    for (_, d), (o, lse) in zip(DIL_PAIRS, outs[:3]):
        shape = (batch * seq, GROUP_WIDTH) if d == 1 else (batch, d, seq // d, GROUP_WIDTH)
        attn_a.append((o.reshape(shape), lse.reshape(shape)))
    return attn_a, outs[3][0].reshape(batch * seq, SWA_Q_WIDTH)


def _out_mlp_body(in_refs, out_refs, scratch_refs, step):
    del step
    (x_ref, o0_ref, l0_ref, o1_ref, l1_ref, o2_ref, l2_ref, ob_ref, gate_ref,
     wa_ref, wb_ref, wo_ref, ln_ref, wup_ref, wdn_ref) = in_refs
    (out_ref,) = out_refs
    (slab_ref,) = scratch_refs
    tm = x_ref.shape[0]
    n_slabs = GROUP_WIDTH // LANES

    def interleaved(src_ref, base):
        d = src_ref.shape[0]
        for r in range(d):
            for slab in range(n_slabs):
                slab_ref[base + slab, pl.ds(r, tm // d, stride=d), :] = (
                    src_ref[r, :, slab * LANES:(slab + 1) * LANES].astype(_F32))
        return jnp.concatenate([slab_ref[base + slab] for slab in range(n_slabs)], axis=1)

    o0, l0 = o0_ref[...].astype(_F32), l0_ref[...]
    o1, l1 = interleaved(o1_ref, 0), interleaved(l1_ref, n_slabs)
    o2, l2 = interleaved(o2_ref, 2 * n_slabs), interleaved(l2_ref, 3 * n_slabs)
    m = jnp.maximum(jnp.maximum(l0, l1), l2)
    e0, e1, e2 = jnp.exp2(l0 - m), jnp.exp2(l1 - m), jnp.exp2(l2 - m)
    oa = ((e0 * o0 + e1 * o1 + e2 * o2) / (e0 + e1 + e2)).astype(_BF16)

    ya = jnp.dot(oa, wa_ref[...], preferred_element_type=_F32)
    yb = jnp.dot(ob_ref[...], wb_ref[...], preferred_element_type=_F32)
    gate_a = gate_ref[:, :D_MODEL].astype(_F32)
    gate_b = gate_ref[:, D_MODEL:].astype(_F32)
    mix = (gate_a * ya + gate_b * yb).astype(_BF16)
    x1 = x_ref[...] + jnp.dot(mix, wo_ref[...], preferred_element_type=_F32)

    ms = jnp.mean(x1 * x1, axis=-1, keepdims=True)
    h2 = (x1 * lax.rsqrt(ms + EPS) * ln_ref[...]).astype(_BF16)
    acc = x1
    for c in range(D_FF // FF_CHUNK):
        u = jnp.dot(h2, wup_ref[:, c * FF_CHUNK:(c + 1) * FF_CHUNK], preferred_element_type=_F32)
        a = jnp.square(jnp.maximum(u, 0.0)).astype(_BF16)
        acc = acc + jnp.dot(a, wdn_ref[c * FF_CHUNK:(c + 1) * FF_CHUNK, :], preferred_element_type=_F32)
    out_ref[...] = acc


def _out_mlp_component(x2, attn_a, ob, gates, wa, wb, wo, ln2, wup, wdn, seq):
    tm = TOKEN_TILE
    tiles_per_seq = seq // tm
    row = lambda width: pl.BlockSpec((tm, width), lambda i: (i, 0))
    full_row = row(D_MODEL)

    def deinterleaved(d):
        return pl.BlockSpec((None, d, tm // d, GROUP_WIDTH), lambda i: (i // tiles_per_seq, 0, i % tiles_per_seq, 0))

    group_specs = [row(GROUP_WIDTH)] * 2
    for _, d in DIL_PAIRS[1:]:
        group_specs += [deinterleaved(d)] * 2
    return _Component(
        inputs=[x2] + [a for pair in attn_a for a in pair] + [ob, gates, wa, wb, wo, ln2, wup, wdn],
        in_specs=[full_row] + group_specs + [row(SWA_Q_WIDTH), row(GATE_WIDTH),
                  _resident(wa.shape), _resident(wb.shape), _resident(wo.shape), _resident((1, D_MODEL)),
                  _resident(wup.shape), _resident(wdn.shape)],
        out_specs=[full_row],
        out_shapes=[jax.ShapeDtypeStruct(x2.shape, _F32)],
        scratch_shapes=[pltpu.VMEM((4 * GROUP_WIDTH // LANES, tm, LANES), _F32)],
        body=_out_mlp_body)


def _rope_inv_freq():
    return (ROPE_THETA ** (-jnp.arange(0, ROPE_DIM, 2, dtype=_F32) / ROPE_DIM))[:, None]


def kernel(x, positions, ln1_g, w_in, q_norm_a, k_norm_a, q_norm_b, k_norm_b, sinks,
           w_branch_a, w_branch_b, w_out, ln2_g, w_up, w_down):
    b, s, d_model = x.shape
    assert d_model == D_MODEL and ln1_g.shape[0] == 1 and s % (max(d for _, d in DIL_PAIRS) * BLOCK) == 0
    assert s % TOKEN_TILE == 0 and s % IN_PROJ_TILE == 0 and s % ATTN_TILE == 0
    n_tiles = b * s // TOKEN_TILE
    attn_steps = b * s // ATTN_TILE
    x2 = x.reshape(b * s, d_model)
    log2e = np.float32(np.log2(np.e))
    scale = np.float32(log2e / np.sqrt(HEAD_DIM))
    gains = jnp.stack([jnp.tile(q_norm_a[0] * scale, 2), jnp.tile(k_norm_a[0], 2),
                       jnp.tile(q_norm_b[0] * scale, 2), jnp.tile(k_norm_b[0], 2)])[:, None, :]
    head = np.arange(LANES) // HEAD_DIM
    seg = jnp.asarray(head[:, None] == head[None, :], _BF16)
    pos_rows = positions.reshape(b * s // IN_PROJ_TILE, 1, IN_PROJ_TILE)
    sinks2 = sinks[0] * log2e

    (proj,) = _run_components(
        [_in_proj_component(x2, pos_rows, ln1_g, w_in[0].astype(_BF16), gains, _rope_inv_freq(), seg, b, s)],
        b * s // IN_PROJ_TILE, "in_proj", IN_PROJ_VMEM_BYTES)
    mlp_weights = [w_branch_a[0], w_branch_b[0], w_out[0], w_up[0], w_down[0]]
    *attn_outs, mlp_weights = _run_components(
        _attention_components(proj, b, s, sinks2) + [_cast_component(mlp_weights, attn_steps)], attn_steps,
        "attention", ATTN_VMEM_BYTES)
    attn_a, ob = _attention_outputs(attn_outs, b, s)
    ((out,),) = _run_components(
        [_out_mlp_component(x2, attn_a, ob, proj[4], *mlp_weights[:3], ln2_g, *mlp_weights[3:], s)],
        n_tiles, "out_mlp", OUT_MLP_VMEM_BYTES)
    return out.reshape(b, s, d_model)
```

```python
import functools

import numpy as np
import jax
import jax.numpy as jnp
from jax import lax
from jax.experimental import pallas as pl
from jax.experimental.pallas import tpu as pltpu

D_MODEL = 1024
HEAD_DIM = 64
DIL_PAIRS = ((128, 1), (512, 4), (2048, 16))
GROUP_HEADS = 4
GROUP_WIDTH = GROUP_HEADS * HEAD_DIM
DIL_WIDTH = 768
N_DIL_GROUPS = DIL_WIDTH // GROUP_WIDTH
GROUP_QKV_WIDTH = 3 * GROUP_WIDTH
SWA_WINDOW = 128
SWA_Q_WIDTH = 512
SWA_KV_WIDTH = 128
QKV_A_WIDTH = 3 * DIL_WIDTH
QKV_B_WIDTH = SWA_Q_WIDTH + 2 * SWA_KV_WIDTH
GATE_WIDTH = 2 * D_MODEL
D_FF = 4 * D_MODEL
ROPE_THETA = 500000.0
ROPE_DIM = HEAD_DIM // 4
ROPE_HALF = ROPE_DIM // 2
BLOCK = 128
EPS = 1e-6
NEG = -1e30

LANES = 128
BF16_SUBLANES = 16
CHUNK = 256
FF_CHUNK = 512
TOKEN_TILE = 512
ATTN_TILE = 1024
IN_PROJ_TILE = 1024
PROJ_LOOKAHEAD = 2
IN_PROJ_VMEM_BYTES = 56 * 1024 * 1024
ATTN_VMEM_BYTES = 42 * 1024 * 1024
OUT_MLP_VMEM_BYTES = 48 * 1024 * 1024

_BF16 = jnp.bfloat16
_F32 = jnp.float32


def _resident(shape):
    return pl.BlockSpec(shape, lambda *_: (0,) * len(shape), pipeline_mode=pl.Buffered(1))


class _Component:
    def __init__(self, inputs, in_specs, out_specs, out_shapes, scratch_shapes, body):
        self.inputs, self.in_specs, self.out_specs = list(inputs), list(in_specs), list(out_specs)
        self.out_shapes, self.scratch_shapes, self.body = list(out_shapes), list(scratch_shapes), body


def _run_components(components, n_steps, name, vmem_limit_bytes):
    n_in = [len(c.inputs) for c in components]
    n_out = [len(c.out_specs) for c in components]
    n_scr = [len(c.scratch_shapes) for c in components]

    def body(*refs):
        refs = list(refs)
        ins, outs, scrs = [], [], []
        pos = 0
        for group, counts in ((ins, n_in), (outs, n_out), (scrs, n_scr)):
            for k in counts:
                group.append(refs[pos:pos + k])
                pos += k
        step = pl.program_id(0)
        for c, i, o, s in zip(components, ins, outs, scrs):
            c.body(i, o, s, step)

    outs = pl.pallas_call(
        body,
        grid=(n_steps,),
        in_specs=[s for c in components for s in c.in_specs],
        out_specs=[s for c in components for s in c.out_specs],
        out_shape=[s for c in components for s in c.out_shapes],
        scratch_shapes=[s for c in components for s in c.scratch_shapes],
        compiler_params=pltpu.CompilerParams(dimension_semantics=("arbitrary",), vmem_limit_bytes=vmem_limit_bytes),
        name=name,
    )(*[a for c in components for a in c.inputs])
    split, pos = [], 0
    for k in n_out:
        split.append(outs[pos:pos + k])
        pos += k
    return split


def _in_proj_body(in_refs, out_refs, scratch_refs, step):
    del step
    x_ref, pos_ref, ln_ref, w_ref, gain_ref, freq_ref, seg_ref = in_refs
    qkv0_ref, qkv1_ref, qkv2_ref, qkvb_ref, gate_ref = out_refs
    slab1_ref, slab2_ref = scratch_refs
    tm = x_ref.shape[0]
    x = x_ref[...]
    ms = jnp.mean(x * x, axis=-1, keepdims=True)
    h = (x * lax.rsqrt(ms + EPS) * ln_ref[...]).astype(_BF16)

    ang = freq_ref[...] * pos_ref[...].astype(_F32)
    packed = jnp.concatenate([jnp.cos(ang), jnp.sin(ang), jnp.ones_like(ang),
                              jnp.zeros((LANES - 3 * ROPE_HALF, tm), _F32)], axis=0).T
    lane = lax.broadcasted_iota(jnp.int32, (tm, LANES), 1) % HEAD_DIM
    rotary = lane < ROPE_DIM
    cos_t = jnp.take_along_axis(packed, jnp.where(rotary, lane % ROPE_HALF, 2 * ROPE_HALF), axis=1,
                                mode="promise_in_bounds")
    sin_t = jnp.take_along_axis(packed, jnp.where(rotary, ROPE_HALF + lane % ROPE_HALF, 3 * ROPE_HALF), axis=1,
                                mode="promise_in_bounds")
    first_half = lane < ROPE_HALF
    sin_from_hi = jnp.where(first_half, -sin_t, 0.0)
    sin_from_lo = jnp.where(first_half, 0.0, sin_t)
    seg = seg_ref[...]

    def norm_rope(y, gain):
        ss = jnp.dot((y * y).astype(_BF16), seg, preferred_element_type=_F32)
        yn = y * lax.rsqrt(ss * (1.0 / HEAD_DIM) + EPS) * gain
        up = pltpu.roll(yn, LANES - ROPE_HALF, 1)
        dn = pltpu.roll(yn, ROPE_HALF, 1)
        return yn * cos_t + up * sin_from_hi + dn * sin_from_lo

    def project(w_col):
        return jnp.dot(h, w_ref[:, w_col:w_col + CHUNK], preferred_element_type=_F32)

    tasks = []
    group_out = (qkv0_ref, slab1_ref, slab2_ref)

    def dilated_store(g, slab):
        def store(yh):
            if g == 0:
                qkv0_ref[:, slab * LANES:(slab + 1) * LANES] = yh.astype(_BF16)
            else:
                group_out[g][slab] = yh
        return store

    def row_store(out_ref, col):
        def store(yh):
            out_ref[:, col:col + LANES] = yh.astype(_BF16)
        return store

    def gate_store(col):
        def store(yh):
            gate_ref[:, col:col + LANES] = (0.5 * jnp.tanh(0.5 * yh) + 0.5).astype(_BF16)
        return store

    n_halves = CHUNK // LANES
    for part in range(3):
        for g in range(N_DIL_GROUPS):
            tasks.append((part * DIL_WIDTH + g * GROUP_WIDTH,
                          [(part if part < 2 else None, dilated_store(g, part * n_halves + i))
                           for i in range(n_halves)]))
    for c in range(QKV_B_WIDTH // CHUNK):
        cols = [c * CHUNK + i * LANES for i in range(n_halves)]
        tasks.append((QKV_A_WIDTH + c * CHUNK,
                      [(2 if col < SWA_Q_WIDTH else (3 if col < SWA_Q_WIDTH + SWA_KV_WIDTH else None),
                        row_store(qkvb_ref, col)) for col in cols]))
    for c in range(GATE_WIDTH // CHUNK):
        tasks.append((QKV_A_WIDTH + QKV_B_WIDTH + c * CHUNK,
                      [(None, gate_store(c * CHUNK + i * LANES)) for i in range(n_halves)]))

    def epilogue(y, stores):
        for i, (gain_idx, store) in enumerate(stores):
            yh = y[:, i * LANES:(i + 1) * LANES]
            store(yh if gain_idx is None else norm_rope(yh, gain_ref[gain_idx]))

    in_flight = []
    for w_col, stores in tasks:
        in_flight.append((project(w_col), stores))
        if len(in_flight) > PROJ_LOOKAHEAD:
            epilogue(*in_flight.pop(0))
    for y, stores in in_flight:
        epilogue(y, stores)

    for g, out_ref, slab_ref in ((1, qkv1_ref, slab1_ref), (2, qkv2_ref, slab2_ref)):
        d = DIL_PAIRS[g][1]
        for r in range(d):
            for slab in range(GROUP_QKV_WIDTH // LANES):
                rows = slab_ref[slab, pl.ds(r, tm // d, stride=d), :]
                out_ref[r, :, slab * LANES:(slab + 1) * LANES] = rows.astype(_BF16)


def _in_proj_component(x2, pos_rows, ln1, w_in, gains, freq, seg, batch, seq):
    tm = IN_PROJ_TILE
    n = batch * seq
    tiles_per_seq = seq // tm
    out_row = lambda width: pl.BlockSpec((tm, width), lambda i: (i, 0))

    def deinterleaved(d):
        return pl.BlockSpec((None, d, tm // d, GROUP_QKV_WIDTH),
                            lambda i: (i // tiles_per_seq, 0, i % tiles_per_seq, 0))

    d1, d2 = DIL_PAIRS[1][1], DIL_PAIRS[2][1]
    n_slabs = GROUP_QKV_WIDTH // LANES
    return _Component(
        inputs=[x2, pos_rows, ln1, w_in, gains, freq, seg],
        in_specs=[out_row(D_MODEL), pl.BlockSpec((None, 1, tm), lambda i: (i, 0, 0)),
                  _resident((1, D_MODEL)), _resident(w_in.shape), _resident(gains.shape), _resident(freq.shape),
                  _resident(seg.shape)],
        out_specs=[out_row(GROUP_QKV_WIDTH), deinterleaved(d1), deinterleaved(d2), out_row(QKV_B_WIDTH),
                   out_row(GATE_WIDTH)],
        out_shapes=[jax.ShapeDtypeStruct((n, GROUP_QKV_WIDTH), _BF16),
                    jax.ShapeDtypeStruct((batch, d1, seq // d1, GROUP_QKV_WIDTH), _BF16),
                    jax.ShapeDtypeStruct((batch, d2, seq // d2, GROUP_QKV_WIDTH), _BF16),
                    jax.ShapeDtypeStruct((n, QKV_B_WIDTH), _BF16),
                    jax.ShapeDtypeStruct((n, GATE_WIDTH), _BF16)],
        scratch_shapes=[pltpu.VMEM((n_slabs, tm, LANES), _F32), pltpu.VMEM((n_slabs, tm, LANES), _F32)],
        body=_in_proj_body)


_NT = (((1,), (1,)), ((), ()))
_STACK = GROUP_HEADS * BLOCK


def _stacked_scores(q, k_win):
    head_of_lane = lax.broadcasted_iota(jnp.int32, q.shape, 1) // HEAD_DIM
    zero = jnp.zeros_like(q)
    q_stack = jnp.concatenate([jnp.where(head_of_lane == hd, q, zero) for hd in range(GROUP_HEADS)], axis=0)
    return lax.dot_general(q_stack, k_win, _NT, preferred_element_type=_F32)


def _band_bias(max_dist, first_key_col):
    row = lax.broadcasted_iota(jnp.int32, (BLOCK, 2 * BLOCK), 0)
    col = lax.broadcasted_iota(jnp.int32, (BLOCK, 2 * BLOCK), 1)
    dist = row + BLOCK - col
    one = jnp.where((dist >= 0) & (dist <= max_dist) & (col >= first_key_col), 0.0, NEG).astype(_F32)
    return jnp.concatenate([one] * GROUP_HEADS, axis=0)


def _head_rows(hd):
    return slice(hd * BLOCK, (hd + 1) * BLOCK)


def _merge_heads(stack_col):
    low = lax.broadcasted_iota(jnp.int32, (BLOCK, LANES), 1) < HEAD_DIM
    return jnp.concatenate([jnp.where(low, stack_col(2 * pair, pair), stack_col(2 * pair + 1, pair))
                            for pair in range(GROUP_HEADS // 2)], axis=1)


def _attend(s, bias, v_win):
    s = s + bias
    m = jnp.max(s, axis=1, keepdims=True)
    p = jnp.exp2(s - m).astype(_BF16)
    pair_rows = 2 * BLOCK
    ones = jnp.ones((v_win.shape[0], LANES), _BF16)
    pv = [jnp.dot(p[pair * pair_rows:(pair + 1) * pair_rows],
                  jnp.concatenate([v_win[:, pair * LANES:(pair + 1) * LANES], ones], axis=1),
                  preferred_element_type=_F32) for pair in range(GROUP_HEADS // 2)]
    head_block = lambda hd: slice((hd % 2) * BLOCK, (hd % 2 + 1) * BLOCK)
    denom = _merge_heads(lambda hd, pair: pv[pair][head_block(hd), LANES:])
    o = _merge_heads(lambda hd, pair: pv[pair][head_block(hd), :LANES]) / denom
    m = _merge_heads(lambda hd, pair: jnp.broadcast_to(m[_head_rows(hd)], (BLOCK, LANES)))
    return o, m + jnp.log2(denom)


def _window(qb, prev_ref, cur_ref, seq=None):
    idx = (lambda rows: (rows, slice(None))) if seq is None else (lambda rows: (seq, rows, slice(None)))
    rows = slice(qb * BLOCK, (qb + 1) * BLOCK)
    if qb == 0:
        prev = prev_ref[idx(slice(0, BLOCK))]
    else:
        prev = cur_ref[idx(slice((qb - 1) * BLOCK, qb * BLOCK))]
    return jnp.concatenate([prev, cur_ref[idx(rows)]], axis=0)


def _dilated_body(in_refs, out_refs, scratch_refs, step, *, max_dist, tiles_per_seq):
    del scratch_refs
    q_ref, kprev_ref, k_ref, vprev_ref, v_ref = in_refs
    o_ref, lse_ref = out_refs
    n_seqs, tile, _ = q_ref.shape
    first_tile = (step % tiles_per_seq) == 0
    bias_first = _band_bias(max_dist, jnp.where(first_tile, BLOCK, 0))
    bias_inner = _band_bias(max_dist, 0)
    for seq in range(n_seqs):
        for qb in range(tile // BLOCK):
            rows = slice(qb * BLOCK, (qb + 1) * BLOCK)
            s = _stacked_scores(q_ref[seq, rows, :], _window(qb, kprev_ref, k_ref, seq))
            o, lse = _attend(s, bias_first if qb == 0 else bias_inner, _window(qb, vprev_ref, v_ref, seq))
            o_ref[seq, rows, :] = o.astype(o_ref.dtype)
            lse_ref[seq, rows, :] = lse


def _dilated_component(qkv, window, dilation):
    n_seqs, length, _ = qkv.shape
    tile = min(ATTN_TILE, length)
    seqs = ATTN_TILE // tile
    n_blocks = tile // BLOCK
    tiles_per_seq = length // tile

    def cur(col):
        return pl.BlockSpec((seqs, tile, GROUP_WIDTH), lambda i: (i // tiles_per_seq, i % tiles_per_seq, col))

    def prev(col):
        return pl.BlockSpec((seqs, BLOCK, GROUP_WIDTH),
                            lambda i: (i // tiles_per_seq, jnp.maximum((i % tiles_per_seq) * n_blocks - 1, 0), col))

    out_spec = pl.BlockSpec((seqs, tile, GROUP_WIDTH), lambda i: (i // tiles_per_seq, i % tiles_per_seq, 0))
    return _Component(
        inputs=[qkv] * 5,
        in_specs=[cur(0), prev(1), cur(1), prev(2), cur(2)],
        out_specs=[out_spec, out_spec],
        out_shapes=[jax.ShapeDtypeStruct((n_seqs, length, GROUP_WIDTH), _BF16),
                    jax.ShapeDtypeStruct((n_seqs, length, GROUP_WIDTH), _F32)],
        scratch_shapes=[],
        body=functools.partial(_dilated_body, max_dist=window // dilation, tiles_per_seq=tiles_per_seq))


def _swa_body(in_refs, out_refs, scratch_refs, step, *, tiles_per_seq):
    sink_ref, q_ref, kprev_ref, k_ref, vprev_ref, v_ref = in_refs
    (o_ref,) = out_refs
    kdup_ref, vdup_ref = scratch_refs
    tile = q_ref.shape[0]
    first_tile = (step % tiles_per_seq) == 0
    n_kv_heads = SWA_KV_WIDTH // HEAD_DIM

    def duplicate(dst_ref, row0, t):
        low = lax.broadcasted_iota(jnp.int32, t.shape, 1) < HEAD_DIM
        t32 = t.astype(_F32)
        swapped = pltpu.roll(t32, HEAD_DIM, 1)
        dst_ref[0, row0:row0 + t.shape[0], :] = jnp.where(low, t32, swapped).astype(_BF16)
        dst_ref[1, row0:row0 + t.shape[0], :] = jnp.where(low, swapped, t32).astype(_BF16)

    for dst_ref, prev_ref, cur_ref in ((kdup_ref, kprev_ref, k_ref), (vdup_ref, vprev_ref, v_ref)):
        duplicate(dst_ref, 0, prev_ref[...])
        duplicate(dst_ref, BLOCK, cur_ref[...])

    low = lax.broadcasted_iota(jnp.int32, (BLOCK, LANES), 1) < HEAD_DIM

    def scores(qb, kv):
        rows = slice(qb * BLOCK, (qb + 1) * BLOCK)
        parts = []
        for pair in range(GROUP_HEADS // 2):
            lane0 = kv * GROUP_WIDTH + pair * LANES
            q_pair = q_ref[rows, lane0:lane0 + LANES]
            zero = jnp.zeros_like(q_pair)
            parts += [jnp.where(low, q_pair, zero), jnp.where(low, zero, q_pair)]
        q_stack = jnp.concatenate([jnp.concatenate(parts, axis=0), row_onehot], axis=1)
        k_win = jnp.concatenate([kdup_ref[kv, qb * BLOCK:(qb + 2) * BLOCK, :],
                                 bias_t_first if qb == 0 else bias_t_inner], axis=1)
        return lax.dot_general(q_stack, k_win, _NT, preferred_element_type=_F32)

    def bias_t(first_key_col):
        key = lax.broadcasted_iota(jnp.int32, (2 * BLOCK, BLOCK), 0)
        dist = lax.broadcasted_iota(jnp.int32, (2 * BLOCK, BLOCK), 1) + BLOCK - key
        ok = ((dist >= 0) & (dist <= SWA_WINDOW - 1) & (key >= first_key_col)) | (key == 0)
        return jnp.where(ok, 0.0, NEG).astype(_BF16)

    bias_t_first = bias_t(jnp.where(first_tile, BLOCK, 0))
    bias_t_inner = bias_t(0)
    row_onehot = (lax.broadcasted_iota(jnp.int32, (_STACK, BLOCK), 0) % BLOCK
                  == lax.broadcasted_iota(jnp.int32, (_STACK, BLOCK), 1)).astype(_BF16)

    sink_lane = lax.broadcasted_iota(jnp.int32, (BLOCK, LANES), 1) == 0
    sink_row = lax.broadcasted_iota(jnp.int32, (2 * BLOCK, LANES), 0) == 0

    def finish(qb, kv, s):
        rows = slice(qb * BLOCK, (qb + 1) * BLOCK)
        s_left = jnp.concatenate([jnp.where(sink_lane, sink_ref[kv * GROUP_HEADS + hd], s[_head_rows(hd), :LANES])
                                  for hd in range(GROUP_HEADS)], axis=0)
        s = jnp.concatenate([s_left, s[:, LANES:]], axis=1)
        p = jnp.exp2(s - jnp.max(s, axis=1, keepdims=True)).astype(_BF16)
        v_win = jnp.where(sink_row, jnp.zeros((), _BF16), vdup_ref[kv, qb * BLOCK:(qb + 2) * BLOCK, :])
        pv = jnp.dot(p, jnp.concatenate([v_win, jnp.ones_like(v_win)], axis=1),
                     preferred_element_type=_F32)
        for pair in range(GROUP_HEADS // 2):
            lane0 = kv * GROUP_WIDTH + pair * LANES
            lo, hi = pv[_head_rows(2 * pair)], pv[_head_rows(2 * pair + 1)]
            o_pair = jnp.where(low, lo[:, :LANES], hi[:, :LANES]) / jnp.where(low, lo[:, LANES:], hi[:, LANES:])
            o_ref[rows, lane0:lane0 + LANES] = o_pair.astype(o_ref.dtype)

    for qb in range(tile // BLOCK):
        for kv in range(n_kv_heads):
            finish(qb, kv, scores(qb, kv))


def _swa_component(qkvb3, sinks):
    b, s, _ = qkvb3.shape
    tile = ATTN_TILE
    n_blocks = tile // BLOCK
    tiles_per_seq = s // tile
    k_col = SWA_Q_WIDTH // LANES
    v_col = k_col + SWA_KV_WIDTH // LANES

    def cur(col):
        return pl.BlockSpec((None, tile, LANES), lambda i: (i // tiles_per_seq, i % tiles_per_seq, col))

    def prev(col):
        return pl.BlockSpec((None, BLOCK, LANES),
                            lambda i: (i // tiles_per_seq, jnp.maximum((i % tiles_per_seq) * n_blocks - 1, 0), col))

    q_spec = pl.BlockSpec((None, tile, SWA_Q_WIDTH), lambda i: (i // tiles_per_seq, i % tiles_per_seq, 0))
    return _Component(
        inputs=[sinks] + [qkvb3] * 5,
        in_specs=[pl.BlockSpec(memory_space=pltpu.SMEM), q_spec, prev(k_col), cur(k_col), prev(v_col), cur(v_col)],
        out_specs=[q_spec],
        out_shapes=[jax.ShapeDtypeStruct((b, s, SWA_Q_WIDTH), _BF16)],
        scratch_shapes=[pltpu.VMEM((SWA_KV_WIDTH // HEAD_DIM, BLOCK + tile, LANES), _BF16)] * 2,
        body=functools.partial(_swa_body, tiles_per_seq=tiles_per_seq))


def _cast_body(in_refs, out_refs, scratch_refs, step):
    del scratch_refs, step
    for src_ref, dst_ref in zip(in_refs, out_refs):
        dst_ref[...] = src_ref[...].astype(dst_ref.dtype)


def _cast_component(weights, n_steps):
    specs = []
    for w in weights:
        rows, cols = w.shape
        block_rows = max(BF16_SUBLANES, rows // n_steps)
        assert rows % block_rows == 0 and rows // block_rows <= n_steps
        last = rows // block_rows - 1
        specs.append(pl.BlockSpec((block_rows, cols), lambda i, last=last: (jnp.minimum(i, last), 0)))
    return _Component(inputs=weights, in_specs=specs, out_specs=specs,
                      out_shapes=[jax.ShapeDtypeStruct(w.shape, _BF16) for w in weights],
                      scratch_shapes=[], body=_cast_body)


def _attention_components(proj, batch, seq, sinks2):
    qkv0, qkv1, qkv2, qkvb, _ = proj
    comps = []
    for (window, d), qkv in zip(DIL_PAIRS, (qkv0, qkv1, qkv2)):
        comps.append(_dilated_component(qkv.reshape(batch * d, seq // d, GROUP_QKV_WIDTH), window, d))
    comps.append(_swa_component(qkvb.reshape(batch, seq, QKV_B_WIDTH), sinks2))
    return comps


def _attention_outputs(outs, batch, seq):
    attn_a = []
    for (_, d), (o, lse) in zip(DIL_PAIRS, outs[:3]):
        shape = (batch * seq, GROUP_WIDTH) if d == 1 else (batch, d, seq // d, GROUP_WIDTH)
        attn_a.append((o.reshape(shape), lse.reshape(shape)))
    return attn_a, outs[3][0].reshape(batch * seq, SWA_Q_WIDTH)


def _out_mlp_body(in_refs, out_refs, scratch_refs, step):
    del step
    (x_ref, o0_ref, l0_ref, o1_ref, l1_ref, o2_ref, l2_ref, ob_ref, gate_ref,
     wa_ref, wb_ref, wo_ref, ln_ref, wup_ref, wdn_ref) = in_refs
    (out_ref,) = out_refs
    (slab_ref,) = scratch_refs
    tm = x_ref.shape[0]
    n_slabs = GROUP_WIDTH // LANES

    def interleaved(src_ref, base):
        d = src_ref.shape[0]
        for r in range(d):
            for slab in range(n_slabs):
                slab_ref[base + slab, pl.ds(r, tm // d, stride=d), :] = (
                    src_ref[r, :, slab * LANES:(slab + 1) * LANES].astype(_F32))
        return jnp.concatenate([slab_ref[base + slab] for slab in range(n_slabs)], axis=1)

    o0, l0 = o0_ref[...].astype(_F32), l0_ref[...]
    o1, l1 = interleaved(o1_ref, 0), interleaved(l1_ref, n_slabs)
    o2, l2 = interleaved(o2_ref, 2 * n_slabs), interleaved(l2_ref, 3 * n_slabs)
    m = jnp.maximum(jnp.maximum(l0, l1), l2)
    e0, e1, e2 = jnp.exp2(l0 - m), jnp.exp2(l1 - m), jnp.exp2(l2 - m)
    oa = ((e0 * o0 + e1 * o1 + e2 * o2) / (e0 + e1 + e2)).astype(_BF16)

    ya = jnp.dot(oa, wa_ref[...], preferred_element_type=_F32)
    yb = jnp.dot(ob_ref[...], wb_ref[...], preferred_element_type=_F32)
    gate_a = gate_ref[:, :D_MODEL].astype(_F32)
    gate_b = gate_ref[:, D_MODEL:].astype(_F32)
    mix = (gate_a * ya + gate_b * yb).astype(_BF16)
    x1 = x_ref[...] + jnp.dot(mix, wo_ref[...], preferred_element_type=_F32)

    ms = jnp.mean(x1 * x1, axis=-1, keepdims=True)
    h2 = (x1 * lax.rsqrt(ms + EPS) * ln_ref[...]).astype(_BF16)
    acc = x1
    for c in range(D_FF // FF_CHUNK):
        u = jnp.dot(h2, wup_ref[:, c * FF_CHUNK:(c + 1) * FF_CHUNK], preferred_element_type=_F32)
        a = jnp.square(jnp.maximum(u, 0.0)).astype(_BF16)
        acc = acc + jnp.dot(a, wdn_ref[c * FF_CHUNK:(c + 1) * FF_CHUNK, :], preferred_element_type=_F32)
    out_ref[...] = acc


def _out_mlp_component(x2, attn_a, ob, gates, wa, wb, wo, ln2, wup, wdn, seq):
    tm = TOKEN_TILE
    tiles_per_seq = seq // tm
    row = lambda width: pl.BlockSpec((tm, width), lambda i: (i, 0))
    full_row = row(D_MODEL)

    def deinterleaved(d):
        return pl.BlockSpec((None, d, tm // d, GROUP_WIDTH), lambda i: (i // tiles_per_seq, 0, i % tiles_per_seq, 0))

    group_specs = [row(GROUP_WIDTH)] * 2
    for _, d in DIL_PAIRS[1:]:
        group_specs += [deinterleaved(d)] * 2
    return _Component(
        inputs=[x2] + [a for pair in attn_a for a in pair] + [ob, gates, wa, wb, wo, ln2, wup, wdn],
        in_specs=[full_row] + group_specs + [row(SWA_Q_WIDTH), row(GATE_WIDTH),
                  _resident(wa.shape), _resident(wb.shape), _resident(wo.shape), _resident((1, D_MODEL)),
                  _resident(wup.shape), _resident(wdn.shape)],
        out_specs=[full_row],
        out_shapes=[jax.ShapeDtypeStruct(x2.shape, _F32)],
        scratch_shapes=[pltpu.VMEM((4 * GROUP_WIDTH // LANES, tm, LANES), _F32)],
        body=_out_mlp_body)


def _rope_inv_freq():
    return (ROPE_THETA ** (-jnp.arange(0, ROPE_DIM, 2, dtype=_F32) / ROPE_DIM))[:, None]


def kernel(x, positions, ln1_g, w_in, q_norm_a, k_norm_a, q_norm_b, k_norm_b, sinks,
           w_branch_a, w_branch_b, w_out, ln2_g, w_up, w_down):
    b, s, d_model = x.shape
    assert d_model == D_MODEL and ln1_g.shape[0] == 1 and s % (max(d for _, d in DIL_PAIRS) * BLOCK) == 0
    assert s % TOKEN_TILE == 0 and s % IN_PROJ_TILE == 0 and s % ATTN_TILE == 0
    n_tiles = b * s // TOKEN_TILE
    attn_steps = b * s // ATTN_TILE
    x2 = x.reshape(b * s, d_model)
    log2e = np.float32(np.log2(np.e))
    scale = np.float32(log2e / np.sqrt(HEAD_DIM))
    gains = jnp.stack([jnp.tile(q_norm_a[0] * scale, 2), jnp.tile(k_norm_a[0], 2),
                       jnp.tile(q_norm_b[0] * scale, 2), jnp.tile(k_norm_b[0], 2)])[:, None, :]
    head = np.arange(LANES) // HEAD_DIM
    seg = jnp.asarray(head[:, None] == head[None, :], _BF16)
    pos_rows = positions.reshape(b * s // IN_PROJ_TILE, 1, IN_PROJ_TILE)
    sinks2 = sinks[0] * log2e

    (proj,) = _run_components(
        [_in_proj_component(x2, pos_rows, ln1_g, w_in[0].astype(_BF16), gains, _rope_inv_freq(), seg, b, s)],
        b * s // IN_PROJ_TILE, "in_proj", IN_PROJ_VMEM_BYTES)
    mlp_weights = [w_branch_a[0], w_branch_b[0], w_out[0], w_up[0], w_down[0]]
    *attn_outs, mlp_weights = _run_components(
        _attention_components(proj, b, s, sinks2) + [_cast_component(mlp_weights, attn_steps)], attn_steps,
        "attention", ATTN_VMEM_BYTES)
    attn_a, ob = _attention_outputs(attn_outs, b, s)
    ((out,),) = _run_components(
        [_out_mlp_component(x2, attn_a, ob, proj[4], *mlp_weights[:3], ln2_g, *mlp_weights[3:], s)],
        n_tiles, "out_mlp", OUT_MLP_VMEM_BYTES)
    return out.reshape(b, s, d_model)
```
